```python
import jax
import jax.numpy as jnp
from jax import lax
import numpy as np

D_MODEL = 1024
BATCH = 8
SEQ = 2048
DEPTH = 1
DEC_BATCH = 128
DEC_SEQ = 1
PAST_LEN = 16384
PAGE_SIZE = 128

GDN_HEADS = 4
GDN_DK = 128
GDN_DV = 128
CONV_WIDTH = 4
GDN_CHUNK = 64
MLA_HEADS = 4
MLA_NOPE = 128
MLA_ROPE = 64
MLA_V = 128
MLA_Q_RANK = 384
MLA_KV_RANK = 256
ROPE_THETA = 10000.0
Q_BLOCK = 128
MLA_SCALE = (MLA_NOPE + MLA_ROPE) ** -0.5
MEM_TOKENS = 256
X_HEADS = 4
X_HEAD_DIM = D_MODEL // X_HEADS
N_EXPERTS = 32
TOP_K = 4
D_FF = D_MODEL
SWIGLU_LIMIT = 7.0
SWIGLU_ALPHA = 1.702
NORM_EPS = 1e-6

GDN_QK_W = GDN_HEADS * GDN_DK
GDN_V_W = GDN_HEADS * GDN_DV
CONV_CH = 2 * GDN_QK_W + GDN_V_W
MLA_OUT_W = MLA_HEADS * MLA_V
MIX_WIDTH = GDN_V_W + MLA_OUT_W
IN_SPLITS = (CONV_CH, GDN_V_W, GDN_HEADS, GDN_HEADS, MLA_Q_RANK, MLA_KV_RANK, MLA_ROPE)
IN_COLS = CONV_CH + GDN_V_W + 2 * GDN_HEADS + MLA_Q_RANK + MLA_KV_RANK + MLA_ROPE

kernel_name = 'hybrid_gdn_mla_xmem_moe_step'


def rms_norm(x, gain):
    xf = x.astype(jnp.float32)
    y = xf * lax.rsqrt(jnp.mean(xf * xf, axis=-1, keepdims=True) + NORM_EPS)
    return (y * gain.astype(jnp.float32)).astype(x.dtype)


def l2_normalize(x):
    return x * lax.rsqrt(jnp.sum(x * x, axis=-1, keepdims=True) + NORM_EPS)


def apply_rope(x, pos):
    half = MLA_ROPE // 2
    inv_freq = 1.0 / (ROPE_THETA ** (jnp.arange(half, dtype=jnp.float32) / half))
    ang = pos.astype(jnp.float32)[:, None] * inv_freq[None, :]
    cos = jnp.cos(ang)[None, :, None, :]
    sin = jnp.sin(ang)[None, :, None, :]
    xf = x.astype(jnp.float32)
    x1, x2 = xf[..., :half], xf[..., half:]
    return jnp.concatenate([x1 * cos - x2 * sin, x2 * cos + x1 * sin], axis=-1).astype(x.dtype)


def split_projection(xn, w_in):
    offs = np.cumsum(IN_SPLITS)[:-1].tolist()
    return jnp.split(xn @ w_in, offs, axis=-1)


def causal_short_conv(xh, w):
    t_new = xh.shape[1] - (CONV_WIDTH - 1)
    out = xh[:, 0:t_new] * w[0]
    for j in range(1, CONV_WIDTH):
        out = out + xh[:, j:j + t_new] * w[j]
    return jax.nn.silu(out)


def gdn_gates(conv_out, b_logit, a_logit, a_log, dt_bias):
    B, T, _ = conv_out.shape
    cf = conv_out.astype(jnp.float32)
    q = l2_normalize(cf[..., :GDN_QK_W].reshape(B, T, GDN_HEADS, GDN_DK)) * (GDN_DK ** -0.5)
    k = l2_normalize(cf[..., GDN_QK_W:2 * GDN_QK_W].reshape(B, T, GDN_HEADS, GDN_DK))
    v = cf[..., 2 * GDN_QK_W:].reshape(B, T, GDN_HEADS, GDN_DV)
    beta = jax.nn.sigmoid(b_logit.astype(jnp.float32))
    g = -jnp.exp(a_log.astype(jnp.float32)) * jax.nn.softplus(a_logit.astype(jnp.float32) + dt_bias.astype(jnp.float32))
    return q, k, v, g, beta


def gdn_chunked(q, k, v, g, beta):
    B, T, H, DK = q.shape
    DV = v.shape[-1]
    C = GDN_CHUNK
    N = T // C

    def chunks(t):
        return jnp.moveaxis(t.reshape((B, N, C) + t.shape[2:]), 2, 3)

    qc, kc, vc, gc, bc = (chunks(t) for t in (q, k, v, g, beta))
    gc = jnp.cumsum(gc, axis=-1)
    incl = jnp.tril(jnp.ones((C, C), dtype=bool))
    strict = jnp.tril(jnp.ones((C, C), dtype=bool), -1)
    decay = jnp.exp(jnp.where(incl, gc[..., :, None] - gc[..., None, :], -jnp.inf))
    kb = kc * bc[..., None]
    vb = vc * bc[..., None]
    lower = jnp.where(strict, jnp.einsum('bnhcd,bnhsd->bnhcs', kb, kc) * decay, 0.0)
    eye = jnp.eye(C, dtype=jnp.float32)
    wy = lax.linalg.triangular_solve(eye + lower, jnp.broadcast_to(eye, lower.shape),
                                     left_side=True, lower=True, unit_diagonal=True)
    u = jnp.einsum('bnhcs,bnhse->bnhce', wy, vb)
    w = jnp.einsum('bnhcs,bnhsd->bnhcd', wy, kb * jnp.exp(gc)[..., None])
    intra = jnp.einsum('bnhcd,bnhsd->bnhcs', qc, kc) * decay

    def step(S, xs):
        q_i, k_i, u_i, w_i, g_i, a_i = xs
        v_new = u_i - jnp.einsum('bhcd,bhde->bhce', w_i, S)
        o_i = (jnp.einsum('bhcd,bhde->bhce', q_i * jnp.exp(g_i)[..., None], S)
               + jnp.einsum('bhcs,bhse->bhce', a_i, v_new))
        g_last = g_i[..., -1:]
        S = (S * jnp.exp(g_last)[..., None]
             + jnp.einsum('bhcd,bhce->bhde', k_i * jnp.exp(g_last - g_i)[..., None], v_new))
        return S, o_i

    S0 = jnp.zeros((B, H, DK, DV), jnp.float32)
    xs = tuple(jnp.moveaxis(t, 1, 0) for t in (qc, kc, u, w, gc, intra))
    S, o = lax.scan(step, S0, xs)
    return S, jnp.transpose(o, (1, 0, 3, 2, 4)).reshape(B, T, H, DV)


def gdn_recurrent(S, q, k, v, g, beta):
    def step(S, xs):
        q_t, k_t, v_t, g_t, b_t = xs
        S = S * jnp.exp(g_t)[..., None, None]
        pred = jnp.einsum('bhd,bhde->bhe', k_t, S)
        S = S + jnp.einsum('bhd,bhe->bhde', k_t, (v_t - pred) * b_t[..., None])
        return S, jnp.einsum('bhd,bhde->bhe', q_t, S)

    S, o = lax.scan(step, S, tuple(jnp.moveaxis(t, 1, 0) for t in (q, k, v, g, beta)))
    return S, jnp.moveaxis(o, 0, 1)


def gdn_output(o, z, gdn_norm):
    B, T = z.shape[:2]
    zz = z.astype(jnp.float32).reshape(B, T, GDN_HEADS, GDN_DV)
    return (rms_norm(o, gdn_norm) * jax.nn.silu(zz)).reshape(B, T, GDN_V_W).astype(z.dtype)


def mla_project(c_q, c_kv, k_pe, pos, q_norm, w_uq, kv_norm):
    q = jnp.einsum('btr,rhd->bthd', rms_norm(c_q, q_norm), w_uq)
    q_nope = q[..., :MLA_NOPE]
    q_pe = apply_rope(q[..., MLA_NOPE:], pos)
    lat = rms_norm(c_kv, kv_norm)
    kpe = apply_rope(k_pe[:, :, None, :], pos)[:, :, 0, :]
    return q_nope, q_pe, lat, kpe


def mla_prompt(q_nope, q_pe, lat, kpe, w_uk, w_uv):
    B, T, H, _ = q_nope.shape
    k_nope = jnp.einsum('bsr,rhd->bshd', lat, w_uk)
    v = jnp.einsum('bsr,rhd->bshd', lat, w_uv)
    nb = T // Q_BLOCK
    qn_b = jnp.moveaxis(q_nope.reshape(B, nb, Q_BLOCK, H, MLA_NOPE), 1, 0)
    qp_b = jnp.moveaxis(q_pe.reshape(B, nb, Q_BLOCK, H, MLA_ROPE), 1, 0)
    kpos = jnp.arange(T)

    def block(args):
        i, qn, qp = args
        s = (jnp.einsum('bqhd,bshd->bhqs', qn, k_nope)
             + jnp.einsum('bqhd,bsd->bhqs', qp, kpe)).astype(jnp.float32) * MLA_SCALE
        qpos = i * Q_BLOCK + jnp.arange(Q_BLOCK)
        s = jnp.where(kpos[None, :] <= qpos[:, None], s, -jnp.inf)
        p = jax.nn.softmax(s, axis=-1).astype(v.dtype)
        return jnp.einsum('bhqs,bshd->bqhd', p, v)

    o = lax.map(block, (jnp.arange(nb), qn_b, qp_b))
    return jnp.moveaxis(o, 0, 1).reshape(B, T, H * MLA_V)


def mla_sample(q_nope, q_pe, lat_new, kpe_new, cache_latent, cache_krope, page_table, layer, w_uk, w_uv):
    B, T, H, _ = q_nope.shape
    q_lat = jnp.einsum('bthd,rhd->bthr', q_nope, w_uk)

    def scores(lat, kpe):
        return (jnp.einsum('bthr,bsr->bhts', q_lat, lat)
                + jnp.einsum('bthd,bsd->bhts', q_pe, kpe)).astype(jnp.float32) * MLA_SCALE

    s_new = jnp.where(jnp.tril(jnp.ones((T, T), dtype=bool)), scores(lat_new, kpe_new), -jnp.inf)
    m0 = jnp.max(s_new, axis=-1)
    p0 = jnp.exp(s_new - m0[..., None])
    l0 = jnp.sum(p0, axis=-1)
    acc0 = jnp.einsum('bhts,bsr->bhtr', p0, lat_new.astype(jnp.float32))

    def page_step(carry, pages):
        m, l, acc = carry
        lat = cache_latent[pages, layer]
        kpe = cache_krope[pages, layer]
        s = scores(lat, kpe)
        m_new = jnp.maximum(m, jnp.max(s, axis=-1))
        corr = jnp.exp(m - m_new)
        p = jnp.exp(s - m_new[..., None])
        l = l * corr + jnp.sum(p, axis=-1)
        acc = acc * corr[..., None] + jnp.einsum('bhts,bsr->bhtr', p, lat.astype(jnp.float32))
        return (m_new, l, acc), None

    (m, l, acc), _ = lax.scan(page_step, (m0, l0, acc0), page_table.T)
    o_lat = (acc / l[..., None]).astype(q_nope.dtype)
    o = jnp.einsum('bhtr,rhd->bthd', o_lat, w_uv)
    return o.reshape(B, T, H * MLA_V)


def memory_kv(mem, mem_norm, w_xk, w_xv):
    B, M, _ = mem.shape
    mn = rms_norm(mem, mem_norm)
    return ((mn @ w_xk).reshape(B, M, X_HEADS, X_HEAD_DIM),
            (mn @ w_xv).reshape(B, M, X_HEADS, X_HEAD_DIM))


def cross_attend(hn, mem_k, mem_v, w_xq, w_xo):
    B, T, _ = hn.shape
    q = (hn @ w_xq).reshape(B, T, X_HEADS, X_HEAD_DIM)
    s = jnp.einsum('bthd,bmhd->bhtm', q, mem_k).astype(jnp.float32) * (X_HEAD_DIM ** -0.5)
    p = jax.nn.softmax(s, axis=-1).astype(mem_v.dtype)
    o = jnp.einsum('bhtm,bmhd->bthd', p, mem_v).reshape(B, T, X_HEADS * X_HEAD_DIM)
    return o @ w_xo


def moe_ffn(h, w_router, b_router, w_e1, b_e1, w_e2, b_e2):
    B, T, D = h.shape
    xt = h.reshape(B * T, D)
    logits = (xt @ w_router + b_router).astype(jnp.float32)
    top_v, top_i = lax.top_k(logits, TOP_K)
    gates = jax.nn.softmax(top_v, axis=-1)
    gate_w = jnp.sum(jax.nn.one_hot(top_i, N_EXPERTS, dtype=jnp.float32) * gates[..., None], axis=1)

    def expert(acc, ew):
        w1, b1, w2, b2, g_col = ew
        hh = xt @ w1 + b1
        glu = jnp.minimum(hh[:, 0::2], SWIGLU_LIMIT)
        lin = jnp.clip(hh[:, 1::2], -SWIGLU_LIMIT, SWIGLU_LIMIT)
        y = ((lin + 1.0) * (glu * jax.nn.sigmoid(SWIGLU_ALPHA * glu))) @ w2 + b2
        return acc + g_col[:, None].astype(y.dtype) * y, None

    out, _ = lax.scan(expert, jnp.zeros_like(xt), (w_e1, b_e1, w_e2, b_e2, gate_w.T))
    return out.reshape(B, T, D)


def layer_tail(h, mem_k, mem_v, norm_x, w_xq, w_xo, norm_ffn, w_router, b_router, w_e1, b_e1, w_e2, b_e2):
    h = h + cross_attend(rms_norm(h, norm_x), mem_k, mem_v, w_xq, w_xo)
    return h + moe_ffn(rms_norm(h, norm_ffn), w_router, b_router, w_e1, b_e1, w_e2, b_e2)


def prompt_layer(x, mem, mem_norm, w_xk, w_xv, norm_mix, w_in, conv_w, a_log, dt_bias, gdn_norm,
                 q_norm, w_uq, kv_norm, w_uk, w_uv, w_out, norm_x, w_xq, w_xo, norm_ffn,
                 w_router, b_router, w_e1, b_e1, w_e2, b_e2):
    B, T, _ = x.shape
    conv_in, z, b_logit, a_logit, c_q, c_kv, k_pe = split_projection(rms_norm(x, norm_mix), w_in)
    conv_hist = jnp.concatenate([jnp.zeros((B, CONV_WIDTH - 1, CONV_CH), conv_in.dtype), conv_in], axis=1)
    q, k, v, g, beta = gdn_gates(causal_short_conv(conv_hist, conv_w), b_logit, a_logit, a_log, dt_bias)
    S, o = gdn_chunked(q, k, v, g, beta)
    y_gdn = gdn_output(o, z, gdn_norm)
    pos = jnp.arange(T, dtype=jnp.int32)
    q_nope, q_pe, lat, kpe = mla_project(c_q, c_kv, k_pe, pos, q_norm, w_uq, kv_norm)
    y_mla = mla_prompt(q_nope, q_pe, lat, kpe, w_uk, w_uv)
    h = x + jnp.concatenate([y_gdn, y_mla], axis=-1) @ w_out
    mem_k, mem_v = memory_kv(mem, mem_norm, w_xk, w_xv)
    y = layer_tail(h, mem_k, mem_v, norm_x, w_xq, w_xo, norm_ffn, w_router, b_router, w_e1, b_e1, w_e2, b_e2)
    return y, S.astype(x.dtype), conv_hist[:, -(CONV_WIDTH - 1):], lat, kpe, mem_k, mem_v


def sample_layer(x, layer, state_gdn, state_conv, cache_latent, cache_krope, page_table, mem_k, mem_v,
                 norm_mix, w_in, conv_w, a_log, dt_bias, gdn_norm, q_norm, w_uq, kv_norm, w_uk, w_uv,
                 w_out, norm_x, w_xq, w_xo, norm_ffn, w_router, b_router, w_e1, b_e1, w_e2, b_e2):
    B, T, _ = x.shape
    conv_in, z, b_logit, a_logit, c_q, c_kv, k_pe = split_projection(rms_norm(x, norm_mix), w_in)
    conv_hist = jnp.concatenate([state_conv.astype(conv_in.dtype), conv_in], axis=1)
    q, k, v, g, beta = gdn_gates(causal_short_conv(conv_hist, conv_w), b_logit, a_logit, a_log, dt_bias)
    S, o = gdn_recurrent(state_gdn.astype(jnp.float32), q, k, v, g, beta)
    y_gdn = gdn_output(o, z, gdn_norm)
    pos = PAST_LEN + jnp.arange(T, dtype=jnp.int32)
    q_nope, q_pe, lat, kpe = mla_project(c_q, c_kv, k_pe, pos, q_norm, w_uq, kv_norm)
    y_mla = mla_sample(q_nope, q_pe, lat, kpe, cache_latent, cache_krope, page_table, layer, w_uk, w_uv)
    h = x + jnp.concatenate([y_gdn, y_mla], axis=-1) @ w_out
    y = layer_tail(h, mem_k, mem_v, norm_x, w_xq, w_xo, norm_ffn, w_router, b_router, w_e1, b_e1, w_e2, b_e2)
    return y, S.astype(state_gdn.dtype), conv_hist[:, -(CONV_WIDTH - 1):], lat, kpe


def setup_inputs(seed: int = 0) -> dict:
    key = jax.random.key(seed)
    ks = iter(jax.random.split(key, 48))
    f32 = jnp.float32

    def nrm(shape, scale):
        return jax.random.normal(next(ks), shape, f32) * scale

    def gain(shape):
        return 1.0 + nrm(shape, 0.02)

    L = DEPTH
    n_pages = PAST_LEN // PAGE_SIZE
    n_used = DEC_BATCH * n_pages
    n_pool = n_used + max(1, n_used // 4)
    page_table = jax.random.permutation(next(ks), n_pool)[:n_used].reshape(DEC_BATCH, n_pages).astype(jnp.int32)
    dt = jnp.exp(jax.random.uniform(next(ks), (L, GDN_HEADS), f32, float(np.log(1e-3)), float(np.log(1e-1))))
    xw = X_HEADS * X_HEAD_DIM
    return {
        'x_prompt': nrm((BATCH, SEQ, D_MODEL), 1.0),
        'x_sample': nrm((DEC_BATCH, DEC_SEQ, D_MODEL), 1.0),
        'state_gdn': nrm((L, DEC_BATCH, GDN_HEADS, GDN_DK, GDN_DV), 0.3),
        'state_conv': nrm((L, DEC_BATCH, CONV_WIDTH - 1, CONV_CH), 1.0),
        'cache_latent': nrm((n_pool, L, PAGE_SIZE, MLA_KV_RANK), 1.0),
        'cache_krope': nrm((n_pool, L, PAGE_SIZE, MLA_ROPE), 1.0),
        'page_table': page_table,
        'cache_mem_k': nrm((L, DEC_BATCH, MEM_TOKENS, X_HEADS, X_HEAD_DIM), 1.0),
        'cache_mem_v': nrm((L, DEC_BATCH, MEM_TOKENS, X_HEADS, X_HEAD_DIM), 1.0),
        'mem_prompt': nrm((BATCH, MEM_TOKENS, D_MODEL), 1.0),
        'norm_mix': gain((L, D_MODEL)),
        'w_in': nrm((L, D_MODEL, IN_COLS), D_MODEL ** -0.5),
        'conv_w': nrm((L, CONV_WIDTH, CONV_CH), CONV_WIDTH ** -0.5),
        'a_log': jnp.log(jax.random.uniform(next(ks), (L, GDN_HEADS), f32, 1.0, 16.0)),
        'dt_bias': dt + jnp.log(-jnp.expm1(-dt)),
        'gdn_norm': gain((L, GDN_DV)),
        'q_norm': gain((L, MLA_Q_RANK)),
        'w_uq': nrm((L, MLA_Q_RANK, MLA_HEADS, MLA_NOPE + MLA_ROPE), MLA_Q_RANK ** -0.5),
        'kv_norm': gain((L, MLA_KV_RANK)),
        'w_uk': nrm((L, MLA_KV_RANK, MLA_HEADS, MLA_NOPE), MLA_KV_RANK ** -0.5),
        'w_uv': nrm((L, MLA_KV_RANK, MLA_HEADS, MLA_V), MLA_KV_RANK ** -0.5),
        'w_out': nrm((L, MIX_WIDTH, D_MODEL), MIX_WIDTH ** -0.5),
        'norm_x': gain((L, D_MODEL)),
        'mem_norm': gain((L, D_MODEL)),
        'w_xq': nrm((L, D_MODEL, xw), D_MODEL ** -0.5),
        'w_xk': nrm((L, D_MODEL, xw), D_MODEL ** -0.5),
        'w_xv': nrm((L, D_MODEL, xw), D_MODEL ** -0.5),
        'w_xo': nrm((L, xw, D_MODEL), xw ** -0.5),
        'norm_ffn': gain((L, D_MODEL)),
        'w_router': nrm((L, D_MODEL, N_EXPERTS), D_MODEL ** -0.5),
        'b_router': nrm((L, N_EXPERTS), 0.01),
        'w_e1': nrm((L, N_EXPERTS, D_MODEL, 2 * D_FF), D_MODEL ** -0.5),
        'b_e1': nrm((L, N_EXPERTS, 2 * D_FF), 0.01),
        'w_e2': nrm((L, N_EXPERTS, D_FF, D_MODEL), D_FF ** -0.5),
        'b_e2': nrm((L, N_EXPERTS, D_MODEL), 0.01),
        'norm_final': gain((D_MODEL,)),
    }


def reference(x_prompt, x_sample, state_gdn, state_conv, cache_latent, cache_krope, page_table,
              cache_mem_k, cache_mem_v, mem_prompt, norm_mix, w_in, conv_w, a_log, dt_bias, gdn_norm,
              q_norm, w_uq, kv_norm, w_uk, w_uv, w_out, norm_x, mem_norm, w_xq, w_xk, w_xv, w_xo,
              norm_ffn, w_router, b_router, w_e1, b_e1, w_e2, b_e2, norm_final):
    hp, hs = x_prompt, x_sample
    gdn_p, conv_p, lat_p, kpe_p, mk_p, mv_p = [], [], [], [], [], []
    gdn_s, conv_s, lat_s, kpe_s = [], [], [], []
    for layer in range(DEPTH):
        shared = (norm_mix[layer], w_in[layer], conv_w[layer], a_log[layer], dt_bias[layer], gdn_norm[layer],
                  q_norm[layer], w_uq[layer], kv_norm[layer], w_uk[layer], w_uv[layer], w_out[layer],
                  norm_x[layer], w_xq[layer], w_xo[layer], norm_ffn[layer], w_router[layer], b_router[layer],
                  w_e1[layer], b_e1[layer], w_e2[layer], b_e2[layer])
        hp, s1, c1, lt1, kp1, mk1, mv1 = prompt_layer(hp, mem_prompt, mem_norm[layer], w_xk[layer], w_xv[layer], *shared)
        gdn_p.append(s1)
        conv_p.append(c1)
        lat_p.append(lt1)
        kpe_p.append(kp1)
        mk_p.append(mk1)
        mv_p.append(mv1)
        hs, s2, c2, lt2, kp2 = sample_layer(hs, layer, state_gdn[layer], state_conv[layer], cache_latent, cache_krope,
                                            page_table, cache_mem_k[layer], cache_mem_v[layer], *shared)
        gdn_s.append(s2)
        conv_s.append(c2)
        lat_s.append(lt2)
        kpe_s.append(kp2)
    y_prompt = rms_norm(hp, norm_final)
    y_sample = rms_norm(hs, norm_final)
    return (y_prompt, y_sample,
            jnp.stack(gdn_p), jnp.stack(conv_p), jnp.stack(lat_p, axis=1), jnp.stack(kpe_p, axis=1),
            jnp.stack(mk_p), jnp.stack(mv_p),
            jnp.stack(gdn_s), jnp.stack(conv_s), jnp.stack(lat_s, axis=1), jnp.stack(kpe_s, axis=1))
```

```python
import functools

import jax
import jax.numpy as jnp
import numpy as np
from jax import lax
from jax.experimental import pallas as pl
from jax.experimental.pallas import tpu as pltpu

F32 = jnp.float32
BF16 = jnp.bfloat16
I32 = jnp.int32

NORM_EPS = 1e-6
LANES = 128
SUBLANES = 8
VMEM_LIMIT = 48 * 1024 * 1024

GDN_HEADS = 4
GDN_D = 128
CONV_WIDTH = 4
CONV_CH = 3 * GDN_HEADS * GDN_D
GDN_BLOCK = 128
MLA_HEADS = 4
MLA_NOPE = 128
MLA_ROPE = 64
MLA_V = 128
MLA_Q_RANK = 384
MLA_KV_RANK = 256
MLA_QK_PAD = 256
ROPE_THETA = 10000.0
MLA_SCALE = (MLA_NOPE + MLA_ROPE) ** -0.5
X_HEADS = 4
N_EXPERTS = 32
TOP_K = 4
SWIGLU_LIMIT = 7.0
SWIGLU_ALPHA = 1.702
BETA_LANE = MLA_ROPE
DECAY_LANE = MLA_ROPE + GDN_HEADS
NEG_BIG = -1e30


def _cparams(*sem):
    return pltpu.CompilerParams(dimension_semantics=sem, vmem_limit_bytes=VMEM_LIMIT)


def _rms(x, gain):
    return x * lax.rsqrt(jnp.mean(x * x, axis=-1, keepdims=True) + NORM_EPS) * gain


def _mm(a, b):
    return jnp.dot(a.astype(BF16), b.astype(BF16), preferred_element_type=F32)


def _mm_nt(a, b):
    return lax.dot_general(a.astype(BF16), b.astype(BF16), (((1,), (1,)), ((), ())),
                           preferred_element_type=F32)


def _mm3(a, b):
    a_hi = a.astype(BF16)
    b_hi = b.astype(BF16)
    a_lo = (a - a_hi.astype(F32)).astype(BF16)
    b_lo = (b - b_hi.astype(F32)).astype(BF16)
    dot = functools.partial(jnp.dot, preferred_element_type=F32)
    return dot(a_hi, b_hi) + dot(a_hi, b_lo) + dot(a_lo, b_hi)


def _sigmoid(x):
    return 1.0 / (1.0 + jnp.exp(-x))


def _softplus(x):
    return jnp.maximum(x, 0.0) + jnp.log1p(jnp.exp(-jnp.abs(x)))


def _linear_kernel(*refs, n_in, has_gain, has_res, splits):
    a_refs = refs[:n_in]
    w_refs = refs[n_in:2 * n_in]
    pos = 2 * n_in
    g_ref = refs[pos] if has_gain else None
    pos += int(has_gain)
    r_ref = refs[pos] if has_res else None
    pos += int(has_res)
    out_refs = refs[pos:]
    a0 = a_refs[0][...]
    if has_gain:
        a0 = _rms(a0.astype(F32), g_ref[...])
    acts = [a0.astype(BF16)] + [a[...].astype(BF16) for a in a_refs[1:]]
    off = 0
    for o_ref, width in zip(out_refs, splits):
        acc = None
        for a, w in zip(acts, w_refs):
            d = jnp.dot(a, w[:, off:off + width], preferred_element_type=F32)
            acc = d if acc is None else acc + d
        if has_res:
            acc = acc + r_ref[:, off:off + width]
        o_ref[...] = acc.astype(o_ref.dtype)
        off += width


def fused_linear(acts, weights, *, gain=None, residual=None, splits=None, out_dtypes=None,
                 tm=256, name="fused_linear"):
    m = acts[0].shape[0]
    n = weights[0].shape[1]
    tm = min(tm, m)
    assert m % tm == 0
    splits = tuple(splits) if splits is not None else (n,)
    assert sum(splits) == n and all(s % LANES == 0 for s in splits)
    out_dtypes = tuple(out_dtypes) if out_dtypes is not None else (F32,) * len(splits)
    in_specs = [pl.BlockSpec((tm, a.shape[1]), lambda i: (i, 0)) for a in acts]
    in_specs += [pl.BlockSpec(w.shape, lambda i: (0, 0)) for w in weights]
    args = list(acts) + list(weights)
    if gain is not None:
        in_specs.append(pl.BlockSpec((1, gain.shape[-1]), lambda i: (0, 0)))
        args.append(gain.reshape(1, -1))
    if residual is not None:
        in_specs.append(pl.BlockSpec((tm, n), lambda i: (i, 0)))
        args.append(residual)
    outs = pl.pallas_call(
        functools.partial(_linear_kernel, n_in=len(acts), has_gain=gain is not None,
                          has_res=residual is not None, splits=splits),
        out_shape=[jax.ShapeDtypeStruct((m, s), dt) for s, dt in zip(splits, out_dtypes)],
        grid=(m // tm,),
        in_specs=in_specs,
        out_specs=[pl.BlockSpec((tm, s), lambda i: (i, 0)) for s in splits],
        compiler_params=_cparams("parallel"),
        name=name,
    )(*args)
    return outs


def _gate_values(kba, alog_row, dtb_row):
    beta = _sigmoid(kba)
    g = -jnp.exp(alog_row) * _softplus(kba + dtb_row)
    return beta, g


def _l2norm(x):
    return x * lax.rsqrt(jnp.sum(x * x, axis=-1, keepdims=True) + NORM_EPS)


def _gdn_prompt_kernel(x_ref, z_ref, kba_ref, cw_ref, alog_ref, dtb_ref, gn_ref,
                       y_ref, s_out_ref, xbuf, state):
    t = pl.program_id(1)
    nt = pl.num_programs(1)
    blk = GDN_BLOCK
    hist = CONV_WIDTH - 1

    @pl.when(t == 0)
    def _():
        xbuf[0:SUBLANES, :] = jnp.zeros((SUBLANES, CONV_CH), F32)
        state[...] = jnp.zeros(state.shape, F32)

    xbuf[SUBLANES:SUBLANES + blk, :] = x_ref[0]
    cw = cw_ref[...]
    conv = xbuf[SUBLANES - hist:SUBLANES - hist + blk, :] * cw[0:1, :]
    for j in range(1, CONV_WIDTH):
        conv = conv + xbuf[SUBLANES - hist + j:SUBLANES - hist + j + blk, :] * cw[j:j + 1, :]
    xbuf[SUBLANES - hist:SUBLANES, :] = xbuf[SUBLANES + blk - hist:SUBLANES + blk, :]
    c = conv * _sigmoid(conv)

    kba = kba_ref[0]
    beta_all, g_all = _gate_values(kba, alog_ref[...], dtb_ref[...])
    row = lax.broadcasted_iota(I32, (blk, blk), 0)
    col = lax.broadcasted_iota(I32, (blk, blk), 1)
    gc = g_all
    shift = 1
    while shift < blk:
        rolled = pltpu.roll(gc, shift, 0)
        gc = gc + jnp.where(row >= shift, rolled, 0.0)
        shift *= 2
    gc_t = gc.T
    incl = row >= col
    strict = row > col
    eye = (row == col).astype(F32)
    z = z_ref[0]
    gn = gn_ref[...]
    nh = GDN_HEADS * GDN_D

    for h in range(GDN_HEADS):
        q = _l2norm(c[:, h * GDN_D:(h + 1) * GDN_D]) * (GDN_D ** -0.5)
        k = _l2norm(c[:, nh + h * GDN_D:nh + (h + 1) * GDN_D])
        v = c[:, 2 * nh + h * GDN_D:2 * nh + (h + 1) * GDN_D]
        bcol = beta_all[:, BETA_LANE + h:BETA_LANE + h + 1]
        gcol = gc[:, DECAY_LANE + h:DECAY_LANE + h + 1]
        grow = gc_t[DECAY_LANE + h:DECAY_LANE + h + 1, :]
        decay = jnp.exp(jnp.where(incl, gcol - grow, NEG_BIG))
        kb = k * bcol
        vb = v * bcol
        a = jnp.where(strict, _mm_nt(kb, k) * decay, 0.0)
        x = eye - a
        p = _mm3(a, a)
        x = x + _mm3(x, p)
        for _ in range(5):
            p = _mm3(p, p)
            x = x + _mm3(x, p)
        egc = jnp.exp(gcol)
        u = _mm(x, vb)
        w = _mm(x, kb * egc)
        intra = _mm_nt(q, k) * decay
        s_h = state[h]
        v_new = u - _mm(w, s_h)
        o = _mm(q * egc, s_h) + _mm(intra, v_new)
        g_last = gcol[blk - 1:blk, :]
        kd = k * jnp.exp(g_last - gcol)
        state[h] = s_h * jnp.exp(g_last) + _mm(kd.T, v_new)
        zz = z[:, h * GDN_D:(h + 1) * GDN_D]
        y_ref[0, :, h * GDN_D:(h + 1) * GDN_D] = (_rms(o, gn) * (zz * _sigmoid(zz))).astype(y_ref.dtype)

    @pl.when(t == nt - 1)
    def _():
        s_out_ref[0] = state[...]


def gdn_prompt(conv_in, z, kba, conv_w, alog_row, dtb_row, gdn_norm):
    b, t, _ = conv_in.shape
    assert t % GDN_BLOCK == 0
    nt = t // GDN_BLOCK
    y, s = pl.pallas_call(
        _gdn_prompt_kernel,
        out_shape=[jax.ShapeDtypeStruct((b, t, GDN_HEADS * GDN_D), BF16),
                   jax.ShapeDtypeStruct((b, GDN_HEADS, GDN_D, GDN_D), F32)],
        grid=(b, nt),
        in_specs=[pl.BlockSpec((1, GDN_BLOCK, CONV_CH), lambda i, j: (i, j, 0)),
                  pl.BlockSpec((1, GDN_BLOCK, GDN_HEADS * GDN_D), lambda i, j: (i, j, 0)),
                  pl.BlockSpec((1, GDN_BLOCK, LANES), lambda i, j: (i, j, 0)),
                  pl.BlockSpec((CONV_WIDTH, CONV_CH), lambda i, j: (0, 0)),
                  pl.BlockSpec((1, LANES), lambda i, j: (0, 0)),
                  pl.BlockSpec((1, LANES), lambda i, j: (0, 0)),
                  pl.BlockSpec((1, GDN_D), lambda i, j: (0, 0))],
        out_specs=[pl.BlockSpec((1, GDN_BLOCK, GDN_HEADS * GDN_D), lambda i, j: (i, j, 0)),
                   pl.BlockSpec((1, GDN_HEADS, GDN_D, GDN_D), lambda i, j: (i, 0, 0, 0))],
        scratch_shapes=[pltpu.VMEM((SUBLANES + GDN_BLOCK, CONV_CH), F32),
                        pltpu.VMEM((GDN_HEADS, GDN_D, GDN_D), F32)],
        compiler_params=_cparams("parallel", "arbitrary"),
        name="gdn_prompt",
    )(conv_in, z, kba, conv_w, alog_row, dtb_row, gdn_norm.reshape(1, -1))
    return y, s


GDN_SAMPLE_BB = 8


def _gdn_sample_kernel(x_ref, sc_ref, kba_ref, z_ref, s_ref, cw_ref, alog_ref, dtb_ref, gn_ref,
                       s_out_ref, sc_out_ref, y_ref, tbuf):
    bb = GDN_SAMPLE_BB
    x = x_ref[...]
    cw = cw_ref[...]
    conv = x * cw[CONV_WIDTH - 1:CONV_WIDTH, :]
    for j in range(CONV_WIDTH - 1):
        conv = conv + sc_ref[:, j, :] * cw[j:j + 1, :]
    for j in range(CONV_WIDTH - 2):
        sc_out_ref[:, j, :] = sc_ref[:, j + 1, :]
    sc_out_ref[:, CONV_WIDTH - 2, :] = x
    c = conv * _sigmoid(conv)
    beta_all, g_all = _gate_values(kba_ref[...], alog_ref[...], dtb_ref[...])
    eg_all = jnp.exp(g_all)
    z = z_ref[...]
    gn = gn_ref[...]
    nh = GDN_HEADS * GDN_D
    tbuf[...] = jnp.zeros(tbuf.shape, F32)
    for h in range(GDN_HEADS):
        q = _l2norm(c[:, h * GDN_D:(h + 1) * GDN_D]) * (GDN_D ** -0.5)
        k = _l2norm(c[:, nh + h * GDN_D:nh + (h + 1) * GDN_D])
        v = c[:, 2 * nh + h * GDN_D:2 * nh + (h + 1) * GDN_D]
        tbuf[0:bb, :] = q
        q_t = tbuf[...].T
        tbuf[0:bb, :] = k
        k_t = tbuf[...].T
        for b in range(bb):
            qcol = q_t[:, b:b + 1]
            kcol = k_t[:, b:b + 1]
            eg = eg_all[b:b + 1, DECAY_LANE + h:DECAY_LANE + h + 1]
            beta = beta_all[b:b + 1, BETA_LANE + h:BETA_LANE + h + 1]
            s1 = s_ref[b, h] * eg
            pred = jnp.sum(s1 * kcol, axis=0, keepdims=True)
            u = (v[b:b + 1, :] - pred) * beta
            s2 = s1 + kcol * u
            s_out_ref[b, h] = s2
            o = jnp.sum(s2 * qcol, axis=0, keepdims=True)
            zz = z[b:b + 1, h * GDN_D:(h + 1) * GDN_D]
            y_ref[b:b + 1, h * GDN_D:(h + 1) * GDN_D] = (_rms(o, gn) * (zz * _sigmoid(zz))).astype(y_ref.dtype)


def gdn_sample(conv_in, state_conv, kba, z, state_gdn, conv_w, alog_row, dtb_row, gdn_norm):
    b = conv_in.shape[0]
    bb = GDN_SAMPLE_BB
    assert b % bb == 0
    hist = CONV_WIDTH - 1
    return pl.pallas_call(
        _gdn_sample_kernel,
        out_shape=[jax.ShapeDtypeStruct(state_gdn.shape, F32),
                   jax.ShapeDtypeStruct(state_conv.shape, F32),
                   jax.ShapeDtypeStruct((b, GDN_HEADS * GDN_D), BF16)],
        grid=(b // bb,),
        in_specs=[pl.BlockSpec((bb, CONV_CH), lambda i: (i, 0)),
                  pl.BlockSpec((bb, hist, CONV_CH), lambda i: (i, 0, 0)),
                  pl.BlockSpec((bb, LANES), lambda i: (i, 0)),
                  pl.BlockSpec((bb, GDN_HEADS * GDN_D), lambda i: (i, 0)),
                  pl.BlockSpec((bb, GDN_HEADS, GDN_D, GDN_D), lambda i: (i, 0, 0, 0)),
                  pl.BlockSpec((CONV_WIDTH, CONV_CH), lambda i: (0, 0)),
                  pl.BlockSpec((1, LANES), lambda i: (0, 0)),
                  pl.BlockSpec((1, LANES), lambda i: (0, 0)),
                  pl.BlockSpec((1, GDN_D), lambda i: (0, 0))],
        out_specs=[pl.BlockSpec((bb, GDN_HEADS, GDN_D, GDN_D), lambda i: (i, 0, 0, 0)),
                   pl.BlockSpec((bb, hist, CONV_CH), lambda i: (i, 0, 0)),
                   pl.BlockSpec((bb, GDN_HEADS * GDN_D), lambda i: (i, 0))],
        scratch_shapes=[pltpu.VMEM((GDN_D, GDN_D), F32)],
        compiler_params=_cparams("parallel"),
        name="gdn_sample",
    )(conv_in, state_conv, kba, z, state_gdn, conv_w, alog_row, dtb_row, gdn_norm.reshape(1, -1))


def _rope128(x, cos, sin):
    half = MLA_ROPE // 2
    lane = lax.broadcasted_iota(I32, x.shape, 1)
    swapped = jnp.where(lane < half, pltpu.roll(x, LANES - half, 1), pltpu.roll(x, half, 1))
    return x * cos + swapped * sin


def _mla_prep_kernel(cq_ref, ckv_ref, kba_ref, cos_ref, sin_ref, qn_ref, kvn_ref, wuq_ref, wuk_ref, wuv_ref,
                     q_ref, k_ref, v_ref, lat_ref, kpe_ref):
    cos = cos_ref[...]
    sin = sin_ref[...]
    qn = _rms(cq_ref[...], qn_ref[...]).astype(BF16)
    lat = _rms(ckv_ref[...], kvn_ref[...])
    lat_ref[...] = lat
    lat_b = lat.astype(BF16)
    kpe = _rope128(kba_ref[...], cos, sin)
    kpe_ref[...] = kpe[:, :MLA_ROPE]
    for h in range(MLA_HEADS):
        lo = h * MLA_QK_PAD
        q_ref[:, lo:lo + MLA_NOPE] = jnp.dot(
            qn, wuq_ref[:, lo:lo + MLA_NOPE], preferred_element_type=F32).astype(q_ref.dtype)
        q_pe = jnp.dot(qn, wuq_ref[:, lo + MLA_NOPE:lo + MLA_QK_PAD], preferred_element_type=F32)
        q_ref[:, lo + MLA_NOPE:lo + MLA_QK_PAD] = _rope128(q_pe, cos, sin).astype(q_ref.dtype)
        k_ref[:, lo:lo + MLA_NOPE] = jnp.dot(
            lat_b, wuk_ref[:, h * MLA_NOPE:(h + 1) * MLA_NOPE], preferred_element_type=F32).astype(k_ref.dtype)
        k_ref[:, lo + MLA_NOPE:lo + MLA_QK_PAD] = kpe.astype(k_ref.dtype)
    v_ref[...] = jnp.dot(lat_b, wuv_ref[...], preferred_element_type=F32).astype(v_ref.dtype)


def mla_prep(c_q, c_kv, kba, cos_tab, sin_tab, q_norm, kv_norm, wuq_p, wuk, wuv, *, seq, q_dtype, tm=256):
    m = c_q.shape[0]
    tm = min(tm, m, seq)
    assert m % tm == 0 and seq % tm == 0
    nseq = seq // tm
    hq = MLA_HEADS * MLA_QK_PAD
    row = lambda i: (i, 0)
    const = lambda i: (0, 0)
    return pl.pallas_call(
        _mla_prep_kernel,
        out_shape=[jax.ShapeDtypeStruct((m, hq), q_dtype),
                   jax.ShapeDtypeStruct((m, hq), BF16),
                   jax.ShapeDtypeStruct((m, MLA_HEADS * MLA_V), BF16),
                   jax.ShapeDtypeStruct((m, MLA_KV_RANK), F32),
                   jax.ShapeDtypeStruct((m, MLA_ROPE), F32)],
        grid=(m // tm,),
        in_specs=[pl.BlockSpec((tm, MLA_Q_RANK), row),
                  pl.BlockSpec((tm, MLA_KV_RANK), row),
                  pl.BlockSpec((tm, LANES), row),
                  pl.BlockSpec((tm, LANES), lambda i: (i % nseq, 0)),
                  pl.BlockSpec((tm, LANES), lambda i: (i % nseq, 0)),
                  pl.BlockSpec((1, MLA_Q_RANK), const),
                  pl.BlockSpec((1, MLA_KV_RANK), const),
                  pl.BlockSpec(wuq_p.shape, const),
                  pl.BlockSpec(wuk.shape, const),
                  pl.BlockSpec(wuv.shape, const)],
        out_specs=[pl.BlockSpec((tm, hq), row),
                   pl.BlockSpec((tm, hq), row),
                   pl.BlockSpec((tm, MLA_HEADS * MLA_V), row),
                   pl.BlockSpec((tm, MLA_KV_RANK), row),
                   pl.BlockSpec((tm, MLA_ROPE), row)],
        compiler_params=_cparams("parallel"),
        name="mla_prep",
    )(c_q, c_kv, kba, cos_tab, sin_tab, q_norm.reshape(1, -1), kv_norm.reshape(1, -1), wuq_p, wuk, wuv)


def _flash_kernel(q_ref, k_ref, v_ref, o_ref, m_s, l_s, acc_s, *, tq, tk):
    qi = pl.program_id(2)
    ki = pl.program_id(3)

    @pl.when(ki == 0)
    def _():
        m_s[...] = jnp.full(m_s.shape, NEG_BIG, F32)
        l_s[...] = jnp.zeros(l_s.shape, F32)
        acc_s[...] = jnp.zeros(acc_s.shape, F32)

    @pl.when(ki * tk <= qi * tq + (tq - 1))
    def _():
        s = lax.dot_general(q_ref[0], k_ref[0], (((1,), (1,)), ((), ())), preferred_element_type=F32) * MLA_SCALE
        qpos = qi * tq + lax.broadcasted_iota(I32, (tq, tk), 0)
        kpos = ki * tk + lax.broadcasted_iota(I32, (tq, tk), 1)
        s = jnp.where(kpos <= qpos, s, NEG_BIG)
        m_prev = m_s[...]
        m_new = jnp.maximum(m_prev, jnp.max(s, axis=-1, keepdims=True))
        corr = jnp.exp(m_prev - m_new)
        p = jnp.exp(s - m_new)
        l_s[...] = l_s[...] * corr + jnp.sum(p, axis=-1, keepdims=True)
        acc_s[...] = acc_s[...] * corr + jnp.dot(p.astype(BF16), v_ref[0], preferred_element_type=F32)
        m_s[...] = m_new

    @pl.when(ki == pl.num_programs(3) - 1)
    def _():
        o_ref[0] = (acc_s[...] / l_s[...]).astype(o_ref.dtype)


def mla_flash(q, k, v, *, tq=512, tk=512):
    b, t, _ = q.shape
    tq = min(tq, t)
    tk = min(tk, t)
    assert t % tq == 0 and t % tk == 0
    last_k = lambda qi: (qi * tq + tq - 1) // tk

    def kv_map(bi, h, qi, ki):
        return (bi, jnp.minimum(ki, last_k(qi)), h)

    return pl.pallas_call(
        functools.partial(_flash_kernel, tq=tq, tk=tk),
        out_shape=jax.ShapeDtypeStruct((b, t, MLA_HEADS * MLA_V), BF16),
        grid=(b, MLA_HEADS, t // tq, t // tk),
        in_specs=[pl.BlockSpec((1, tq, MLA_QK_PAD), lambda bi, h, qi, ki: (bi, qi, h)),
                  pl.BlockSpec((1, tk, MLA_QK_PAD), kv_map),
                  pl.BlockSpec((1, tk, MLA_V), kv_map)],
        out_specs=pl.BlockSpec((1, tq, MLA_V), lambda bi, h, qi, ki: (bi, qi, h)),
        scratch_shapes=[pltpu.VMEM((tq, 1), F32), pltpu.VMEM((tq, 1), F32), pltpu.VMEM((tq, MLA_V), F32)],
        compiler_params=_cparams("parallel", "parallel", "parallel", "arbitrary"),
        name="mla_flash",
    )(q, k, v)


DEC_HEAD_PAD = 8
DEC_GROUP = 8


def _decode_kernel(pt_ref, ptn_ref, qlat_ref, qpe_ref, latn_ref, kpen_ref, wuv_ref, lat_hbm, kpe_hbm,
                   o_ref, latbuf, kpebuf, sem, *, nb, n_pages, page):
    b = pl.program_id(0)
    grp = DEC_GROUP
    n_groups = n_pages // grp

    def page_copies(pg, slot, j):
        return (pltpu.make_async_copy(lat_hbm.at[pg, 0], latbuf.at[slot, j], sem.at[0, slot]),
                pltpu.make_async_copy(kpe_hbm.at[pg, 0], kpebuf.at[slot, j], sem.at[1, slot]))

    def start_group(tbl_ref, g, slot):
        for j in range(grp):
            for cp in page_copies(tbl_ref[0, 0, g * grp + j], slot, j):
                cp.start()

    def wait_group(slot):
        for j in range(grp):
            for cp in page_copies(0, slot, j):
                cp.wait()

    @pl.when(b == 0)
    def _():
        start_group(pt_ref, 0, 0)

    qlat = qlat_ref[0]
    qpe = qpe_ref[0]
    latn = latn_ref[0]
    kpen = kpen_ref[0]
    qlat_b = qlat.astype(BF16)
    qpe_b = qpe.astype(BF16)
    s_new = (jnp.sum(qlat * latn, axis=-1, keepdims=True)
             + jnp.sum(qpe * kpen, axis=-1, keepdims=True)) * MLA_SCALE
    m0 = s_new
    l0 = jnp.ones_like(s_new)
    acc0 = jnp.broadcast_to(latn, qlat.shape)

    def consume(slot, carry):
        m, l, acc = carry
        lat = latbuf[slot].reshape(grp * page, MLA_KV_RANK).astype(BF16)
        kpe = kpebuf[slot].reshape(grp * page, MLA_ROPE).astype(BF16)
        s = (lax.dot_general(qlat_b, lat, (((1,), (1,)), ((), ())), preferred_element_type=F32)
             + lax.dot_general(qpe_b, kpe, (((1,), (1,)), ((), ())), preferred_element_type=F32)) * MLA_SCALE
        m_new = jnp.maximum(m, jnp.max(s, axis=-1, keepdims=True))
        corr = jnp.exp(m - m_new)
        p = jnp.exp(s - m_new)
        l = l * corr + jnp.sum(p, axis=-1, keepdims=True)
        acc = acc * corr + jnp.dot(p.astype(BF16), lat, preferred_element_type=F32)
        return m_new, l, acc

    def pair_body(i, carry):
        g0 = 2 * i
        start_group(pt_ref, g0 + 1, 1)
        wait_group(0)
        carry = consume(0, carry)

        @pl.when(g0 + 2 < n_groups)
        def _():
            start_group(pt_ref, g0 + 2, 0)

        @pl.when(jnp.logical_and(g0 + 2 >= n_groups, b + 1 < nb))
        def _():
            start_group(ptn_ref, 0, 0)

        wait_group(1)
        return consume(1, carry)

    m, l, acc = lax.fori_loop(0, n_groups // 2, pair_body, (m0, l0, acc0))
    o_lat = (acc / l).astype(BF16)
    res = jnp.dot(o_lat, wuv_ref[...], preferred_element_type=F32)
    o_ref[0] = jnp.concatenate(
        [res[h:h + 1, h * MLA_V:(h + 1) * MLA_V] for h in range(MLA_HEADS)], axis=1).astype(o_ref.dtype)


def mla_decode(page_table, qlat8, qpe8, lat_new, kpe_new, wuv, cache_latent, cache_krope):
    b, n_pages = page_table.shape
    page = cache_latent.shape[2]
    assert n_pages % (2 * DEC_GROUP) == 0
    pt3 = page_table.reshape(b, 1, n_pages)
    smem_row = lambda f: pl.BlockSpec((1, 1, n_pages), f, memory_space=pltpu.SMEM)
    return pl.pallas_call(
        functools.partial(_decode_kernel, nb=b, n_pages=n_pages, page=page),
        out_shape=jax.ShapeDtypeStruct((b, 1, MLA_HEADS * MLA_V), BF16),
        grid=(b,),
        in_specs=[smem_row(lambda i: (i, 0, 0)),
                  smem_row(lambda i: (jnp.minimum(i + 1, b - 1), 0, 0)),
                  pl.BlockSpec((1, DEC_HEAD_PAD, MLA_KV_RANK), lambda i: (i, 0, 0)),
                  pl.BlockSpec((1, DEC_HEAD_PAD, MLA_ROPE), lambda i: (i, 0, 0)),
                  pl.BlockSpec((1, 1, MLA_KV_RANK), lambda i: (i, 0, 0)),
                  pl.BlockSpec((1, 1, MLA_ROPE), lambda i: (i, 0, 0)),
                  pl.BlockSpec(wuv.shape, lambda i: (0, 0)),
                  pl.BlockSpec(memory_space=pl.ANY),
                  pl.BlockSpec(memory_space=pl.ANY)],
        out_specs=pl.BlockSpec((1, 1, MLA_HEADS * MLA_V), lambda i: (i, 0, 0)),
        scratch_shapes=[pltpu.VMEM((2, DEC_GROUP, page, MLA_KV_RANK), F32),
                        pltpu.VMEM((2, DEC_GROUP, page, MLA_ROPE), F32),
                        pltpu.SemaphoreType.DMA((2, 2))],
        compiler_params=_cparams("arbitrary"),
        name="mla_decode",
    )(pt3, pt3, qlat8, qpe8, lat_new.reshape(b, 1, -1), kpe_new.reshape(b, 1, -1), wuv,
      cache_latent, cache_krope)


def _softmax_rows(s):
    m = jnp.max(s, axis=-1, keepdims=True)
    p = jnp.exp(s - m)
    return p / jnp.sum(p, axis=-1, keepdims=True)


def _xattn_prompt_kernel(q_ref, k_ref, v_ref, o_ref, *, dh):
    scale = dh ** -0.5
    for h in range(X_HEADS):
        q = q_ref[0, :, h * dh:(h + 1) * dh]
        s = lax.dot_general(q, k_ref[0, :, h * dh:(h + 1) * dh], (((1,), (1,)), ((), ())),
                            preferred_element_type=F32) * scale
        p = _softmax_rows(s).astype(BF16)
        o_ref[0, :, h * dh:(h + 1) * dh] = jnp.dot(
            p, v_ref[0, :, h * dh:(h + 1) * dh], preferred_element_type=F32).astype(o_ref.dtype)


def xattn_prompt(q, k, v, *, tq=512):
    b, t, d = q.shape
    mem = k.shape[1]
    tq = min(tq, t)
    return pl.pallas_call(
        functools.partial(_xattn_prompt_kernel, dh=d // X_HEADS),
        out_shape=jax.ShapeDtypeStruct((b, t, d), BF16),
        grid=(b, t // tq),
        in_specs=[pl.BlockSpec((1, tq, d), lambda i, j: (i, j, 0)),
                  pl.BlockSpec((1, mem, d), lambda i, j: (i, 0, 0)),
                  pl.BlockSpec((1, mem, d), lambda i, j: (i, 0, 0))],
        out_specs=pl.BlockSpec((1, tq, d), lambda i, j: (i, j, 0)),
        compiler_params=_cparams("parallel", "parallel"),
        name="xattn_prompt",
    )(q, k, v)


XATTN_SAMPLE_BB = 2


def _xattn_sample_kernel(q_ref, k_ref, v_ref, o_ref, *, dh):
    scale = dh ** -0.5
    for b in range(XATTN_SAMPLE_BB):
        for h in range(X_HEADS):
            q = jnp.broadcast_to(q_ref[b, :, h * dh:(h + 1) * dh], (SUBLANES, dh)).astype(BF16)
            kh = k_ref[b, :, h * dh:(h + 1) * dh].astype(BF16)
            vh = v_ref[b, :, h * dh:(h + 1) * dh].astype(BF16)
            s = lax.dot_general(q, kh, (((1,), (1,)), ((), ())), preferred_element_type=F32) * scale
            p = _softmax_rows(s).astype(BF16)
            o = jnp.dot(p, vh, preferred_element_type=F32)
            o_ref[b, :, h * dh:(h + 1) * dh] = o[0:1, :].astype(o_ref.dtype)


def xattn_sample(q, k, v):
    b, _, d = q.shape
    mem = k.shape[1]
    bb = XATTN_SAMPLE_BB
    assert b % bb == 0
    return pl.pallas_call(
        functools.partial(_xattn_sample_kernel, dh=d // X_HEADS),
        out_shape=jax.ShapeDtypeStruct((b, 1, d), BF16),
        grid=(b // bb,),
        in_specs=[pl.BlockSpec((bb, 1, d), lambda i: (i, 0, 0)),
                  pl.BlockSpec((bb, mem, d), lambda i: (i, 0, 0)),
                  pl.BlockSpec((bb, mem, d), lambda i: (i, 0, 0))],
        out_specs=pl.BlockSpec((bb, 1, d), lambda i: (i, 0, 0)),
        compiler_params=_cparams("parallel"),
        name="xattn_sample",
    )(q, k, v)


def _post_xattn_kernel(ox_ref, wxo_ref, h_ref, g_ref, wr_hi_ref, wr_lo_ref, br_ref,
                       h2_ref, hn_ref, gate_ref, idx_ref):
    h2 = h_ref[...] + jnp.dot(ox_ref[...], wxo_ref[...], preferred_element_type=F32)
    h2_ref[...] = h2
    hn = _rms(h2, g_ref[...])
    hn_ref[...] = hn
    hn_hi = hn.astype(BF16)
    hn_lo = (hn - hn_hi.astype(F32)).astype(BF16)
    logits = (jnp.dot(hn_hi, wr_hi_ref[...], preferred_element_type=F32)
              + jnp.dot(hn_hi, wr_lo_ref[...], preferred_element_type=F32)
              + jnp.dot(hn_lo, wr_hi_ref[...], preferred_element_type=F32)) + br_ref[...]
    lane = lax.broadcasted_iota(I32, logits.shape, 1)
    logits = jnp.where(lane < N_EXPERTS, logits, NEG_BIG)
    vals = []
    gates = jnp.zeros(logits.shape, F32)
    ids = jnp.zeros(logits.shape, I32)
    for k in range(TOP_K):
        m = jnp.max(logits, axis=-1, keepdims=True)
        idx = jnp.min(jnp.where(logits == m, lane, LANES), axis=-1, keepdims=True)
        vals.append(m)
        ids = jnp.where(lane == k, idx, ids)
        logits = jnp.where(lane == idx, NEG_BIG, logits)
    exps = [jnp.exp(v - vals[0]) for v in vals]
    denom = exps[0]
    for e in exps[1:]:
        denom = denom + e
    for k in range(TOP_K):
        gates = jnp.where(lane == k, exps[k] / denom, gates)
    gate_ref[...] = gates
    idx_ref[...] = ids


def post_xattn(ox, wxo, h, norm_ffn, wr_hi, wr_lo, br_row, *, tm=256):
    m, d = h.shape
    tm = min(tm, m)
    row = lambda i: (i, 0)
    const = lambda i: (0, 0)
    return pl.pallas_call(
        _post_xattn_kernel,
        out_shape=[jax.ShapeDtypeStruct((m, d), F32), jax.ShapeDtypeStruct((m, d), F32),
                   jax.ShapeDtypeStruct((m, LANES), F32), jax.ShapeDtypeStruct((m, LANES), I32)],
        grid=(m // tm,),
        in_specs=[pl.BlockSpec((tm, d), row), pl.BlockSpec(wxo.shape, const), pl.BlockSpec((tm, d), row),
                  pl.BlockSpec((1, d), const), pl.BlockSpec(wr_hi.shape, const), pl.BlockSpec(wr_lo.shape, const),
                  pl.BlockSpec((1, LANES), const)],
        out_specs=[pl.BlockSpec((tm, d), row), pl.BlockSpec((tm, d), row),
                   pl.BlockSpec((tm, LANES), row), pl.BlockSpec((tm, LANES), row)],
        compiler_params=_cparams("parallel"),
        name="post_xattn_router",
    )(ox, wxo, h, norm_ffn.reshape(1, -1), wr_hi, wr_lo, br_row)


MOE_TILE = 256


def _gather_rows(idx_ref, n_rows, src_hbm, dst, sem):
    def body(r, _):
        tok = idx_ref[0, 0, r]
        pltpu.make_async_copy(src_hbm.at[pl.ds(tok, 1)], dst.at[pl.ds(r, 1)], sem).start()
        return 0
    lax.fori_loop(0, n_rows, body, 0, unroll=8)


def _wait_rows(n_rows, src_hbm, dst, sem):
    def body(r, _):
        pltpu.make_async_copy(src_hbm.at[pl.ds(0, 1)], dst.at[pl.ds(r, 1)], sem).wait()
        return 0
    lax.fori_loop(0, n_rows, body, 0, unroll=8)


def _moe_kernel(te_ref, tv_ref, idx_ref, idxn_ref, x_hbm, w1g_ref, w1l_ref, b1g_ref, b1l_ref, w2_ref, b2_ref,
                y_ref, xbuf, sem, *, nt):
    t = pl.program_id(0)
    tm = MOE_TILE
    slot = t % 2

    @pl.when(jnp.logical_and(t == 0, tv_ref[0] > 0))
    def _():
        _gather_rows(idx_ref, tm, x_hbm, xbuf.at[0], sem.at[0])

    @pl.when(t + 1 < nt)
    def _():
        @pl.when(tv_ref[jnp.minimum(t + 1, nt - 1)] > 0)
        def _():
            _gather_rows(idxn_ref, tm, x_hbm, xbuf.at[1 - slot], sem.at[1 - slot])

    @pl.when(tv_ref[t] > 0)
    def _():
        _wait_rows(tm, x_hbm, xbuf.at[slot], sem.at[slot])
        x = xbuf[slot].astype(BF16)
        g = jnp.dot(x, w1g_ref[0], preferred_element_type=F32) + b1g_ref[0]
        lin = jnp.dot(x, w1l_ref[0], preferred_element_type=F32) + b1l_ref[0]
        glu = jnp.minimum(g, SWIGLU_LIMIT)
        lin = jnp.clip(lin, -SWIGLU_LIMIT, SWIGLU_LIMIT)
        act = (lin + 1.0) * (glu * _sigmoid(SWIGLU_ALPHA * glu))
        y_ref[...] = jnp.dot(act.astype(BF16), w2_ref[0], preferred_element_type=F32) + b2_ref[0]

    @pl.when(tv_ref[t] == 0)
    def _():
        y_ref[...] = jnp.zeros(y_ref.shape, F32)


def moe_experts(tile_expert, tile_valid, row_token, x, w1g, w1l, b1g, b1l, w2, b2):
    n_tiles = tile_expert.shape[0]
    tm = MOE_TILE
    d = x.shape[1]
    f = w1g.shape[2]
    idx3 = row_token.reshape(n_tiles, 1, tm)
    wmap = lambda t, te, tv: (te[t], 0, 0)
    grid_spec = pltpu.PrefetchScalarGridSpec(
        num_scalar_prefetch=2,
        grid=(n_tiles,),
        in_specs=[pl.BlockSpec((1, 1, tm), lambda t, te, tv: (t, 0, 0), memory_space=pltpu.SMEM),
                  pl.BlockSpec((1, 1, tm), lambda t, te, tv: (jnp.minimum(t + 1, n_tiles - 1), 0, 0),
                               memory_space=pltpu.SMEM),
                  pl.BlockSpec(memory_space=pl.ANY),
                  pl.BlockSpec((1, d, f), wmap), pl.BlockSpec((1, d, f), wmap),
                  pl.BlockSpec((1, 1, f), wmap), pl.BlockSpec((1, 1, f), wmap),
                  pl.BlockSpec((1, f, d), wmap), pl.BlockSpec((1, 1, d), wmap)],
        out_specs=pl.BlockSpec((tm, d), lambda t, te, tv: (t, 0)),
        scratch_shapes=[pltpu.VMEM((2, tm, d), F32), pltpu.SemaphoreType.DMA((2,))],
    )
    return pl.pallas_call(
        functools.partial(_moe_kernel, nt=n_tiles),
        out_shape=jax.ShapeDtypeStruct((n_tiles * tm, d), F32),
        grid_spec=grid_spec,
        compiler_params=_cparams("arbitrary"),
        name="moe_experts",
    )(tile_expert, tile_valid, idx3, idx3, x, w1g, w1l, b1g, b1l, w2, b2)


COMBINE_TILE = 128


def _combine_kernel(pos_ref, posn_ref, y_hbm, h2_ref, gate_ref, gain_ref, o_ref, ybuf, sem, *, nt):
    t = pl.program_id(0)
    tm = COMBINE_TILE
    n_rows = tm * TOP_K
    slot = t % 2

    @pl.when(t == 0)
    def _():
        _gather_rows(pos_ref, n_rows, y_hbm, ybuf.at[0], sem.at[0])

    if nt > 1:
        @pl.when(t + 1 < nt)
        def _():
            _gather_rows(posn_ref, n_rows, y_hbm, ybuf.at[1 - slot], sem.at[1 - slot])

    _wait_rows(n_rows, y_hbm, ybuf.at[slot], sem.at[slot])
    gates = gate_ref[...]
    acc = h2_ref[...]
    for k in range(TOP_K):
        acc = acc + gates[:, k:k + 1] * ybuf[slot, k * tm:(k + 1) * tm, :]
    o_ref[...] = _rms(acc, gain_ref[...])


def moe_combine(pos_km, y_sorted, h2, gates, norm_final):
    m, d = h2.shape
    tm = COMBINE_TILE
    assert m % tm == 0
    n_tiles = m // tm
    row = lambda t: (t, 0)
    return pl.pallas_call(
        functools.partial(_combine_kernel, nt=n_tiles),
        out_shape=jax.ShapeDtypeStruct((m, d), F32),
        grid=(n_tiles,),
        in_specs=[pl.BlockSpec((1, 1, TOP_K * tm), lambda t: (t, 0, 0), memory_space=pltpu.SMEM),
                  pl.BlockSpec((1, 1, TOP_K * tm), lambda t: (jnp.minimum(t + 1, n_tiles - 1), 0, 0),
                               memory_space=pltpu.SMEM),
                  pl.BlockSpec(memory_space=pl.ANY),
                  pl.BlockSpec((tm, d), row), pl.BlockSpec((tm, LANES), row),
                  pl.BlockSpec((1, d), lambda t: (0, 0))],
        out_specs=pl.BlockSpec((tm, d), row),
        scratch_shapes=[pltpu.VMEM((2, TOP_K * tm, d), F32), pltpu.SemaphoreType.DMA((2,))],
        compiler_params=_cparams("arbitrary"),
        name="moe_combine",
    )(pos_km, pos_km, y_sorted, h2, gates, norm_final.reshape(1, -1))


def _route(ids):
    n = ids.shape[0]
    a = n * TOP_K
    tm = MOE_TILE
    n_tiles = (a + N_EXPERTS * (tm - 1) + tm - 1) // tm
    e_flat = ids.reshape(-1)
    order = jnp.argsort(e_flat, stable=True).astype(I32)
    sorted_e = e_flat[order]
    counts = jnp.zeros((N_EXPERTS,), I32).at[e_flat].add(1)
    tiles_per = (counts + tm - 1) // tm
    tile_end = jnp.cumsum(tiles_per)
    pad_start = (tile_end - tiles_per) * tm
    start = jnp.cumsum(counts) - counts
    dest = pad_start[sorted_e] + (jnp.arange(a, dtype=I32) - start[sorted_e])
    row_token = jnp.zeros((n_tiles * tm,), I32).at[dest].set(order // TOP_K)
    pos = jnp.zeros((a,), I32).at[order].set(dest).reshape(n, TOP_K)
    tile_ids = jnp.arange(n_tiles, dtype=I32)
    used = tile_end[-1]
    tile_valid = (tile_ids < used).astype(I32)
    tile_expert = jnp.searchsorted(tile_end, jnp.minimum(tile_ids, used - 1), side="right").astype(I32)
    tile_expert = jnp.minimum(tile_expert, N_EXPERTS - 1)
    return tile_expert, tile_valid, row_token, pos


def _pos_tiles(pos):
    n = pos.shape[0]
    tm = COMBINE_TILE
    return pos.reshape(n // tm, tm, TOP_K).transpose(0, 2, 1).reshape(n // tm, 1, TOP_K * tm)


def _rope_tables(positions):
    half = MLA_ROPE // 2
    inv_freq = 1.0 / (ROPE_THETA ** (jnp.arange(half, dtype=F32) / half))
    ang = positions.astype(F32)[:, None] * inv_freq[None, :]
    cos, sin = jnp.cos(ang), jnp.sin(ang)
    zeros = jnp.zeros((positions.shape[0], LANES - MLA_ROPE), F32)
    return (jnp.concatenate([cos, cos, zeros], axis=1), jnp.concatenate([-sin, sin, zeros], axis=1))


def _in_proj_weight(w_in):
    o = np.cumsum([0, CONV_CH, GDN_HEADS * GDN_D, GDN_HEADS, GDN_HEADS, MLA_Q_RANK, MLA_KV_RANK, MLA_ROPE])
    conv, z, bl, al, cq, ckv, kpe = (w_in[:, o[i]:o[i + 1]] for i in range(7))
    pad = jnp.zeros((w_in.shape[0], LANES - MLA_ROPE - 2 * GDN_HEADS), w_in.dtype)
    return jnp.concatenate([conv, z, cq, ckv, kpe, bl, al, pad], axis=1).astype(BF16)


IN_SPLITS = (CONV_CH, GDN_HEADS * GDN_D, MLA_Q_RANK, MLA_KV_RANK, LANES)


def _lane_row(vals, lane0):
    return jnp.zeros((1, LANES), F32).at[0, lane0:lane0 + vals.shape[0]].set(vals.astype(F32))


def kernel(x_prompt, x_sample, state_gdn, state_conv, cache_latent, cache_krope, page_table, cache_mem_k, cache_mem_v, mem_prompt, norm_mix, w_in, conv_w, a_log, dt_bias, gdn_norm, q_norm, w_uq, kv_norm, w_uk, w_uv, w_out, norm_x, mem_norm, w_xq, w_xk, w_xv, w_xo, norm_ffn, w_router, b_router, w_e1, b_e1, w_e2, b_e2, norm_final):
    depth = w_in.shape[0]
    assert depth == 1
    L = 0
    bp, t, d = x_prompt.shape
    bs = x_sample.shape[0]
    assert x_sample.shape[1] == 1
    past_len = page_table.shape[1] * cache_latent.shape[2]
    mp = bp * t

    w_in_p = _in_proj_weight(w_in[L])
    alog_row = _lane_row(a_log[L], DECAY_LANE)
    dtb_row = _lane_row(dt_bias[L], DECAY_LANE)
    wuq = w_uq[L]
    wuq_p = jnp.concatenate(
        [wuq, jnp.zeros(wuq.shape[:2] + (MLA_QK_PAD - wuq.shape[2],), wuq.dtype)], axis=2
    ).reshape(wuq.shape[0], MLA_HEADS * MLA_QK_PAD).astype(BF16)
    wuk = w_uk[L].reshape(MLA_KV_RANK, MLA_HEADS * MLA_NOPE).astype(BF16)
    wuv = w_uv[L].reshape(MLA_KV_RANK, MLA_HEADS * MLA_V).astype(BF16)
    wuk_t = jnp.transpose(w_uk[L], (1, 2, 0)).astype(BF16)
    w_out_b = w_out[L].astype(BF16)
    n_gdn = GDN_HEADS * GDN_D
    w_xq_b, w_xo_b = w_xq[L].astype(BF16), w_xo[L].astype(BF16)
    w_xkv_b = jnp.concatenate([w_xk[L], w_xv[L]], axis=1).astype(BF16)
    wr = jnp.concatenate([w_router[L], jnp.zeros((d, LANES - N_EXPERTS), F32)], axis=1)
    wr_hi = wr.astype(BF16)
    wr_lo = (wr - wr_hi.astype(F32)).astype(BF16)
    br_row = _lane_row(b_router[L], 0)
    w1 = w_e1[L]
    w1g, w1l = w1[:, :, 0::2].astype(BF16), w1[:, :, 1::2].astype(BF16)
    b1g, b1l = b_e1[L][:, None, 0::2], b_e1[L][:, None, 1::2]
    w2 = w_e2[L].astype(BF16)
    b2 = b_e2[L][:, None, :]
    cos_p, sin_p = _rope_tables(jnp.arange(t, dtype=I32))
    cos_s, sin_s = _rope_tables(jnp.full((bs,), past_len, I32))

    xp = x_prompt.reshape(mp, d)
    xs = x_sample.reshape(bs, d)
    conv_p, z_p, cq_p, ckv_p, kba_p = fused_linear([xp], [w_in_p], gain=norm_mix[L], splits=IN_SPLITS, name="in_proj_prompt")
    conv_s, z_s, cq_s, ckv_s, kba_s = fused_linear([xs], [w_in_p], gain=norm_mix[L], splits=IN_SPLITS, name="in_proj_sample")

    y_gdn_p, gdn_state_p = gdn_prompt(conv_p.reshape(bp, t, CONV_CH), z_p.reshape(bp, t, n_gdn),
                                      kba_p.reshape(bp, t, LANES), conv_w[L], alog_row, dtb_row, gdn_norm[L])
    conv_state_p = conv_p.reshape(bp, t, CONV_CH)[:, t - (CONV_WIDTH - 1):, :]
    gdn_state_s, conv_state_s, y_gdn_s = gdn_sample(conv_s, state_conv[L], kba_s, z_s, state_gdn[L],
                                                    conv_w[L], alog_row, dtb_row, gdn_norm[L])

    q_p, k_p, v_p, lat_p, kpe_p = mla_prep(cq_p, ckv_p, kba_p, cos_p, sin_p, q_norm[L], kv_norm[L],
                                           wuq_p, wuk, wuv, seq=t, q_dtype=BF16)
    y_mla_p = mla_flash(q_p.reshape(bp, t, -1), k_p.reshape(bp, t, -1), v_p.reshape(bp, t, -1))
    q_s, _, _, lat_s, kpe_s = mla_prep(cq_s, ckv_s, kba_s, cos_s, sin_s, q_norm[L], kv_norm[L],
                                       wuq_p, wuk, wuv, seq=bs, q_dtype=F32)
    q_s4 = q_s.reshape(bs, MLA_HEADS, MLA_QK_PAD)
    qlat = jnp.concatenate(
        [fused_linear([q_s4[:, h, :MLA_NOPE]], [wuk_t[h]], name=f"absorb_q{h}")[0][:, None, :]
         for h in range(MLA_HEADS)], axis=1)
    head_pad = ((0, 0), (0, DEC_HEAD_PAD - MLA_HEADS), (0, 0))
    qlat8 = jnp.pad(qlat, head_pad)
    qpe8 = jnp.pad(q_s4[:, :, MLA_NOPE:MLA_NOPE + MLA_ROPE], head_pad)
    y_mla_s = mla_decode(page_table, qlat8, qpe8, lat_s, kpe_s, wuv,
                         cache_latent[:, L:L + 1], cache_krope[:, L:L + 1]).reshape(bs, -1)

    (h_p,) = fused_linear([y_gdn_p.reshape(mp, n_gdn), y_mla_p.reshape(mp, -1)], [w_out_b[:n_gdn], w_out_b[n_gdn:]],
                          residual=xp, name="out_proj_prompt")
    (h_s,) = fused_linear([y_gdn_s, y_mla_s], [w_out_b[:n_gdn], w_out_b[n_gdn:]], residual=xs, name="out_proj_sample")

    mem_tokens = mem_prompt.shape[1]
    mk_f, mv_f, mk_b, mv_b = fused_linear([mem_prompt.reshape(bp * mem_tokens, d)], [jnp.concatenate([w_xkv_b, w_xkv_b], axis=1)],
                                          gain=mem_norm[L], splits=(d, d, d, d), out_dtypes=(F32, F32, BF16, BF16),
                                          name="memory_kv")
    (qx_p,) = fused_linear([h_p], [w_xq_b], gain=norm_x[L], out_dtypes=(BF16,), name="xq_prompt")
    (qx_s,) = fused_linear([h_s], [w_xq_b], gain=norm_x[L], out_dtypes=(BF16,), name="xq_sample")
    ox_p = xattn_prompt(qx_p.reshape(bp, t, d), mk_b.reshape(bp, mem_tokens, d), mv_b.reshape(bp, mem_tokens, d))
    ox_s = xattn_sample(qx_s.reshape(bs, 1, d), cache_mem_k[L].reshape(bs, mem_tokens, d),
                        cache_mem_v[L].reshape(bs, mem_tokens, d))

    h2_p, hn_p, gate_p, ids_p = post_xattn(ox_p.reshape(mp, d), w_xo_b, h_p, norm_ffn[L], wr_hi, wr_lo, br_row)
    h2_s, hn_s, gate_s, ids_s = post_xattn(ox_s.reshape(bs, d), w_xo_b, h_s, norm_ffn[L], wr_hi, wr_lo, br_row)

    hn_all = jnp.concatenate([hn_p, hn_s], axis=0)
    ids_all = jnp.concatenate([ids_p[:, :TOP_K], ids_s[:, :TOP_K]], axis=0)
    tile_expert, tile_valid, row_token, pos = _route(ids_all)
    y_sorted = moe_experts(tile_expert, tile_valid, row_token, hn_all, w1g, w1l, b1g, b1l, w2, b2)
    y_prompt = moe_combine(_pos_tiles(pos[:mp]), y_sorted, h2_p, gate_p, norm_final)
    y_sample = moe_combine(_pos_tiles(pos[mp:]), y_sorted, h2_s, gate_s, norm_final)

    x_heads = X_HEADS
    return (y_prompt.reshape(bp, t, d), y_sample.reshape(bs, 1, d),
            gdn_state_p[None], conv_state_p[None],
            lat_p.reshape(bp, 1, t, MLA_KV_RANK), kpe_p.reshape(bp, 1, t, MLA_ROPE),
            mk_f.reshape(1, bp, mem_tokens, x_heads, d // x_heads), mv_f.reshape(1, bp, mem_tokens, x_heads, d // x_heads),
            gdn_state_s[None], conv_state_s[None],
            lat_s.reshape(bs, 1, 1, MLA_KV_RANK), kpe_s.reshape(bs, 1, 1, MLA_ROPE))
```

```python
import functools

import jax
import jax.numpy as jnp
import numpy as np
from jax import lax
from jax.experimental import pallas as pl
from jax.experimental.pallas import tpu as pltpu

F32 = jnp.float32
BF16 = jnp.bfloat16
I32 = jnp.int32

NORM_EPS = 1e-6
LANES = 128
SUBLANES = 8
VMEM_LIMIT = 48 * 1024 * 1024
MOE_VMEM_LIMIT = 56 * 1024 * 1024

GDN_HEADS = 4
GDN_D = 128
CONV_WIDTH = 4
CONV_CH = 3 * GDN_HEADS * GDN_D
GDN_BLOCK = 128
MLA_HEADS = 4
MLA_NOPE = 128
MLA_ROPE = 64
MLA_V = 128
MLA_Q_RANK = 384
MLA_KV_RANK = 256
MLA_QK_PAD = 256
ROPE_THETA = 10000.0
MLA_SCALE = (MLA_NOPE + MLA_ROPE) ** -0.5
X_HEADS = 4
N_EXPERTS = 32
TOP_K = 4
SWIGLU_LIMIT = 7.0
SWIGLU_ALPHA = 1.702
BETA_LANE = MLA_ROPE
DECAY_LANE = MLA_ROPE + GDN_HEADS
NEG_BIG = -1e30


def _cparams(*sem):
    return pltpu.CompilerParams(dimension_semantics=sem, vmem_limit_bytes=VMEM_LIMIT)


def _rms(x, gain):
    return x * lax.rsqrt(jnp.mean(x * x, axis=-1, keepdims=True) + NORM_EPS) * gain


def _mm(a, b):
    return jnp.dot(a.astype(BF16), b.astype(BF16), preferred_element_type=F32)


def _mm_nt(a, b):
    return lax.dot_general(a.astype(BF16), b.astype(BF16), (((1,), (1,)), ((), ())),
                           preferred_element_type=F32)


def _mm3(a, b):
    a_hi = a.astype(BF16)
    b_hi = b.astype(BF16)
    a_lo = (a - a_hi.astype(F32)).astype(BF16)
    b_lo = (b - b_hi.astype(F32)).astype(BF16)
    dot = functools.partial(jnp.dot, preferred_element_type=F32)
    return dot(a_hi, b_hi) + dot(a_hi, b_lo) + dot(a_lo, b_hi)


def _sigmoid(x):
    return 1.0 / (1.0 + jnp.exp(-x))


def _softplus(x):
    return jnp.maximum(x, 0.0) + jnp.log1p(jnp.exp(-jnp.abs(x)))


def _linear_kernel(*refs, n_in, has_gain, has_res, splits):
    a_refs = refs[:n_in]
    w_refs = refs[n_in:2 * n_in]
    pos = 2 * n_in
    g_ref = refs[pos] if has_gain else None
    pos += int(has_gain)
    r_ref = refs[pos] if has_res else None
    pos += int(has_res)
    out_refs = refs[pos:]
    a0 = a_refs[0][...]
    if has_gain:
        a0 = _rms(a0.astype(F32), g_ref[...])
    acts = [a0.astype(BF16)] + [a[...].astype(BF16) for a in a_refs[1:]]
    off = 0
    for i, width in enumerate(splits):
        acc = None
        for a, w in zip(acts, w_refs):
            d = jnp.dot(a, w[:, off:off + width], preferred_element_type=F32)
            acc = d if acc is None else acc + d
        if has_res:
            acc = acc + r_ref[:, off:off + width]
        for o_ref in out_refs[i::len(splits)]:
            o_ref[...] = acc.astype(o_ref.dtype)
        off += width


def fused_linear(acts, weights, *, gain=None, residual=None, splits=None, out_dtypes=None,
                 bf16_copies=False, tm=256, name="fused_linear"):
    m = acts[0].shape[0]
    n = weights[0].shape[1]
    tm = min(tm, m)
    assert m % tm == 0
    splits = tuple(splits) if splits is not None else (n,)
    assert sum(splits) == n and all(s % LANES == 0 for s in splits)
    out_dtypes = tuple(out_dtypes) if out_dtypes is not None else (F32,) * len(splits)
    out_widths = splits
    if bf16_copies:
        out_widths = splits + splits
        out_dtypes = out_dtypes + (BF16,) * len(splits)
    in_specs = [pl.BlockSpec((tm, a.shape[1]), lambda i: (i, 0)) for a in acts]
    in_specs += [pl.BlockSpec(w.shape, lambda i: (0, 0)) for w in weights]
    args = list(acts) + list(weights)
    if gain is not None:
        in_specs.append(pl.BlockSpec((1, gain.shape[-1]), lambda i: (0, 0)))
        args.append(gain.reshape(1, -1))
    if residual is not None:
        in_specs.append(pl.BlockSpec((tm, n), lambda i: (i, 0)))
        args.append(residual)
    outs = pl.pallas_call(
        functools.partial(_linear_kernel, n_in=len(acts), has_gain=gain is not None,
                          has_res=residual is not None, splits=splits),
        out_shape=[jax.ShapeDtypeStruct((m, s), dt) for s, dt in zip(out_widths, out_dtypes)],
        grid=(m // tm,),
        in_specs=in_specs,
        out_specs=[pl.BlockSpec((tm, s), lambda i: (i, 0)) for s in out_widths],
        compiler_params=_cparams("parallel"),
        name=name,
    )(*args)
    return outs


def _gate_values(kba, alog_row, dtb_row):
    beta = _sigmoid(kba)
    g = -jnp.exp(alog_row) * _softplus(kba + dtb_row)
    return beta, g


def _l2norm(x):
    return x * lax.rsqrt(jnp.sum(x * x, axis=-1, keepdims=True) + NORM_EPS)


def _gdn_prompt_kernel(x_ref, z_ref, kba_ref, cw_ref, alog_ref, dtb_ref, gn_ref,
                       y_ref, s_out_ref, xbuf, state):
    t = pl.program_id(1)
    nt = pl.num_programs(1)
    blk = GDN_BLOCK
    hist = CONV_WIDTH - 1

    @pl.when(t == 0)
    def _():
        xbuf[0:SUBLANES, :] = jnp.zeros((SUBLANES, CONV_CH), F32)
        state[...] = jnp.zeros(state.shape, F32)

    xbuf[SUBLANES:SUBLANES + blk, :] = x_ref[0]
    cw = cw_ref[...]
    conv = xbuf[SUBLANES - hist:SUBLANES - hist + blk, :] * cw[0:1, :]
    for j in range(1, CONV_WIDTH):
        conv = conv + xbuf[SUBLANES - hist + j:SUBLANES - hist + j + blk, :] * cw[j:j + 1, :]
    xbuf[SUBLANES - hist:SUBLANES, :] = xbuf[SUBLANES + blk - hist:SUBLANES + blk, :]
    c = conv * _sigmoid(conv)

    kba = kba_ref[0]
    beta_all, g_all = _gate_values(kba, alog_ref[...], dtb_ref[...])
    row = lax.broadcasted_iota(I32, (blk, blk), 0)
    col = lax.broadcasted_iota(I32, (blk, blk), 1)
    gc = g_all
    shift = 1
    while shift < blk:
        rolled = pltpu.roll(gc, shift, 0)
        gc = gc + jnp.where(row >= shift, rolled, 0.0)
        shift *= 2
    gc_t = gc.T
    incl = row >= col
    strict = row > col
    eye = (row == col).astype(F32)
    z = z_ref[0]
    gn = gn_ref[...]
    nh = GDN_HEADS * GDN_D

    for h in range(GDN_HEADS):
        q = _l2norm(c[:, h * GDN_D:(h + 1) * GDN_D]) * (GDN_D ** -0.5)
        k = _l2norm(c[:, nh + h * GDN_D:nh + (h + 1) * GDN_D])
        v = c[:, 2 * nh + h * GDN_D:2 * nh + (h + 1) * GDN_D]
        bcol = beta_all[:, BETA_LANE + h:BETA_LANE + h + 1]
        gcol = gc[:, DECAY_LANE + h:DECAY_LANE + h + 1]
        grow = gc_t[DECAY_LANE + h:DECAY_LANE + h + 1, :]
        decay = jnp.exp(jnp.where(incl, gcol - grow, NEG_BIG))
        kb = k * bcol
        vb = v * bcol
        a = jnp.where(strict, _mm_nt(kb, k) * decay, 0.0)
        x = eye - a
        p = _mm3(a, a)
        x = x + _mm3(x, p)
        for _ in range(5):
            p = _mm3(p, p)
            x = x + _mm3(x, p)
        egc = jnp.exp(gcol)
        u = _mm(x, vb)
        w = _mm(x, kb * egc)
        intra = _mm_nt(q, k) * decay
        s_h = state[h]
        v_new = u - _mm(w, s_h)
        o = _mm(q * egc, s_h) + _mm(intra, v_new)
        g_last = gcol[blk - 1:blk, :]
        kd = k * jnp.exp(g_last - gcol)
        state[h] = s_h * jnp.exp(g_last) + _mm(kd.T, v_new)
        zz = z[:, h * GDN_D:(h + 1) * GDN_D]
        y_ref[0, :, h * GDN_D:(h + 1) * GDN_D] = (_rms(o, gn) * (zz * _sigmoid(zz))).astype(y_ref.dtype)

    @pl.when(t == nt - 1)
    def _():
        s_out_ref[0] = state[...]


def gdn_prompt(conv_in, z, kba, conv_w, alog_row, dtb_row, gdn_norm):
    b, t, _ = conv_in.shape
    assert t % GDN_BLOCK == 0
    nt = t // GDN_BLOCK
    y, s = pl.pallas_call(
        _gdn_prompt_kernel,
        out_shape=[jax.ShapeDtypeStruct((b, t, GDN_HEADS * GDN_D), BF16),
                   jax.ShapeDtypeStruct((b, GDN_HEADS, GDN_D, GDN_D), F32)],
        grid=(b, nt),
        in_specs=[pl.BlockSpec((1, GDN_BLOCK, CONV_CH), lambda i, j: (i, j, 0)),
                  pl.BlockSpec((1, GDN_BLOCK, GDN_HEADS * GDN_D), lambda i, j: (i, j, 0)),
                  pl.BlockSpec((1, GDN_BLOCK, LANES), lambda i, j: (i, j, 0)),
                  pl.BlockSpec((CONV_WIDTH, CONV_CH), lambda i, j: (0, 0)),
                  pl.BlockSpec((1, LANES), lambda i, j: (0, 0)),
                  pl.BlockSpec((1, LANES), lambda i, j: (0, 0)),
                  pl.BlockSpec((1, GDN_D), lambda i, j: (0, 0))],
        out_specs=[pl.BlockSpec((1, GDN_BLOCK, GDN_HEADS * GDN_D), lambda i, j: (i, j, 0)),
                   pl.BlockSpec((1, GDN_HEADS, GDN_D, GDN_D), lambda i, j: (i, 0, 0, 0))],
        scratch_shapes=[pltpu.VMEM((SUBLANES + GDN_BLOCK, CONV_CH), F32),
                        pltpu.VMEM((GDN_HEADS, GDN_D, GDN_D), F32)],
        compiler_params=_cparams("parallel", "arbitrary"),
        name="gdn_prompt",
    )(conv_in, z, kba, conv_w, alog_row, dtb_row, gdn_norm.reshape(1, -1))
    return y, s


GDN_SAMPLE_BB = 8


def _gdn_sample_kernel(x_ref, sc_ref, kba_ref, z_ref, s_ref, cw_ref, alog_ref, dtb_ref, gn_ref,
                       s_out_ref, sc_out_ref, y_ref, tbuf):
    bb = GDN_SAMPLE_BB
    x = x_ref[...]
    cw = cw_ref[...]
    conv = x * cw[CONV_WIDTH - 1:CONV_WIDTH, :]
    for j in range(CONV_WIDTH - 1):
        conv = conv + sc_ref[:, j, :] * cw[j:j + 1, :]
    for j in range(CONV_WIDTH - 2):
        sc_out_ref[:, j, :] = sc_ref[:, j + 1, :]
    sc_out_ref[:, CONV_WIDTH - 2, :] = x
    c = conv * _sigmoid(conv)
    beta_all, g_all = _gate_values(kba_ref[...], alog_ref[...], dtb_ref[...])
    eg_all = jnp.exp(g_all)
    z = z_ref[...]
    gn = gn_ref[...]
    nh = GDN_HEADS * GDN_D
    tbuf[...] = jnp.zeros(tbuf.shape, F32)
    for h in range(GDN_HEADS):
        q = _l2norm(c[:, h * GDN_D:(h + 1) * GDN_D]) * (GDN_D ** -0.5)
        k = _l2norm(c[:, nh + h * GDN_D:nh + (h + 1) * GDN_D])
        v = c[:, 2 * nh + h * GDN_D:2 * nh + (h + 1) * GDN_D]
        tbuf[0:bb, :] = q
        q_t = tbuf[...].T
        tbuf[0:bb, :] = k
        k_t = tbuf[...].T
        for b in range(bb):
            qcol = q_t[:, b:b + 1]
            kcol = k_t[:, b:b + 1]
            eg = eg_all[b:b + 1, DECAY_LANE + h:DECAY_LANE + h + 1]
            beta = beta_all[b:b + 1, BETA_LANE + h:BETA_LANE + h + 1]
            s1 = s_ref[b, h] * eg
            pred = jnp.sum(s1 * kcol, axis=0, keepdims=True)
            u = (v[b:b + 1, :] - pred) * beta
            s2 = s1 + kcol * u
            s_out_ref[b, h] = s2
            o = jnp.sum(s2 * qcol, axis=0, keepdims=True)
            zz = z[b:b + 1, h * GDN_D:(h + 1) * GDN_D]
            y_ref[b:b + 1, h * GDN_D:(h + 1) * GDN_D] = (_rms(o, gn) * (zz * _sigmoid(zz))).astype(y_ref.dtype)


def gdn_sample(conv_in, state_conv, kba, z, state_gdn, conv_w, alog_row, dtb_row, gdn_norm):
    b = conv_in.shape[0]
    bb = GDN_SAMPLE_BB
    assert b % bb == 0
    hist = CONV_WIDTH - 1
    return pl.pallas_call(
        _gdn_sample_kernel,
        out_shape=[jax.ShapeDtypeStruct(state_gdn.shape, F32),
                   jax.ShapeDtypeStruct(state_conv.shape, F32),
                   jax.ShapeDtypeStruct((b, GDN_HEADS * GDN_D), BF16)],
        grid=(b // bb,),
        in_specs=[pl.BlockSpec((bb, CONV_CH), lambda i: (i, 0)),
                  pl.BlockSpec((bb, hist, CONV_CH), lambda i: (i, 0, 0)),
                  pl.BlockSpec((bb, LANES), lambda i: (i, 0)),
                  pl.BlockSpec((bb, GDN_HEADS * GDN_D), lambda i: (i, 0)),
                  pl.BlockSpec((bb, GDN_HEADS, GDN_D, GDN_D), lambda i: (i, 0, 0, 0)),
                  pl.BlockSpec((CONV_WIDTH, CONV_CH), lambda i: (0, 0)),
                  pl.BlockSpec((1, LANES), lambda i: (0, 0)),
                  pl.BlockSpec((1, LANES), lambda i: (0, 0)),
                  pl.BlockSpec((1, GDN_D), lambda i: (0, 0))],
        out_specs=[pl.BlockSpec((bb, GDN_HEADS, GDN_D, GDN_D), lambda i: (i, 0, 0, 0)),
                   pl.BlockSpec((bb, hist, CONV_CH), lambda i: (i, 0, 0)),
                   pl.BlockSpec((bb, GDN_HEADS * GDN_D), lambda i: (i, 0))],
        scratch_shapes=[pltpu.VMEM((GDN_D, GDN_D), F32)],
        compiler_params=_cparams("parallel"),
        name="gdn_sample",
    )(conv_in, state_conv, kba, z, state_gdn, conv_w, alog_row, dtb_row, gdn_norm.reshape(1, -1))


def _rope128(x, cos, sin):
    half = MLA_ROPE // 2
    lane = lax.broadcasted_iota(I32, x.shape, 1)
    swapped = jnp.where(lane < half, pltpu.roll(x, LANES - half, 1), pltpu.roll(x, half, 1))
    return x * cos + swapped * sin


def _mla_prep_kernel(cq_ref, ckv_ref, kba_ref, cos_ref, sin_ref, qn_ref, kvn_ref, wuq_ref, wuk_ref, wuv_ref,
                     q_ref, k_ref, v_ref, lat_ref, kpe_ref):
    cos = cos_ref[...]
    sin = sin_ref[...]
    qn = _rms(cq_ref[...], qn_ref[...]).astype(BF16)
    lat = _rms(ckv_ref[...], kvn_ref[...])
    lat_ref[...] = lat
    lat_b = lat.astype(BF16)
    kpe = _rope128(kba_ref[...], cos, sin)
    kpe_ref[...] = kpe[:, :MLA_ROPE]
    for h in range(MLA_HEADS):
        lo = h * MLA_QK_PAD
        q_ref[:, lo:lo + MLA_NOPE] = jnp.dot(
            qn, wuq_ref[:, lo:lo + MLA_NOPE], preferred_element_type=F32).astype(q_ref.dtype)
        q_pe = jnp.dot(qn, wuq_ref[:, lo + MLA_NOPE:lo + MLA_QK_PAD], preferred_element_type=F32)
        q_ref[:, lo + MLA_NOPE:lo + MLA_QK_PAD] = _rope128(q_pe, cos, sin).astype(q_ref.dtype)
        k_ref[:, lo:lo + MLA_NOPE] = jnp.dot(
            lat_b, wuk_ref[:, h * MLA_NOPE:(h + 1) * MLA_NOPE], preferred_element_type=F32).astype(k_ref.dtype)
        k_ref[:, lo + MLA_NOPE:lo + MLA_QK_PAD] = kpe.astype(k_ref.dtype)
    v_ref[...] = jnp.dot(lat_b, wuv_ref[...], preferred_element_type=F32).astype(v_ref.dtype)


def mla_prep(c_q, c_kv, kba, cos_tab, sin_tab, q_norm, kv_norm, wuq_p, wuk, wuv, *, seq, q_dtype, tm=256):
    m = c_q.shape[0]
    tm = min(tm, m, seq)
    assert m % tm == 0 and seq % tm == 0
    nseq = seq // tm
    hq = MLA_HEADS * MLA_QK_PAD
    row = lambda i: (i, 0)
    const = lambda i: (0, 0)
    return pl.pallas_call(
        _mla_prep_kernel,
        out_shape=[jax.ShapeDtypeStruct((m, hq), q_dtype),
                   jax.ShapeDtypeStruct((m, hq), BF16),
                   jax.ShapeDtypeStruct((m, MLA_HEADS * MLA_V), BF16),
                   jax.ShapeDtypeStruct((m, MLA_KV_RANK), F32),
                   jax.ShapeDtypeStruct((m, MLA_ROPE), F32)],
        grid=(m // tm,),
        in_specs=[pl.BlockSpec((tm, MLA_Q_RANK), row),
                  pl.BlockSpec((tm, MLA_KV_RANK), row),
                  pl.BlockSpec((tm, LANES), row),
                  pl.BlockSpec((tm, LANES), lambda i: (i % nseq, 0)),
                  pl.BlockSpec((tm, LANES), lambda i: (i % nseq, 0)),
                  pl.BlockSpec((1, MLA_Q_RANK), const),
                  pl.BlockSpec((1, MLA_KV_RANK), const),
                  pl.BlockSpec(wuq_p.shape, const),
                  pl.BlockSpec(wuk.shape, const),
                  pl.BlockSpec(wuv.shape, const)],
        out_specs=[pl.BlockSpec((tm, hq), row),
                   pl.BlockSpec((tm, hq), row),
                   pl.BlockSpec((tm, MLA_HEADS * MLA_V), row),
                   pl.BlockSpec((tm, MLA_KV_RANK), row),
                   pl.BlockSpec((tm, MLA_ROPE), row)],
        compiler_params=_cparams("parallel"),
        name="mla_prep",
    )(c_q, c_kv, kba, cos_tab, sin_tab, q_norm.reshape(1, -1), kv_norm.reshape(1, -1), wuq_p, wuk, wuv)


def _flash_kernel(q_ref, k_ref, v_ref, o_ref, m_s, l_s, acc_s, *, tq, tk):
    qi = pl.program_id(2)
    ki = pl.program_id(3)

    @pl.when(ki == 0)
    def _():
        m_s[...] = jnp.full(m_s.shape, NEG_BIG, F32)
        l_s[...] = jnp.zeros(l_s.shape, F32)
        acc_s[...] = jnp.zeros(acc_s.shape, F32)

    @pl.when(ki * tk <= qi * tq + (tq - 1))
    def _():
        s = lax.dot_general(q_ref[0], k_ref[0], (((1,), (1,)), ((), ())), preferred_element_type=F32) * MLA_SCALE
        qpos = qi * tq + lax.broadcasted_iota(I32, (tq, tk), 0)
        kpos = ki * tk + lax.broadcasted_iota(I32, (tq, tk), 1)
        s = jnp.where(kpos <= qpos, s, NEG_BIG)
        m_prev = m_s[...]
        m_new = jnp.maximum(m_prev, jnp.max(s, axis=-1, keepdims=True))
        corr = jnp.exp(m_prev - m_new)
        p = jnp.exp(s - m_new)
        l_s[...] = l_s[...] * corr + jnp.sum(p, axis=-1, keepdims=True)
        acc_s[...] = acc_s[...] * corr + jnp.dot(p.astype(BF16), v_ref[0], preferred_element_type=F32)
        m_s[...] = m_new

    @pl.when(ki == pl.num_programs(3) - 1)
    def _():
        o_ref[0] = (acc_s[...] / l_s[...]).astype(o_ref.dtype)


def mla_flash(q, k, v, *, tq=512, tk=512):
    b, t, _ = q.shape
    tq = min(tq, t)
    tk = min(tk, t)
    assert t % tq == 0 and t % tk == 0
    last_k = lambda qi: (qi * tq + tq - 1) // tk

    def kv_map(bi, h, qi, ki):
        return (bi, jnp.minimum(ki, last_k(qi)), h)

    return pl.pallas_call(
        functools.partial(_flash_kernel, tq=tq, tk=tk),
        out_shape=jax.ShapeDtypeStruct((b, t, MLA_HEADS * MLA_V), BF16),
        grid=(b, MLA_HEADS, t // tq, t // tk),
        in_specs=[pl.BlockSpec((1, tq, MLA_QK_PAD), lambda bi, h, qi, ki: (bi, qi, h)),
                  pl.BlockSpec((1, tk, MLA_QK_PAD), kv_map),
                  pl.BlockSpec((1, tk, MLA_V), kv_map)],
        out_specs=pl.BlockSpec((1, tq, MLA_V), lambda bi, h, qi, ki: (bi, qi, h)),
        scratch_shapes=[pltpu.VMEM((tq, 1), F32), pltpu.VMEM((tq, 1), F32), pltpu.VMEM((tq, MLA_V), F32)],
        compiler_params=_cparams("parallel", "parallel", "parallel", "arbitrary"),
        name="mla_flash",
    )(q, k, v)


DEC_HEAD_PAD = 8
DEC_GROUP = 16
DEC_SLOTS = 4


def _decode_kernel(pt_ref, ptn_ref, qlat_ref, qpe_ref, latn_ref, kpen_ref, wuv_ref, lat_hbm, kpe_hbm,
                   o_ref, latbuf, kpebuf, sem, *, nb, n_pages, page):
    b = pl.program_id(0)
    grp = DEC_GROUP
    ns = DEC_SLOTS
    n_groups = n_pages // grp
    n_outer = n_groups // ns

    def page_copies(pg, slot, j):
        return (pltpu.make_async_copy(lat_hbm.at[pg, 0], latbuf.at[slot, j], sem.at[0, slot]),
                pltpu.make_async_copy(kpe_hbm.at[pg, 0], kpebuf.at[slot, j], sem.at[1, slot]))

    def start_group(tbl_ref, g, slot):
        for j in range(grp):
            for cp in page_copies(tbl_ref[0, 0, g * grp + j], slot, j):
                cp.start()

    def wait_group(slot):
        for j in range(grp):
            for cp in page_copies(0, slot, j):
                cp.wait()

    @pl.when(b == 0)
    def _():
        for s in range(ns - 1):
            start_group(pt_ref, s, s)

    qlat = qlat_ref[0]
    qpe = qpe_ref[0]
    latn = latn_ref[0]
    kpen = kpen_ref[0]
    qlat_b = qlat.astype(BF16)
    qpe_b = qpe.astype(BF16)
    s_new = (jnp.sum(qlat * latn, axis=-1, keepdims=True)
             + jnp.sum(qpe * kpen, axis=-1, keepdims=True)) * MLA_SCALE
    m0 = s_new
    l0 = jnp.ones_like(s_new)
    acc0 = jnp.broadcast_to(latn, qlat.shape)

    def consume(slot, carry):
        m, l, acc = carry
        lat = latbuf[slot].reshape(grp * page, MLA_KV_RANK).astype(BF16)
        kpe_t = jnp.concatenate([kpebuf[slot, j] for j in range(grp)], axis=1).astype(BF16)
        s = (lax.dot_general(qlat_b, lat, (((1,), (1,)), ((), ())), preferred_element_type=F32)
             + jnp.dot(qpe_b, kpe_t, preferred_element_type=F32)) * MLA_SCALE
        m_new = jnp.maximum(m, jnp.max(s, axis=-1, keepdims=True))
        corr = jnp.exp(m - m_new)
        p = jnp.exp(s - m_new)
        l = l * corr + jnp.sum(p, axis=-1, keepdims=True)
        acc = acc * corr + jnp.dot(p.astype(BF16), lat, preferred_element_type=F32)
        return m_new, l, acc

    def ring_body(i, carry):
        for j in range(ns):
            ahead = (j + ns - 1) % ns
            if j == 0:
                start_group(pt_ref, i * ns + ns - 1, ahead)
            else:
                @pl.when(i + 1 < n_outer)
                def _():
                    start_group(pt_ref, (i + 1) * ns + ahead, ahead)

                @pl.when(jnp.logical_and(i + 1 >= n_outer, b + 1 < nb))
                def _():
                    start_group(ptn_ref, ahead, ahead)
            wait_group(j)
            carry = consume(j, carry)
        return carry

    m, l, acc = lax.fori_loop(0, n_outer, ring_body, (m0, l0, acc0))
    o_lat = (acc / l).astype(BF16)
    res = jnp.dot(o_lat, wuv_ref[...], preferred_element_type=F32)
    o_ref[0] = jnp.concatenate(
        [res[h:h + 1, h * MLA_V:(h + 1) * MLA_V] for h in range(MLA_HEADS)], axis=1).astype(o_ref.dtype)


def mla_decode(page_table, qlat8, qpe8, lat_new, kpe_new, wuv, cache_latent, cache_krope_t):
    b, n_pages = page_table.shape
    page = cache_latent.shape[2]
    assert n_pages % (DEC_SLOTS * DEC_GROUP) == 0
    pt3 = page_table.reshape(b, 1, n_pages)
    smem_row = lambda f: pl.BlockSpec((1, 1, n_pages), f, memory_space=pltpu.SMEM)
    return pl.pallas_call(
        functools.partial(_decode_kernel, nb=b, n_pages=n_pages, page=page),
        out_shape=jax.ShapeDtypeStruct((b, 1, MLA_HEADS * MLA_V), BF16),
        grid=(b,),
        in_specs=[smem_row(lambda i: (i, 0, 0)),
                  smem_row(lambda i: (jnp.minimum(i + 1, b - 1), 0, 0)),
                  pl.BlockSpec((1, DEC_HEAD_PAD, MLA_KV_RANK), lambda i: (i, 0, 0)),
                  pl.BlockSpec((1, DEC_HEAD_PAD, MLA_ROPE), lambda i: (i, 0, 0)),
                  pl.BlockSpec((1, 1, MLA_KV_RANK), lambda i: (i, 0, 0)),
                  pl.BlockSpec((1, 1, MLA_ROPE), lambda i: (i, 0, 0)),
                  pl.BlockSpec(wuv.shape, lambda i: (0, 0)),
                  pl.BlockSpec(memory_space=pl.ANY),
                  pl.BlockSpec(memory_space=pl.ANY)],
        out_specs=pl.BlockSpec((1, 1, MLA_HEADS * MLA_V), lambda i: (i, 0, 0)),
        scratch_shapes=[pltpu.VMEM((DEC_SLOTS, DEC_GROUP, page, MLA_KV_RANK), F32),
                        pltpu.VMEM((DEC_SLOTS, DEC_GROUP, MLA_ROPE, page), F32),
                        pltpu.SemaphoreType.DMA((2, DEC_SLOTS))],
        compiler_params=_cparams("arbitrary"),
        name="mla_decode",
    )(pt3, pt3, qlat8, qpe8, lat_new.reshape(b, 1, -1), kpe_new.reshape(b, 1, -1), wuv,
      cache_latent, cache_krope_t)


def _softmax_rows(s):
    m = jnp.max(s, axis=-1, keepdims=True)
    p = jnp.exp(s - m)
    return p / jnp.sum(p, axis=-1, keepdims=True)


def _xattn_prompt_kernel(q_ref, k_ref, v_ref, o_ref, *, dh):
    scale = dh ** -0.5
    for h in range(X_HEADS):
        q = q_ref[0, :, h * dh:(h + 1) * dh]
        s = lax.dot_general(q, k_ref[0, :, h * dh:(h + 1) * dh], (((1,), (1,)), ((), ())),
                            preferred_element_type=F32) * scale
        p = _softmax_rows(s).astype(BF16)
        o_ref[0, :, h * dh:(h + 1) * dh] = jnp.dot(
            p, v_ref[0, :, h * dh:(h + 1) * dh], preferred_element_type=F32).astype(o_ref.dtype)


def xattn_prompt(q, k, v, *, tq=512):
    b, t, d = q.shape
    mem = k.shape[1]
    tq = min(tq, t)
    return pl.pallas_call(
        functools.partial(_xattn_prompt_kernel, dh=d // X_HEADS),
        out_shape=jax.ShapeDtypeStruct((b, t, d), BF16),
        grid=(b, t // tq),
        in_specs=[pl.BlockSpec((1, tq, d), lambda i, j: (i, j, 0)),
                  pl.BlockSpec((1, mem, d), lambda i, j: (i, 0, 0)),
                  pl.BlockSpec((1, mem, d), lambda i, j: (i, 0, 0))],
        out_specs=pl.BlockSpec((1, tq, d), lambda i, j: (i, j, 0)),
        compiler_params=_cparams("parallel", "parallel"),
        name="xattn_prompt",
    )(q, k, v)


XATTN_SAMPLE_BB = 2


def _xattn_sample_kernel(q_ref, k_ref, v_ref, o_ref, *, dh):
    scale = dh ** -0.5
    for b in range(XATTN_SAMPLE_BB):
        for h in range(X_HEADS):
            q = jnp.broadcast_to(q_ref[b, :, h * dh:(h + 1) * dh], (SUBLANES, dh)).astype(BF16)
            kh = k_ref[b, :, h, :].astype(BF16)
            vh = v_ref[b, :, h, :].astype(BF16)
            s = lax.dot_general(q, kh, (((1,), (1,)), ((), ())), preferred_element_type=F32) * scale
            p = _softmax_rows(s).astype(BF16)
            o = jnp.dot(p, vh, preferred_element_type=F32)
            o_ref[b, :, h * dh:(h + 1) * dh] = o[0:1, :].astype(o_ref.dtype)


def xattn_sample(q, k, v):
    b, _, d = q.shape
    mem = k.shape[1]
    bb = XATTN_SAMPLE_BB
    assert b % bb == 0
    kv_spec = pl.BlockSpec((bb, mem, X_HEADS, d // X_HEADS), lambda i: (i, 0, 0, 0))
    return pl.pallas_call(
        functools.partial(_xattn_sample_kernel, dh=d // X_HEADS),
        out_shape=jax.ShapeDtypeStruct((b, 1, d), BF16),
        grid=(b // bb,),
        in_specs=[pl.BlockSpec((bb, 1, d), lambda i: (i, 0, 0)), kv_spec, kv_spec],
        out_specs=pl.BlockSpec((bb, 1, d), lambda i: (i, 0, 0)),
        compiler_params=_cparams("parallel"),
        name="xattn_sample",
    )(q, k, v)


def _post_xattn_kernel(ox_ref, wxo_ref, h_ref, g_ref, wr_hi_ref, wr_lo_ref, br_ref, cnt_in_ref,
                       h2_ref, hn_ref, gate_ref, idx_ref, rank_ref, cnt_ref, cnt_s):
    @pl.when(pl.program_id(0) == 0)
    def _():
        cnt_s[...] = cnt_in_ref[...]

    h2 = h_ref[...] + jnp.dot(ox_ref[...], wxo_ref[...], preferred_element_type=F32)
    h2_ref[...] = h2
    hn = _rms(h2, g_ref[...])
    hn_ref[...] = hn
    hn_hi = hn.astype(BF16)
    hn_lo = (hn - hn_hi.astype(F32)).astype(BF16)
    logits = (jnp.dot(hn_hi, wr_hi_ref[...], preferred_element_type=F32)
              + jnp.dot(hn_hi, wr_lo_ref[...], preferred_element_type=F32)
              + jnp.dot(hn_lo, wr_hi_ref[...], preferred_element_type=F32)) + br_ref[...]
    lane = lax.broadcasted_iota(I32, logits.shape, 1)
    logits = jnp.where(lane < N_EXPERTS, logits, NEG_BIG)
    vals, picks = [], []
    gates = jnp.zeros(logits.shape, F32)
    ids = jnp.zeros(logits.shape, I32)
    member = jnp.zeros(logits.shape, F32)
    for k in range(TOP_K):
        m = jnp.max(logits, axis=-1, keepdims=True)
        idx = jnp.min(jnp.where(logits == m, lane, LANES), axis=-1, keepdims=True)
        vals.append(m)
        picks.append(idx)
        ids = jnp.where(lane == k, idx, ids)
        member = jnp.where(lane == idx, 1.0, member)
        logits = jnp.where(lane == idx, NEG_BIG, logits)
    exps = [jnp.exp(v - vals[0]) for v in vals]
    denom = exps[0]
    for e in exps[1:]:
        denom = denom + e
    for k in range(TOP_K):
        gates = jnp.where(lane == k, exps[k] / denom, gates)
    gate_ref[...] = gates
    idx_ref[...] = ids
    tm = logits.shape[0]
    before = (lax.broadcasted_iota(I32, (tm, tm), 0) > lax.broadcasted_iota(I32, (tm, tm), 1)).astype(BF16)
    prior = jnp.dot(before, member.astype(BF16), preferred_element_type=F32) + cnt_s[...]
    ranks = jnp.zeros(logits.shape, I32)
    for k in range(TOP_K):
        rk = jnp.sum(jnp.where(lane == picks[k], prior, 0.0), axis=-1, keepdims=True)
        ranks = jnp.where(lane == k, rk.astype(I32), ranks)
    rank_ref[...] = ranks
    cnt = cnt_s[...] + jnp.sum(member, axis=0, keepdims=True)
    cnt_s[...] = cnt
    cnt_ref[...] = cnt


def post_xattn(ox, wxo, h, norm_ffn, wr_hi, wr_lo, br_row, cnt_in, *, tm=256):
    m, d = h.shape
    tm = min(tm, m)
    row = lambda i: (i, 0)
    const = lambda i: (0, 0)
    return pl.pallas_call(
        _post_xattn_kernel,
        out_shape=[jax.ShapeDtypeStruct((m, d), F32), jax.ShapeDtypeStruct((m, d), F32),
                   jax.ShapeDtypeStruct((m, LANES), F32), jax.ShapeDtypeStruct((m, LANES), I32),
                   jax.ShapeDtypeStruct((m, LANES), I32), jax.ShapeDtypeStruct((1, LANES), F32)],
        grid=(m // tm,),
        in_specs=[pl.BlockSpec((tm, d), row), pl.BlockSpec(wxo.shape, const), pl.BlockSpec((tm, d), row),
                  pl.BlockSpec((1, d), const), pl.BlockSpec(wr_hi.shape, const), pl.BlockSpec(wr_lo.shape, const),
                  pl.BlockSpec((1, LANES), const), pl.BlockSpec((1, LANES), const)],
        out_specs=[pl.BlockSpec((tm, d), row), pl.BlockSpec((tm, d), row),
                   pl.BlockSpec((tm, LANES), row), pl.BlockSpec((tm, LANES), row),
                   pl.BlockSpec((tm, LANES), row), pl.BlockSpec((1, LANES), const)],
        scratch_shapes=[pltpu.VMEM((1, LANES), F32)],
        compiler_params=_cparams("arbitrary"),
        name="post_xattn_router",
    )(ox, wxo, h, norm_ffn.reshape(1, -1), wr_hi, wr_lo, br_row, cnt_in)


MOE_TILE = 256


ROW_TILE = 128
MOE_FF_CHUNK = 512


def _row_dma_loop(n_tokens, make_copies, wait):
    def body(n, _):
        for cp in make_copies(n):
            cp.wait() if wait else cp.start()
        return 0
    lax.fori_loop(0, n_tokens, body, 0, unroll=4)


def _dispatch_kernel(dest_ref, x_ref, xs_in_ref, xs_ref, buf, sem, *, nt):
    del xs_in_ref
    t = pl.program_id(0)
    tm = ROW_TILE
    slot = t % 2

    def copies(s, dest_of):
        def make(n):
            return [pltpu.make_async_copy(buf.at[s, pl.ds(n, 1)], xs_ref.at[pl.ds(dest_of(n, k), 1)], sem.at[s])
                    for k in range(TOP_K)]
        return make

    def wait_slot(s):
        _row_dma_loop(tm, copies(s, lambda n, k: 0), wait=True)

    if nt > 2:
        @pl.when(t >= 2)
        def _():
            wait_slot(slot)

    buf[slot] = x_ref[...]
    _row_dma_loop(tm, copies(slot, lambda n, k: dest_ref[0, 0, n * TOP_K + k]), wait=False)

    @pl.when(t == nt - 1)
    def _():
        wait_slot(slot)
        if nt > 1:
            wait_slot(1 - slot)


def moe_dispatch(dest, x, xs):
    m, d = x.shape
    tm = ROW_TILE
    assert m % tm == 0
    n_tiles = m // tm
    return pl.pallas_call(
        functools.partial(_dispatch_kernel, nt=n_tiles),
        out_shape=jax.ShapeDtypeStruct(xs.shape, xs.dtype),
        grid=(n_tiles,),
        in_specs=[pl.BlockSpec((1, 1, TOP_K * tm), lambda t: (t, 0, 0), memory_space=pltpu.SMEM),
                  pl.BlockSpec((tm, d), lambda t: (t, 0)),
                  pl.BlockSpec(memory_space=pl.ANY)],
        out_specs=pl.BlockSpec(memory_space=pl.ANY),
        scratch_shapes=[pltpu.VMEM((2, tm, d), x.dtype), pltpu.SemaphoreType.DMA((2,))],
        input_output_aliases={2: 0},
        compiler_params=_cparams("arbitrary"),
        name="moe_dispatch",
    )(dest.reshape(n_tiles, 1, TOP_K * tm), x, xs)


def _moe_kernel(te_ref, tv_ref, x_ref, w1_ref, b1_ref, w2x_ref, b2_ref, y_ref, w1b):
    t = pl.program_id(0)

    @pl.when(jnp.logical_or(t == 0, te_ref[t] != te_ref[jnp.maximum(t - 1, 0)]))
    def _():
        w1b[...] = w1_ref[0].astype(BF16)

    @pl.when(tv_ref[t] > 0)
    def _():
        x = x_ref[...].astype(BF16)
        acc = jnp.broadcast_to(b2_ref[0], y_ref.shape)
        ff2 = w1b.shape[1]
        for c in range(ff2 // MOE_FF_CHUNK):
            lo = c * MOE_FF_CHUNK
            hh = jnp.dot(x, w1b[:, lo:lo + MOE_FF_CHUNK], preferred_element_type=F32) + b1_ref[0, :, lo:lo + MOE_FF_CHUNK]
            parts = []
            for j in range(MOE_FF_CHUNK // LANES):
                piece = hh[:, j * LANES:(j + 1) * LANES]
                nxt = pltpu.roll(piece, LANES - 1, 1)
                glu = jnp.minimum(piece, SWIGLU_LIMIT)
                lin = jnp.clip(nxt, -SWIGLU_LIMIT, SWIGLU_LIMIT)
                parts.append((lin + 1.0) * (glu * _sigmoid(SWIGLU_ALPHA * glu)))
            act = jnp.concatenate(parts, axis=1).astype(BF16)
            acc = acc + jnp.dot(act, w2x_ref[0, lo:lo + MOE_FF_CHUNK, :], preferred_element_type=F32)
        y_ref[...] = acc

    @pl.when(tv_ref[t] == 0)
    def _():
        y_ref[...] = jnp.zeros(y_ref.shape, F32)


def moe_experts(tile_expert, tile_valid, xs, w1, b1, w2x, b2):
    n_tiles = tile_expert.shape[0]
    tm = MOE_TILE
    d = xs.shape[1]
    ff2 = w1.shape[2]
    wmap = lambda t, te, tv: (te[t], 0, 0)
    grid_spec = pltpu.PrefetchScalarGridSpec(
        num_scalar_prefetch=2,
        grid=(n_tiles,),
        in_specs=[pl.BlockSpec((tm, d), lambda t, te, tv: (t, 0)),
                  pl.BlockSpec((1, d, ff2), wmap), pl.BlockSpec((1, 1, ff2), wmap),
                  pl.BlockSpec((1, ff2, d), wmap), pl.BlockSpec((1, 1, d), wmap)],
        out_specs=pl.BlockSpec((tm, d), lambda t, te, tv: (t, 0)),
        scratch_shapes=[pltpu.VMEM((d, ff2), BF16)],
    )
    return pl.pallas_call(
        _moe_kernel,
        out_shape=jax.ShapeDtypeStruct((n_tiles * tm, d), F32),
        grid_spec=grid_spec,
        compiler_params=pltpu.CompilerParams(dimension_semantics=("arbitrary",), vmem_limit_bytes=MOE_VMEM_LIMIT),
        name="moe_experts",
    )(tile_expert, tile_valid, xs, w1, b1, w2x, b2)


def _combine_kernel(dest_ref, destn_ref, y_hbm, h2_ref, gate_ref, gain_ref, o_ref, ybuf, sem, *, nt):
    t = pl.program_id(0)
    tm = ROW_TILE
    slot = t % 2

    def copies(s, dest_of):
        def make(n):
            return [pltpu.make_async_copy(y_hbm.at[pl.ds(dest_of(n, k), 1)], ybuf.at[s, pl.ds(k * tm + n, 1)], sem.at[s])
                    for k in range(TOP_K)]
        return make

    @pl.when(t == 0)
    def _():
        _row_dma_loop(tm, copies(0, lambda n, k: dest_ref[0, 0, n * TOP_K + k]), wait=False)

    if nt > 1:
        @pl.when(t + 1 < nt)
        def _():
            _row_dma_loop(tm, copies(1 - slot, lambda n, k: destn_ref[0, 0, n * TOP_K + k]), wait=False)

    _row_dma_loop(tm, copies(slot, lambda n, k: 0), wait=True)
    gates = gate_ref[...]
    acc = h2_ref[...]
    for k in range(TOP_K):
        acc = acc + gates[:, k:k + 1] * ybuf[slot, k * tm:(k + 1) * tm, :]
    o_ref[...] = _rms(acc, gain_ref[...])


def moe_combine(dest, y_sorted, h2, gates, norm_final):
    m, d = h2.shape
    tm = ROW_TILE
    assert m % tm == 0
    n_tiles = m // tm
    row = lambda t: (t, 0)
    dest3 = dest.reshape(n_tiles, 1, TOP_K * tm)
    return pl.pallas_call(
        functools.partial(_combine_kernel, nt=n_tiles),
        out_shape=jax.ShapeDtypeStruct((m, d), F32),
        grid=(n_tiles,),
        in_specs=[pl.BlockSpec((1, 1, TOP_K * tm), lambda t: (t, 0, 0), memory_space=pltpu.SMEM),
                  pl.BlockSpec((1, 1, TOP_K * tm), lambda t: (jnp.minimum(t + 1, n_tiles - 1), 0, 0),
                               memory_space=pltpu.SMEM),
                  pl.BlockSpec(memory_space=pl.ANY),
                  pl.BlockSpec((tm, d), row), pl.BlockSpec((tm, LANES), row),
                  pl.BlockSpec((1, d), lambda t: (0, 0))],
        out_specs=pl.BlockSpec((tm, d), row),
        scratch_shapes=[pltpu.VMEM((2, TOP_K * tm, d), F32), pltpu.SemaphoreType.DMA((2,))],
        compiler_params=_cparams("arbitrary"),
        name="moe_combine",
    )(dest3, dest3, y_sorted, h2, gates, norm_final.reshape(1, -1))


def _moe_plan(counts_row, n_assign):
    tm = MOE_TILE
    n_tiles = (n_assign + N_EXPERTS * (tm - 1) + tm - 1) // tm
    counts = counts_row[0, :N_EXPERTS].astype(I32)
    tiles_per = (counts + tm - 1) // tm
    tile_end = jnp.cumsum(tiles_per)
    pad_start = (tile_end - tiles_per) * tm
    tile_ids = jnp.arange(n_tiles, dtype=I32)
    used = tile_end[-1]
    tile_valid = (tile_ids < used).astype(I32)
    clamped = jnp.minimum(tile_ids, used - 1)
    tile_expert = jnp.sum((clamped[:, None] >= tile_end[None, :]).astype(I32), axis=1)
    return n_tiles, jnp.minimum(tile_expert, N_EXPERTS - 1), tile_valid, pad_start


def _dest_rows(ids, ranks, pad_start):
    sel = ids[:, :TOP_K, None] == jnp.arange(N_EXPERTS, dtype=I32)
    return ranks[:, :TOP_K] + jnp.sum(jnp.where(sel, pad_start, 0), axis=-1)


def _rope_tables(positions):
    half = MLA_ROPE // 2
    inv_freq = 1.0 / (ROPE_THETA ** (jnp.arange(half, dtype=F32) / half))
    ang = positions.astype(F32)[:, None] * inv_freq[None, :]
    cos, sin = jnp.cos(ang), jnp.sin(ang)
    zeros = jnp.zeros((positions.shape[0], LANES - MLA_ROPE), F32)
    return (jnp.concatenate([cos, cos, zeros], axis=1), jnp.concatenate([-sin, sin, zeros], axis=1))


def _in_proj_weight(w_in):
    o = np.cumsum([0, CONV_CH, GDN_HEADS * GDN_D, GDN_HEADS, GDN_HEADS, MLA_Q_RANK, MLA_KV_RANK, MLA_ROPE])
    conv, z, bl, al, cq, ckv, kpe = (w_in[:, o[i]:o[i + 1]] for i in range(7))
    pad = jnp.zeros((w_in.shape[0], LANES - MLA_ROPE - 2 * GDN_HEADS), w_in.dtype)
    return jnp.concatenate([conv, z, cq, ckv, kpe, bl, al, pad], axis=1).astype(BF16)


IN_SPLITS = (CONV_CH, GDN_HEADS * GDN_D, MLA_Q_RANK, MLA_KV_RANK, LANES)


def _lane_row(vals, lane0):
    return jnp.zeros((1, LANES), F32).at[0, lane0:lane0 + vals.shape[0]].set(vals.astype(F32))


def kernel(x_prompt, x_sample, state_gdn, state_conv, cache_latent, cache_krope, page_table, cache_mem_k, cache_mem_v, mem_prompt, norm_mix, w_in, conv_w, a_log, dt_bias, gdn_norm, q_norm, w_uq, kv_norm, w_uk, w_uv, w_out, norm_x, mem_norm, w_xq, w_xk, w_xv, w_xo, norm_ffn, w_router, b_router, w_e1, b_e1, w_e2, b_e2, norm_final):
    depth = w_in.shape[0]
    assert depth == 1
    L = 0
    bp, t, d = x_prompt.shape
    bs = x_sample.shape[0]
    assert x_sample.shape[1] == 1
    past_len = page_table.shape[1] * cache_latent.shape[2]
    mp = bp * t

    w_in_p = _in_proj_weight(w_in[L])
    alog_row = _lane_row(a_log[L], DECAY_LANE)
    dtb_row = _lane_row(dt_bias[L], DECAY_LANE)
    wuq = w_uq[L]
    wuq_p = jnp.concatenate(
        [wuq, jnp.zeros(wuq.shape[:2] + (MLA_QK_PAD - wuq.shape[2],), wuq.dtype)], axis=2
    ).reshape(wuq.shape[0], MLA_HEADS * MLA_QK_PAD).astype(BF16)
    wuk = w_uk[L].reshape(MLA_KV_RANK, MLA_HEADS * MLA_NOPE).astype(BF16)
    wuv = w_uv[L].reshape(MLA_KV_RANK, MLA_HEADS * MLA_V).astype(BF16)
    wuk_t = jnp.transpose(w_uk[L], (1, 2, 0)).astype(BF16)
    w_out_b = w_out[L].astype(BF16)
    n_gdn = GDN_HEADS * GDN_D
    w_xq_b, w_xo_b = w_xq[L].astype(BF16), w_xo[L].astype(BF16)
    w_xkv_b = jnp.concatenate([w_xk[L], w_xv[L]], axis=1).astype(BF16)
    wr = jnp.concatenate([w_router[L], jnp.zeros((d, LANES - N_EXPERTS), F32)], axis=1)
    wr_hi = wr.astype(BF16)
    wr_lo = (wr - wr_hi.astype(F32)).astype(BF16)
    br_row = _lane_row(b_router[L], 0)
    w2_b = w_e2[L].astype(BF16)
    w2x = jnp.stack([w2_b, jnp.zeros_like(w2_b)], axis=2).reshape(N_EXPERTS, -1, d)
    b1 = b_e1[L][:, None, :]
    b2 = b_e2[L][:, None, :]
    cos_p, sin_p = _rope_tables(jnp.arange(t, dtype=I32))
    cos_s, sin_s = _rope_tables(jnp.full((bs,), past_len, I32))

    xp = x_prompt.reshape(mp, d)
    xs = x_sample.reshape(bs, d)
    conv_p, z_p, cq_p, ckv_p, kba_p = fused_linear([xp], [w_in_p], gain=norm_mix[L], splits=IN_SPLITS, name="in_proj_prompt")
    conv_s, z_s, cq_s, ckv_s, kba_s = fused_linear([xs], [w_in_p], gain=norm_mix[L], splits=IN_SPLITS, name="in_proj_sample")

    y_gdn_p, gdn_state_p = gdn_prompt(conv_p.reshape(bp, t, CONV_CH), z_p.reshape(bp, t, n_gdn),
                                      kba_p.reshape(bp, t, LANES), conv_w[L], alog_row, dtb_row, gdn_norm[L])
    conv_state_p = conv_p.reshape(bp, t, CONV_CH)[:, t - (CONV_WIDTH - 1):, :]
    gdn_state_s, conv_state_s, y_gdn_s = gdn_sample(conv_s, state_conv[L], kba_s, z_s, state_gdn[L],
                                                    conv_w[L], alog_row, dtb_row, gdn_norm[L])

    q_p, k_p, v_p, lat_p, kpe_p = mla_prep(cq_p, ckv_p, kba_p, cos_p, sin_p, q_norm[L], kv_norm[L],
                                           wuq_p, wuk, wuv, seq=t, q_dtype=BF16)
    y_mla_p = mla_flash(q_p.reshape(bp, t, -1), k_p.reshape(bp, t, -1), v_p.reshape(bp, t, -1))
    q_s, _, _, lat_s, kpe_s = mla_prep(cq_s, ckv_s, kba_s, cos_s, sin_s, q_norm[L], kv_norm[L],
                                       wuq_p, wuk, wuv, seq=bs, q_dtype=F32)
    q_s4 = q_s.reshape(bs, MLA_HEADS, MLA_QK_PAD)
    qlat = jnp.concatenate(
        [fused_linear([q_s4[:, h, :MLA_NOPE]], [wuk_t[h]], name=f"absorb_q{h}")[0][:, None, :]
         for h in range(MLA_HEADS)], axis=1)
    head_pad = ((0, 0), (0, DEC_HEAD_PAD - MLA_HEADS), (0, 0))
    qlat8 = jnp.pad(qlat, head_pad)
    qpe8 = jnp.pad(q_s4[:, :, MLA_NOPE:MLA_NOPE + MLA_ROPE], head_pad)
    y_mla_s = mla_decode(page_table, qlat8, qpe8, lat_s, kpe_s, wuv,
                         cache_latent, jnp.swapaxes(cache_krope, 2, 3)).reshape(bs, -1)

    (h_p,) = fused_linear([y_gdn_p.reshape(mp, n_gdn), y_mla_p.reshape(mp, -1)], [w_out_b[:n_gdn], w_out_b[n_gdn:]],
                          residual=xp, name="out_proj_prompt")
    (h_s,) = fused_linear([y_gdn_s, y_mla_s], [w_out_b[:n_gdn], w_out_b[n_gdn:]], residual=xs, name="out_proj_sample")

    mem_tokens = mem_prompt.shape[1]
    mk_f, mv_f, mk_b, mv_b = fused_linear([mem_prompt.reshape(bp * mem_tokens, d)], [w_xkv_b], gain=mem_norm[L],
                                          splits=(d, d), bf16_copies=True, name="memory_kv")
    (qx_p,) = fused_linear([h_p], [w_xq_b], gain=norm_x[L], out_dtypes=(BF16,), name="xq_prompt")
    (qx_s,) = fused_linear([h_s], [w_xq_b], gain=norm_x[L], out_dtypes=(BF16,), name="xq_sample")
    ox_p = xattn_prompt(qx_p.reshape(bp, t, d), mk_b.reshape(bp, mem_tokens, d), mv_b.reshape(bp, mem_tokens, d))
    ox_s = xattn_sample(qx_s.reshape(bs, 1, d), cache_mem_k[L], cache_mem_v[L])

    h2_p, hn_p, gate_p, ids_p, rank_p, cnt_p = post_xattn(ox_p.reshape(mp, d), w_xo_b, h_p, norm_ffn[L],
                                                          wr_hi, wr_lo, br_row, jnp.zeros((1, LANES), F32))
    h2_s, hn_s, gate_s, ids_s, rank_s, cnt_all = post_xattn(ox_s.reshape(bs, d), w_xo_b, h_s, norm_ffn[L],
                                                            wr_hi, wr_lo, br_row, cnt_p)

    n_tiles, tile_expert, tile_valid, pad_start = _moe_plan(cnt_all, TOP_K * (mp + bs))
    dest_p = _dest_rows(ids_p, rank_p, pad_start)
    dest_s = _dest_rows(ids_s, rank_s, pad_start)
    x_grouped = jnp.zeros((n_tiles * MOE_TILE, d), F32)
    x_grouped = moe_dispatch(dest_p, hn_p, x_grouped)
    x_grouped = moe_dispatch(dest_s, hn_s, x_grouped)
    y_sorted = moe_experts(tile_expert, tile_valid, x_grouped, w_e1[L], b1, w2x, b2)
    y_prompt = moe_combine(dest_p, y_sorted, h2_p, gate_p, norm_final)
    y_sample = moe_combine(dest_s, y_sorted, h2_s, gate_s, norm_final)

    x_heads = X_HEADS
    return (y_prompt.reshape(bp, t, d), y_sample.reshape(bs, 1, d),
            gdn_state_p[None], conv_state_p[None],
            lat_p.reshape(bp, 1, t, MLA_KV_RANK), kpe_p.reshape(bp, 1, t, MLA_ROPE),
            mk_f.reshape(1, bp, mem_tokens, x_heads, d // x_heads), mv_f.reshape(1, bp, mem_tokens, x_heads, d // x_heads),
            gdn_state_s[None], conv_state_s[None],
            lat_s.reshape(bs, 1, 1, MLA_KV_RANK), kpe_s.reshape(bs, 1, 1, MLA_ROPE))
```

```python
import functools

import jax
import jax.numpy as jnp
import numpy as np
from jax import lax
from jax.experimental import pallas as pl
from jax.experimental.pallas import tpu as pltpu

F32 = jnp.float32
BF16 = jnp.bfloat16
I32 = jnp.int32

NORM_EPS = 1e-6
LANES = 128
SUBLANES = 8
VMEM_LIMIT = 48 * 1024 * 1024
MOE_VMEM_LIMIT = 56 * 1024 * 1024

GDN_HEADS = 4
GDN_D = 128
CONV_WIDTH = 4
CONV_CH = 3 * GDN_HEADS * GDN_D
GDN_BLOCK = 128
MLA_HEADS = 4
MLA_NOPE = 128
MLA_ROPE = 64
MLA_V = 128
MLA_Q_RANK = 384
MLA_KV_RANK = 256
MLA_QK_PAD = 256
ROPE_THETA = 10000.0
MLA_SCALE = (MLA_NOPE + MLA_ROPE) ** -0.5
X_HEADS = 4
N_EXPERTS = 32
TOP_K = 4
SWIGLU_LIMIT = 7.0
SWIGLU_ALPHA = 1.702
BETA_LANE = MLA_ROPE
DECAY_LANE = MLA_ROPE + GDN_HEADS
NEG_BIG = -1e30


def _cparams(*sem):
    return pltpu.CompilerParams(dimension_semantics=sem, vmem_limit_bytes=VMEM_LIMIT)


def _rms(x, gain):
    return x * lax.rsqrt(jnp.mean(x * x, axis=-1, keepdims=True) + NORM_EPS) * gain


def _mm(a, b):
    return jnp.dot(a.astype(BF16), b.astype(BF16), preferred_element_type=F32)


def _mm_nt(a, b):
    return lax.dot_general(a.astype(BF16), b.astype(BF16), (((1,), (1,)), ((), ())),
                           preferred_element_type=F32)


def _mm3(a, b):
    a_hi = a.astype(BF16)
    b_hi = b.astype(BF16)
    a_lo = (a - a_hi.astype(F32)).astype(BF16)
    b_lo = (b - b_hi.astype(F32)).astype(BF16)
    return jnp.dot(jnp.concatenate([a_hi, a_lo, a_hi], axis=1), jnp.concatenate([b_hi, b_hi, b_lo], axis=0),
                   preferred_element_type=F32)


def _sigmoid(x):
    return 1.0 / (1.0 + jnp.exp(-x))


def _softplus(x):
    return jnp.maximum(x, 0.0) + jnp.log1p(jnp.exp(-jnp.abs(x)))


def _linear_kernel(*refs, n_in, has_gain, has_res, splits):
    a_refs = refs[:n_in]
    w_refs = refs[n_in:2 * n_in]
    pos = 2 * n_in
    g_ref = refs[pos] if has_gain else None
    pos += int(has_gain)
    r_ref = refs[pos] if has_res else None
    pos += int(has_res)
    out_refs = refs[pos:]
    a0 = a_refs[0][...]
    if has_gain:
        a0 = _rms(a0.astype(F32), g_ref[...])
    acts = [a0.astype(BF16)] + [a[...].astype(BF16) for a in a_refs[1:]]
    off = 0
    for i, width in enumerate(splits):
        acc = None
        for a, w in zip(acts, w_refs):
            d = jnp.dot(a, w[:, off:off + width], preferred_element_type=F32)
            acc = d if acc is None else acc + d
        if has_res:
            acc = acc + r_ref[:, off:off + width]
        for o_ref in out_refs[i::len(splits)]:
            o_ref[...] = acc.astype(o_ref.dtype)
        off += width


def fused_linear(acts, weights, *, gain=None, residual=None, splits=None, out_dtypes=None,
                 bf16_copies=False, tm=256, name="fused_linear"):
    m = acts[0].shape[0]
    n = weights[0].shape[1]
    tm = min(tm, m)
    assert m % tm == 0
    splits = tuple(splits) if splits is not None else (n,)
    assert sum(splits) == n and all(s % LANES == 0 for s in splits)
    out_dtypes = tuple(out_dtypes) if out_dtypes is not None else (F32,) * len(splits)
    out_widths = splits
    if bf16_copies:
        out_widths = splits + splits
        out_dtypes = out_dtypes + (BF16,) * len(splits)
    in_specs = [pl.BlockSpec((tm, a.shape[1]), lambda i: (i, 0)) for a in acts]
    in_specs += [pl.BlockSpec(w.shape, lambda i: (0, 0)) for w in weights]
    args = list(acts) + list(weights)
    if gain is not None:
        in_specs.append(pl.BlockSpec((1, gain.shape[-1]), lambda i: (0, 0)))
        args.append(gain.reshape(1, -1))
    if residual is not None:
        in_specs.append(pl.BlockSpec((tm, n), lambda i: (i, 0)))
        args.append(residual)
    outs = pl.pallas_call(
        functools.partial(_linear_kernel, n_in=len(acts), has_gain=gain is not None,
                          has_res=residual is not None, splits=splits),
        out_shape=[jax.ShapeDtypeStruct((m, s), dt) for s, dt in zip(out_widths, out_dtypes)],
        grid=(m // tm,),
        in_specs=in_specs,
        out_specs=[pl.BlockSpec((tm, s), lambda i: (i, 0)) for s in out_widths],
        compiler_params=_cparams("parallel"),
        name=name,
    )(*args)
    return outs


def _gate_values(kba, alog_row, dtb_row):
    beta = _sigmoid(kba)
    g = -jnp.exp(alog_row) * _softplus(kba + dtb_row)
    return beta, g


def _l2norm(x):
    return x * lax.rsqrt(jnp.sum(x * x, axis=-1, keepdims=True) + NORM_EPS)


GDN_PROMPT_BB = 2


def _gdn_prompt_kernel(x_ref, z_ref, kba_ref, cw_ref, alog_ref, dtb_ref, gn_ref,
                       y_ref, s_out_ref, xbuf, state):
    t = pl.program_id(1)
    nt = pl.num_programs(1)

    @pl.when(t == 0)
    def _():
        xbuf[:, 0:SUBLANES, :] = jnp.zeros((GDN_PROMPT_BB, SUBLANES, CONV_CH), F32)
        state[...] = jnp.zeros(state.shape, F32)

    for bi in range(GDN_PROMPT_BB):
        _gdn_prompt_block(x_ref.at[bi], z_ref.at[bi], kba_ref.at[bi], cw_ref, alog_ref, dtb_ref, gn_ref,
                          y_ref.at[bi], xbuf.at[bi], state.at[bi])

    @pl.when(t == nt - 1)
    def _():
        s_out_ref[...] = state[...]


def _gdn_prompt_block(x_ref, z_ref, kba_ref, cw_ref, alog_ref, dtb_ref, gn_ref, y_ref, xbuf, state):
    blk = GDN_BLOCK
    hist = CONV_WIDTH - 1
    xbuf[SUBLANES:SUBLANES + blk, :] = x_ref[...]
    cw = cw_ref[...]
    conv = xbuf[SUBLANES - hist:SUBLANES - hist + blk, :] * cw[0:1, :]
    for j in range(1, CONV_WIDTH):
        conv = conv + xbuf[SUBLANES - hist + j:SUBLANES - hist + j + blk, :] * cw[j:j + 1, :]
    xbuf[SUBLANES - hist:SUBLANES, :] = xbuf[SUBLANES + blk - hist:SUBLANES + blk, :]
    c = conv * _sigmoid(conv)

    kba = kba_ref[...]
    beta_all, g_all = _gate_values(kba, alog_ref[...], dtb_ref[...])
    row = lax.broadcasted_iota(I32, (blk, blk), 0)
    col = lax.broadcasted_iota(I32, (blk, blk), 1)
    gc = g_all
    shift = 1
    while shift < blk:
        rolled = pltpu.roll(gc, shift, 0)
        gc = gc + jnp.where(row >= shift, rolled, 0.0)
        shift *= 2
    gc_t = gc.T
    incl = row >= col
    strict = row > col
    eye = (row == col).astype(F32)
    z = z_ref[...]
    gn = gn_ref[...]
    nh = GDN_HEADS * GDN_D

    for h in range(GDN_HEADS):
        q = _l2norm(c[:, h * GDN_D:(h + 1) * GDN_D]) * (GDN_D ** -0.5)
        k = _l2norm(c[:, nh + h * GDN_D:nh + (h + 1) * GDN_D])
        v = c[:, 2 * nh + h * GDN_D:2 * nh + (h + 1) * GDN_D]
        bcol = beta_all[:, BETA_LANE + h:BETA_LANE + h + 1]
        gcol = gc[:, DECAY_LANE + h:DECAY_LANE + h + 1]
        grow = gc_t[DECAY_LANE + h:DECAY_LANE + h + 1, :]
        decay = jnp.exp(jnp.where(incl, gcol - grow, NEG_BIG))
        kb = k * bcol
        vb = v * bcol
        a = jnp.where(strict, _mm_nt(kb, k) * decay, 0.0)
        x = eye - a
        p = _mm3(a, a)
        x = x + _mm3(x, p)
        for _ in range(5):
            p = _mm3(p, p)
            x = x + _mm3(x, p)
        egc = jnp.exp(gcol)
        u = _mm(x, vb)
        w = _mm(x, kb * egc)
        intra = _mm_nt(q, k) * decay
        s_h = state[h]
        v_new = u - _mm(w, s_h)
        o = _mm(q * egc, s_h) + _mm(intra, v_new)
        g_last = gcol[blk - 1:blk, :]
        kd = k * jnp.exp(g_last - gcol)
        state[h] = s_h * jnp.exp(g_last) + _mm(kd.T, v_new)
        zz = z[:, h * GDN_D:(h + 1) * GDN_D]
        y_ref[:, h * GDN_D:(h + 1) * GDN_D] = (_rms(o, gn) * (zz * _sigmoid(zz))).astype(y_ref.dtype)


def gdn_prompt(conv_in, z, kba, conv_w, alog_row, dtb_row, gdn_norm):
    b, t, _ = conv_in.shape
    bb = GDN_PROMPT_BB
    assert t % GDN_BLOCK == 0 and b % bb == 0
    nt = t // GDN_BLOCK
    y, s = pl.pallas_call(
        _gdn_prompt_kernel,
        out_shape=[jax.ShapeDtypeStruct((b, t, GDN_HEADS * GDN_D), BF16),
                   jax.ShapeDtypeStruct((b, GDN_HEADS, GDN_D, GDN_D), F32)],
        grid=(b // bb, nt),
        in_specs=[pl.BlockSpec((bb, GDN_BLOCK, CONV_CH), lambda i, j: (i, j, 0)),
                  pl.BlockSpec((bb, GDN_BLOCK, GDN_HEADS * GDN_D), lambda i, j: (i, j, 0)),
                  pl.BlockSpec((bb, GDN_BLOCK, LANES), lambda i, j: (i, j, 0)),
                  pl.BlockSpec((CONV_WIDTH, CONV_CH), lambda i, j: (0, 0)),
                  pl.BlockSpec((1, LANES), lambda i, j: (0, 0)),
                  pl.BlockSpec((1, LANES), lambda i, j: (0, 0)),
                  pl.BlockSpec((1, GDN_D), lambda i, j: (0, 0))],
        out_specs=[pl.BlockSpec((bb, GDN_BLOCK, GDN_HEADS * GDN_D), lambda i, j: (i, j, 0)),
                   pl.BlockSpec((bb, GDN_HEADS, GDN_D, GDN_D), lambda i, j: (i, 0, 0, 0))],
        scratch_shapes=[pltpu.VMEM((bb, SUBLANES + GDN_BLOCK, CONV_CH), F32),
                        pltpu.VMEM((bb, GDN_HEADS, GDN_D, GDN_D), F32)],
        compiler_params=_cparams("parallel", "arbitrary"),
        name="gdn_prompt",
    )(conv_in, z, kba, conv_w, alog_row, dtb_row, gdn_norm.reshape(1, -1))
    return y, s


GDN_SAMPLE_BB = 8


def _gdn_sample_kernel(x_ref, sc_ref, kba_ref, z_ref, s_ref, cw_ref, alog_ref, dtb_ref, gn_ref,
                       s_out_ref, sc_out_ref, y_ref, tbuf):
    bb = GDN_SAMPLE_BB
    x = x_ref[...]
    cw = cw_ref[...]
    conv = x * cw[CONV_WIDTH - 1:CONV_WIDTH, :]
    for j in range(CONV_WIDTH - 1):
        conv = conv + sc_ref[:, j, :] * cw[j:j + 1, :]
    for j in range(CONV_WIDTH - 2):
        sc_out_ref[:, j, :] = sc_ref[:, j + 1, :]
    sc_out_ref[:, CONV_WIDTH - 2, :] = x
    c = conv * _sigmoid(conv)
    beta_all, g_all = _gate_values(kba_ref[...], alog_ref[...], dtb_ref[...])
    eg_all = jnp.exp(g_all)
    z = z_ref[...]
    gn = gn_ref[...]
    nh = GDN_HEADS * GDN_D
    tbuf[...] = jnp.zeros(tbuf.shape, F32)
    for h in range(GDN_HEADS):
        q = _l2norm(c[:, h * GDN_D:(h + 1) * GDN_D]) * (GDN_D ** -0.5)
        k = _l2norm(c[:, nh + h * GDN_D:nh + (h + 1) * GDN_D])
        v = c[:, 2 * nh + h * GDN_D:2 * nh + (h + 1) * GDN_D]
        tbuf[0:bb, :] = q
        q_t = tbuf[...].T
        tbuf[0:bb, :] = k
        k_t = tbuf[...].T
        for b in range(bb):
            qcol = q_t[:, b:b + 1]
            kcol = k_t[:, b:b + 1]
            eg = eg_all[b:b + 1, DECAY_LANE + h:DECAY_LANE + h + 1]
            beta = beta_all[b:b + 1, BETA_LANE + h:BETA_LANE + h + 1]
            s1 = s_ref[b, h] * eg
            pred = jnp.sum(s1 * kcol, axis=0, keepdims=True)
            u = (v[b:b + 1, :] - pred) * beta
            s2 = s1 + kcol * u
            s_out_ref[b, h] = s2
            o = jnp.sum(s2 * qcol, axis=0, keepdims=True)
            zz = z[b:b + 1, h * GDN_D:(h + 1) * GDN_D]
            y_ref[b:b + 1, h * GDN_D:(h + 1) * GDN_D] = (_rms(o, gn) * (zz * _sigmoid(zz))).astype(y_ref.dtype)


def gdn_sample(conv_in, state_conv, kba, z, state_gdn, conv_w, alog_row, dtb_row, gdn_norm):
    b = conv_in.shape[0]
    bb = GDN_SAMPLE_BB
    assert b % bb == 0
    hist = CONV_WIDTH - 1
    return pl.pallas_call(
        _gdn_sample_kernel,
        out_shape=[jax.ShapeDtypeStruct(state_gdn.shape, F32),
                   jax.ShapeDtypeStruct(state_conv.shape, F32),
                   jax.ShapeDtypeStruct((b, GDN_HEADS * GDN_D), BF16)],
        grid=(b // bb,),
        in_specs=[pl.BlockSpec((bb, CONV_CH), lambda i: (i, 0)),
                  pl.BlockSpec((bb, hist, CONV_CH), lambda i: (i, 0, 0)),
                  pl.BlockSpec((bb, LANES), lambda i: (i, 0)),
                  pl.BlockSpec((bb, GDN_HEADS * GDN_D), lambda i: (i, 0)),
                  pl.BlockSpec((bb, GDN_HEADS, GDN_D, GDN_D), lambda i: (i, 0, 0, 0)),
                  pl.BlockSpec((CONV_WIDTH, CONV_CH), lambda i: (0, 0)),
                  pl.BlockSpec((1, LANES), lambda i: (0, 0)),
                  pl.BlockSpec((1, LANES), lambda i: (0, 0)),
                  pl.BlockSpec((1, GDN_D), lambda i: (0, 0))],
        out_specs=[pl.BlockSpec((bb, GDN_HEADS, GDN_D, GDN_D), lambda i: (i, 0, 0, 0)),
                   pl.BlockSpec((bb, hist, CONV_CH), lambda i: (i, 0, 0)),
                   pl.BlockSpec((bb, GDN_HEADS * GDN_D), lambda i: (i, 0))],
        scratch_shapes=[pltpu.VMEM((GDN_D, GDN_D), F32)],
        compiler_params=_cparams("parallel"),
        name="gdn_sample",
    )(conv_in, state_conv, kba, z, state_gdn, conv_w, alog_row, dtb_row, gdn_norm.reshape(1, -1))


def _rope128(x, cos, sin):
    half = MLA_ROPE // 2
    lane = lax.broadcasted_iota(I32, x.shape, 1)
    swapped = jnp.where(lane < half, pltpu.roll(x, LANES - half, 1), pltpu.roll(x, half, 1))
    return x * cos + swapped * sin


def _mla_prep_kernel(cq_ref, ckv_ref, kba_ref, cos_ref, sin_ref, qn_ref, kvn_ref, wuq_ref, wuk_ref, wuv_ref,
                     q_ref, k_ref, v_ref, lat_ref, kpe_ref):
    cos = cos_ref[...]
    sin = sin_ref[...]
    qn = _rms(cq_ref[...], qn_ref[...]).astype(BF16)
    lat = _rms(ckv_ref[...], kvn_ref[...])
    lat_ref[...] = lat
    lat_b = lat.astype(BF16)
    kpe = _rope128(kba_ref[...], cos, sin)
    kpe_ref[...] = kpe[:, :MLA_ROPE]
    for h in range(MLA_HEADS):
        lo = h * MLA_QK_PAD
        q_ref[:, lo:lo + MLA_NOPE] = jnp.dot(
            qn, wuq_ref[:, lo:lo + MLA_NOPE], preferred_element_type=F32).astype(q_ref.dtype)
        q_pe = jnp.dot(qn, wuq_ref[:, lo + MLA_NOPE:lo + MLA_QK_PAD], preferred_element_type=F32)
        q_ref[:, lo + MLA_NOPE:lo + MLA_QK_PAD] = _rope128(q_pe, cos, sin).astype(q_ref.dtype)
        k_ref[:, lo:lo + MLA_NOPE] = jnp.dot(
            lat_b, wuk_ref[:, h * MLA_NOPE:(h + 1) * MLA_NOPE], preferred_element_type=F32).astype(k_ref.dtype)
        k_ref[:, lo + MLA_NOPE:lo + MLA_QK_PAD] = kpe.astype(k_ref.dtype)
    v_ref[...] = jnp.dot(lat_b, wuv_ref[...], preferred_element_type=F32).astype(v_ref.dtype)


def mla_prep(c_q, c_kv, kba, cos_tab, sin_tab, q_norm, kv_norm, wuq_p, wuk, wuv, *, seq, q_dtype, tm=256):
    m = c_q.shape[0]
    tm = min(tm, m, seq)
    assert m % tm == 0 and seq % tm == 0
    nseq = seq // tm
    hq = MLA_HEADS * MLA_QK_PAD
    row = lambda i: (i, 0)
    const = lambda i: (0, 0)
    return pl.pallas_call(
        _mla_prep_kernel,
        out_shape=[jax.ShapeDtypeStruct((m, hq), q_dtype),
                   jax.ShapeDtypeStruct((m, hq), BF16),
                   jax.ShapeDtypeStruct((m, MLA_HEADS * MLA_V), BF16),
                   jax.ShapeDtypeStruct((m, MLA_KV_RANK), F32),
                   jax.ShapeDtypeStruct((m, MLA_ROPE), F32)],
        grid=(m // tm,),
        in_specs=[pl.BlockSpec((tm, MLA_Q_RANK), row),
                  pl.BlockSpec((tm, MLA_KV_RANK), row),
                  pl.BlockSpec((tm, LANES), row),
                  pl.BlockSpec((tm, LANES), lambda i: (i % nseq, 0)),
                  pl.BlockSpec((tm, LANES), lambda i: (i % nseq, 0)),
                  pl.BlockSpec((1, MLA_Q_RANK), const),
                  pl.BlockSpec((1, MLA_KV_RANK), const),
                  pl.BlockSpec(wuq_p.shape, const),
                  pl.BlockSpec(wuk.shape, const),
                  pl.BlockSpec(wuv.shape, const)],
        out_specs=[pl.BlockSpec((tm, hq), row),
                   pl.BlockSpec((tm, hq), row),
                   pl.BlockSpec((tm, MLA_HEADS * MLA_V), row),
                   pl.BlockSpec((tm, MLA_KV_RANK), row),
                   pl.BlockSpec((tm, MLA_ROPE), row)],
        compiler_params=_cparams("parallel"),
        name="mla_prep",
    )(c_q, c_kv, kba, cos_tab, sin_tab, q_norm.reshape(1, -1), kv_norm.reshape(1, -1), wuq_p, wuk, wuv)


def _flash_kernel(q_ref, k_ref, v_ref, o_ref, m_s, l_s, acc_s, *, tq, tk):
    qi = pl.program_id(2)
    ki = pl.program_id(3)

    @pl.when(ki == 0)
    def _():
        m_s[...] = jnp.full(m_s.shape, NEG_BIG, F32)
        l_s[...] = jnp.zeros(l_s.shape, F32)
        acc_s[...] = jnp.zeros(acc_s.shape, F32)

    @pl.when(ki * tk <= qi * tq + (tq - 1))
    def _():
        s = lax.dot_general(q_ref[0], k_ref[0], (((1,), (1,)), ((), ())), preferred_element_type=F32) * MLA_SCALE
        qpos = qi * tq + lax.broadcasted_iota(I32, (tq, tk), 0)
        kpos = ki * tk + lax.broadcasted_iota(I32, (tq, tk), 1)
        s = jnp.where(kpos <= qpos, s, NEG_BIG)
        m_prev = m_s[...]
        m_new = jnp.maximum(m_prev, jnp.max(s, axis=-1, keepdims=True))
        corr = jnp.exp(m_prev - m_new)
        p = jnp.exp(s - m_new)
        l_s[...] = l_s[...] * corr + jnp.sum(p, axis=-1, keepdims=True)
        acc_s[...] = acc_s[...] * corr + jnp.dot(p.astype(BF16), v_ref[0], preferred_element_type=F32)
        m_s[...] = m_new

    @pl.when(ki == pl.num_programs(3) - 1)
    def _():
        o_ref[0] = (acc_s[...] / l_s[...]).astype(o_ref.dtype)


def mla_flash(q, k, v, *, tq=512, tk=512):
    b, t, _ = q.shape
    tq = min(tq, t)
    tk = min(tk, t)
    assert t % tq == 0 and t % tk == 0
    last_k = lambda qi: (qi * tq + tq - 1) // tk

    def kv_map(bi, h, qi, ki):
        return (bi, jnp.minimum(ki, last_k(qi)), h)

    return pl.pallas_call(
        functools.partial(_flash_kernel, tq=tq, tk=tk),
        out_shape=jax.ShapeDtypeStruct((b, t, MLA_HEADS * MLA_V), BF16),
        grid=(b, MLA_HEADS, t // tq, t // tk),
        in_specs=[pl.BlockSpec((1, tq, MLA_QK_PAD), lambda bi, h, qi, ki: (bi, qi, h)),
                  pl.BlockSpec((1, tk, MLA_QK_PAD), kv_map),
                  pl.BlockSpec((1, tk, MLA_V), kv_map)],
        out_specs=pl.BlockSpec((1, tq, MLA_V), lambda bi, h, qi, ki: (bi, qi, h)),
        scratch_shapes=[pltpu.VMEM((tq, 1), F32), pltpu.VMEM((tq, 1), F32), pltpu.VMEM((tq, MLA_V), F32)],
        compiler_params=_cparams("parallel", "parallel", "parallel", "arbitrary"),
        name="mla_flash",
    )(q, k, v)


DEC_HEAD_PAD = 8
DEC_GROUP = 16
DEC_SLOTS = 4


def _decode_kernel(pt_ref, ptn_ref, qlat_ref, qpe_ref, latn_ref, kpen_ref, wuv_ref, lat_hbm, kpe_hbm,
                   o_ref, latbuf, kpebuf, sem, *, nb, n_pages, page):
    b = pl.program_id(0)
    grp = DEC_GROUP
    ns = DEC_SLOTS
    n_groups = n_pages // grp
    n_outer = n_groups // ns

    def page_copies(pg, slot, j):
        return (pltpu.make_async_copy(lat_hbm.at[pg, 0], latbuf.at[slot, j], sem.at[0, slot]),
                pltpu.make_async_copy(kpe_hbm.at[pg, 0], kpebuf.at[slot, j], sem.at[1, slot]))

    def start_group(tbl_ref, g, slot):
        for j in range(grp):
            for cp in page_copies(tbl_ref[0, 0, g * grp + j], slot, j):
                cp.start()

    def wait_group(slot):
        for j in range(grp):
            for cp in page_copies(0, slot, j):
                cp.wait()

    @pl.when(b == 0)
    def _():
        for s in range(ns - 1):
            start_group(pt_ref, s, s)

    qlat = qlat_ref[0]
    qpe = qpe_ref[0]
    latn = latn_ref[0]
    kpen = kpen_ref[0]
    qlat_b = qlat.astype(BF16)
    qpe_b = qpe.astype(BF16)
    s_new = (jnp.sum(qlat * latn, axis=-1, keepdims=True)
             + jnp.sum(qpe * kpen, axis=-1, keepdims=True)) * MLA_SCALE
    m0 = s_new
    l0 = jnp.ones_like(s_new)
    acc0 = jnp.broadcast_to(latn, qlat.shape)

    def consume(slot, carry):
        m, l, acc = carry
        lat = latbuf[slot].reshape(grp * page, MLA_KV_RANK).astype(BF16)
        kpe_t = jnp.concatenate([kpebuf[slot, j] for j in range(grp)], axis=1).astype(BF16)
        s = (lax.dot_general(qlat_b, lat, (((1,), (1,)), ((), ())), preferred_element_type=F32)
             + jnp.dot(qpe_b, kpe_t, preferred_element_type=F32)) * MLA_SCALE
        m_new = jnp.maximum(m, jnp.max(s, axis=-1, keepdims=True))
        corr = jnp.exp(m - m_new)
        p = jnp.exp(s - m_new)
        l = l * corr + jnp.sum(p, axis=-1, keepdims=True)
        acc = acc * corr + jnp.dot(p.astype(BF16), lat, preferred_element_type=F32)
        return m_new, l, acc

    def ring_body(i, carry):
        for j in range(ns):
            ahead = (j + ns - 1) % ns
            if j == 0:
                start_group(pt_ref, i * ns + ns - 1, ahead)
            else:
                @pl.when(i + 1 < n_outer)
                def _():
                    start_group(pt_ref, (i + 1) * ns + ahead, ahead)

                @pl.when(jnp.logical_and(i + 1 >= n_outer, b + 1 < nb))
                def _():
                    start_group(ptn_ref, ahead, ahead)
            wait_group(j)
            carry = consume(j, carry)
        return carry

    m, l, acc = lax.fori_loop(0, n_outer, ring_body, (m0, l0, acc0))
    o_lat = (acc / l).astype(BF16)
    res = jnp.dot(o_lat, wuv_ref[...], preferred_element_type=F32)
    o_ref[0] = jnp.concatenate(
        [res[h:h + 1, h * MLA_V:(h + 1) * MLA_V] for h in range(MLA_HEADS)], axis=1).astype(o_ref.dtype)


def mla_decode(page_table, qlat8, qpe8, lat_new, kpe_new, wuv, cache_latent, cache_krope_t):
    b, n_pages = page_table.shape
    page = cache_latent.shape[2]
    assert n_pages % (DEC_SLOTS * DEC_GROUP) == 0
    pt3 = page_table.reshape(b, 1, n_pages)
    smem_row = lambda f: pl.BlockSpec((1, 1, n_pages), f, memory_space=pltpu.SMEM)
    return pl.pallas_call(
        functools.partial(_decode_kernel, nb=b, n_pages=n_pages, page=page),
        out_shape=jax.ShapeDtypeStruct((b, 1, MLA_HEADS * MLA_V), BF16),
        grid=(b,),
        in_specs=[smem_row(lambda i: (i, 0, 0)),
                  smem_row(lambda i: (jnp.minimum(i + 1, b - 1), 0, 0)),
                  pl.BlockSpec((1, DEC_HEAD_PAD, MLA_KV_RANK), lambda i: (i, 0, 0)),
                  pl.BlockSpec((1, DEC_HEAD_PAD, MLA_ROPE), lambda i: (i, 0, 0)),
                  pl.BlockSpec((1, 1, MLA_KV_RANK), lambda i: (i, 0, 0)),
                  pl.BlockSpec((1, 1, MLA_ROPE), lambda i: (i, 0, 0)),
                  pl.BlockSpec(wuv.shape, lambda i: (0, 0)),
                  pl.BlockSpec(memory_space=pl.ANY),
                  pl.BlockSpec(memory_space=pl.ANY)],
        out_specs=pl.BlockSpec((1, 1, MLA_HEADS * MLA_V), lambda i: (i, 0, 0)),
        scratch_shapes=[pltpu.VMEM((DEC_SLOTS, DEC_GROUP, page, MLA_KV_RANK), F32),
                        pltpu.VMEM((DEC_SLOTS, DEC_GROUP, MLA_ROPE, page), F32),
                        pltpu.SemaphoreType.DMA((2, DEC_SLOTS))],
        compiler_params=_cparams("arbitrary"),
        name="mla_decode",
    )(pt3, pt3, qlat8, qpe8, lat_new.reshape(b, 1, -1), kpe_new.reshape(b, 1, -1), wuv,
      cache_latent, cache_krope_t)


def _softmax_rows(s):
    m = jnp.max(s, axis=-1, keepdims=True)
    p = jnp.exp(s - m)
    return p / jnp.sum(p, axis=-1, keepdims=True)


def _xattn_prompt_kernel(q_ref, k_ref, v_ref, o_ref, *, dh):
    scale = dh ** -0.5
    for h in range(X_HEADS):
        q = q_ref[0, :, h * dh:(h + 1) * dh]
        s = lax.dot_general(q, k_ref[0, :, h * dh:(h + 1) * dh], (((1,), (1,)), ((), ())),
                            preferred_element_type=F32) * scale
        p = _softmax_rows(s).astype(BF16)
        o_ref[0, :, h * dh:(h + 1) * dh] = jnp.dot(
            p, v_ref[0, :, h * dh:(h + 1) * dh], preferred_element_type=F32).astype(o_ref.dtype)


def xattn_prompt(q, k, v, *, tq=512):
    b, t, d = q.shape
    mem = k.shape[1]
    tq = min(tq, t)
    return pl.pallas_call(
        functools.partial(_xattn_prompt_kernel, dh=d // X_HEADS),
        out_shape=jax.ShapeDtypeStruct((b, t, d), BF16),
        grid=(b, t // tq),
        in_specs=[pl.BlockSpec((1, tq, d), lambda i, j: (i, j, 0)),
                  pl.BlockSpec((1, mem, d), lambda i, j: (i, 0, 0)),
                  pl.BlockSpec((1, mem, d), lambda i, j: (i, 0, 0))],
        out_specs=pl.BlockSpec((1, tq, d), lambda i, j: (i, j, 0)),
        compiler_params=_cparams("parallel", "parallel"),
        name="xattn_prompt",
    )(q, k, v)


XATTN_SAMPLE_BB = 2


def _xattn_sample_kernel(q_ref, k_ref, v_ref, o_ref, *, dh):
    scale = dh ** -0.5
    for b in range(XATTN_SAMPLE_BB):
        q = q_ref[b].astype(F32)
        s = jnp.sum(k_ref[b] * q[None], axis=-1, keepdims=True) * scale
        m = jnp.max(s, axis=0, keepdims=True)
        p = jnp.exp(s - m)
        denom = jnp.sum(p, axis=0, keepdims=True)
        o = jnp.sum(p * v_ref[b], axis=0) / denom[0]
        o_ref[b] = o.astype(o_ref.dtype)


def xattn_sample(q, k, v):
    b, heads, dh = q.shape
    mem = k.shape[1]
    bb = XATTN_SAMPLE_BB
    assert b % bb == 0
    kv_spec = pl.BlockSpec((bb, mem, heads, dh), lambda i: (i, 0, 0, 0))
    return pl.pallas_call(
        functools.partial(_xattn_sample_kernel, dh=dh),
        out_shape=jax.ShapeDtypeStruct((b, heads, dh), BF16),
        grid=(b // bb,),
        in_specs=[pl.BlockSpec((bb, heads, dh), lambda i: (i, 0, 0)), kv_spec, kv_spec],
        out_specs=pl.BlockSpec((bb, heads, dh), lambda i: (i, 0, 0)),
        compiler_params=_cparams("parallel"),
        name="xattn_sample",
    )(q, k, v)


def _post_xattn_kernel(ox_ref, wxo_ref, h_ref, g_ref, wr_hi_ref, wr_lo_ref, br_ref, cnt_in_ref,
                       h2_ref, hn_ref, gate_ref, idx_ref, rank_ref, cnt_ref, cnt_s):
    @pl.when(pl.program_id(0) == 0)
    def _():
        cnt_s[...] = cnt_in_ref[...]

    h2 = h_ref[...] + jnp.dot(ox_ref[...], wxo_ref[...], preferred_element_type=F32)
    h2_ref[...] = h2
    hn = _rms(h2, g_ref[...])
    hn_ref[...] = hn
    hn_hi = hn.astype(BF16)
    hn_lo = (hn - hn_hi.astype(F32)).astype(BF16)
    logits = (jnp.dot(hn_hi, wr_hi_ref[...], preferred_element_type=F32)
              + jnp.dot(hn_hi, wr_lo_ref[...], preferred_element_type=F32)
              + jnp.dot(hn_lo, wr_hi_ref[...], preferred_element_type=F32)) + br_ref[...]
    lane = lax.broadcasted_iota(I32, logits.shape, 1)
    logits = jnp.where(lane < N_EXPERTS, logits, NEG_BIG)
    vals, picks = [], []
    gates = jnp.zeros(logits.shape, F32)
    ids = jnp.zeros(logits.shape, I32)
    member = jnp.zeros(logits.shape, F32)
    for k in range(TOP_K):
        m = jnp.max(logits, axis=-1, keepdims=True)
        idx = jnp.min(jnp.where(logits == m, lane, LANES), axis=-1, keepdims=True)
        vals.append(m)
        picks.append(idx)
        ids = jnp.where(lane == k, idx, ids)
        member = jnp.where(lane == idx, 1.0, member)
        logits = jnp.where(lane == idx, NEG_BIG, logits)
    exps = [jnp.exp(v - vals[0]) for v in vals]
    denom = exps[0]
    for e in exps[1:]:
        denom = denom + e
    for k in range(TOP_K):
        gates = jnp.where(lane == k, exps[k] / denom, gates)
    gate_ref[...] = gates
    idx_ref[...] = ids
    tm = logits.shape[0]
    before = (lax.broadcasted_iota(I32, (tm, tm), 0) > lax.broadcasted_iota(I32, (tm, tm), 1)).astype(BF16)
    prior = jnp.dot(before, member.astype(BF16), preferred_element_type=F32) + cnt_s[...]
    ranks = jnp.zeros(logits.shape, I32)
    for k in range(TOP_K):
        rk = jnp.sum(jnp.where(lane == picks[k], prior, 0.0), axis=-1, keepdims=True)
        ranks = jnp.where(lane == k, rk.astype(I32), ranks)
    rank_ref[...] = ranks
    cnt = cnt_s[...] + jnp.sum(member, axis=0, keepdims=True)
    cnt_s[...] = cnt
    cnt_ref[...] = cnt


def post_xattn(ox, wxo, h, norm_ffn, wr_hi, wr_lo, br_row, cnt_in, *, tm=256):
    m, d = h.shape
    tm = min(tm, m)
    row = lambda i: (i, 0)
    const = lambda i: (0, 0)
    return pl.pallas_call(
        _post_xattn_kernel,
        out_shape=[jax.ShapeDtypeStruct((m, d), F32), jax.ShapeDtypeStruct((m, d), F32),
                   jax.ShapeDtypeStruct((m, LANES), F32), jax.ShapeDtypeStruct((m, LANES), I32),
                   jax.ShapeDtypeStruct((m, LANES), I32), jax.ShapeDtypeStruct((1, LANES), F32)],
        grid=(m // tm,),
        in_specs=[pl.BlockSpec((tm, d), row), pl.BlockSpec(wxo.shape, const), pl.BlockSpec((tm, d), row),
                  pl.BlockSpec((1, d), const), pl.BlockSpec(wr_hi.shape, const), pl.BlockSpec(wr_lo.shape, const),
                  pl.BlockSpec((1, LANES), const), pl.BlockSpec((1, LANES), const)],
        out_specs=[pl.BlockSpec((tm, d), row), pl.BlockSpec((tm, d), row),
                   pl.BlockSpec((tm, LANES), row), pl.BlockSpec((tm, LANES), row),
                   pl.BlockSpec((tm, LANES), row), pl.BlockSpec((1, LANES), const)],
        scratch_shapes=[pltpu.VMEM((1, LANES), F32)],
        compiler_params=_cparams("arbitrary"),
        name="post_xattn_router",
    )(ox, wxo, h, norm_ffn.reshape(1, -1), wr_hi, wr_lo, br_row, cnt_in)


MOE_TILE = 256


ROW_TILE = 128
MOE_FF_CHUNK = 512


def _row_dma_loop(n_tokens, make_copies, wait):
    def body(n, _):
        for k, cp in enumerate(make_copies(n)):
            cp.wait() if wait else cp.start(priority=k % 2)
        return 0
    lax.fori_loop(0, n_tokens, body, 0, unroll=4)


def _dispatch_kernel(dest_ref, x_ref, xs_in_ref, xs_ref, buf, sem, *, nt):
    del xs_in_ref
    t = pl.program_id(0)
    tm = ROW_TILE
    slot = t % 2

    def copies(s, dest_of):
        def make(n):
            return [pltpu.make_async_copy(buf.at[s, pl.ds(n, 1)], xs_ref.at[pl.ds(dest_of(n, k), 1)], sem.at[s])
                    for k in range(TOP_K)]
        return make

    def wait_slot(s):
        _row_dma_loop(tm, copies(s, lambda n, k: 0), wait=True)

    if nt > 2:
        @pl.when(t >= 2)
        def _():
            wait_slot(slot)

    buf[slot] = x_ref[...]
    _row_dma_loop(tm, copies(slot, lambda n, k: dest_ref[0, 0, n * TOP_K + k]), wait=False)

    @pl.when(t == nt - 1)
    def _():
        wait_slot(slot)
        if nt > 1:
            wait_slot(1 - slot)


def moe_dispatch(dest, x, xs):
    m, d = x.shape
    tm = ROW_TILE
    assert m % tm == 0
    n_tiles = m // tm
    return pl.pallas_call(
        functools.partial(_dispatch_kernel, nt=n_tiles),
        out_shape=jax.ShapeDtypeStruct(xs.shape, xs.dtype),
        grid=(n_tiles,),
        in_specs=[pl.BlockSpec((1, 1, TOP_K * tm), lambda t: (t, 0, 0), memory_space=pltpu.SMEM),
                  pl.BlockSpec((tm, d), lambda t: (t, 0)),
                  pl.BlockSpec(memory_space=pl.ANY)],
        out_specs=pl.BlockSpec(memory_space=pl.ANY),
        scratch_shapes=[pltpu.VMEM((2, tm, d), x.dtype), pltpu.SemaphoreType.DMA((2,))],
        input_output_aliases={2: 0},
        compiler_params=_cparams("arbitrary"),
        name="moe_dispatch",
    )(dest.reshape(n_tiles, 1, TOP_K * tm), x, xs)


def _moe_kernel(te_ref, tv_ref, x_ref, w1_ref, b1_ref, w2p_ref, b2_ref, y_ref, w1b):
    t = pl.program_id(0)

    @pl.when(jnp.logical_or(t == 0, te_ref[t] != te_ref[jnp.maximum(t - 1, 0)]))
    def _():
        w1b[...] = w1_ref[0].astype(BF16)

    @pl.when(tv_ref[t] > 0)
    def _():
        x = x_ref[...].astype(BF16)
        acc = jnp.broadcast_to(b2_ref[0], y_ref.shape)
        ff2 = w1b.shape[1]
        even = lax.broadcasted_iota(I32, (x.shape[0], LANES), 1) % 2 == 0
        for c in range(ff2 // MOE_FF_CHUNK):
            lo = c * MOE_FF_CHUNK
            hh = jnp.dot(x, w1b[:, lo:lo + MOE_FF_CHUNK], preferred_element_type=F32) + b1_ref[0, :, lo:lo + MOE_FF_CHUNK]
            parts = []
            for j in range(MOE_FF_CHUNK // (2 * LANES)):
                a = hh[:, (2 * j) * LANES:(2 * j + 1) * LANES]
                b = hh[:, (2 * j + 1) * LANES:(2 * j + 2) * LANES]
                glu = jnp.where(even, a, pltpu.roll(b, 1, 1))
                lin = jnp.where(even, pltpu.roll(a, LANES - 1, 1), b)
                glu = jnp.minimum(glu, SWIGLU_LIMIT)
                lin = jnp.clip(lin, -SWIGLU_LIMIT, SWIGLU_LIMIT)
                parts.append((lin + 1.0) * (glu * _sigmoid(SWIGLU_ALPHA * glu)))
            act = jnp.concatenate(parts, axis=1).astype(BF16)
            acc = acc + jnp.dot(act, w2p_ref[0, lo // 2:(lo + MOE_FF_CHUNK) // 2, :], preferred_element_type=F32)
        y_ref[...] = acc

    @pl.when(tv_ref[t] == 0)
    def _():
        y_ref[...] = jnp.zeros(y_ref.shape, F32)


def _pair_rows(w2):
    e, f, d = w2.shape
    half = LANES // 2
    return w2.reshape(e, f // LANES, 2, half, d).transpose(0, 1, 3, 2, 4).reshape(e, f, d)


def moe_experts(tile_expert, tile_valid, xs, w1, b1, w2p, b2):
    n_tiles = tile_expert.shape[0]
    tm = MOE_TILE
    d = xs.shape[1]
    ff2 = w1.shape[2]
    assert ff2 % MOE_FF_CHUNK == 0 and MOE_FF_CHUNK % (2 * LANES) == 0
    wmap = lambda t, te, tv: (te[t], 0, 0)
    grid_spec = pltpu.PrefetchScalarGridSpec(
        num_scalar_prefetch=2,
        grid=(n_tiles,),
        in_specs=[pl.BlockSpec((tm, d), lambda t, te, tv: (t, 0)),
                  pl.BlockSpec((1, d, ff2), wmap), pl.BlockSpec((1, 1, ff2), wmap),
                  pl.BlockSpec((1, ff2 // 2, d), wmap), pl.BlockSpec((1, 1, d), wmap)],
        out_specs=pl.BlockSpec((tm, d), lambda t, te, tv: (t, 0)),
        scratch_shapes=[pltpu.VMEM((d, ff2), BF16)],
    )
    return pl.pallas_call(
        _moe_kernel,
        out_shape=jax.ShapeDtypeStruct((n_tiles * tm, d), F32),
        grid_spec=grid_spec,
        compiler_params=pltpu.CompilerParams(dimension_semantics=("arbitrary",), vmem_limit_bytes=MOE_VMEM_LIMIT),
        name="moe_experts",
    )(tile_expert, tile_valid, xs, w1, b1, w2p, b2)


def _combine_kernel(dest_ref, destn_ref, y_hbm, h2_ref, gate_ref, gain_ref, o_ref, ybuf, sem, *, nt):
    t = pl.program_id(0)
    tm = ROW_TILE
    slot = t % 2

    def copies(s, dest_of):
        def make(n):
            return [pltpu.make_async_copy(y_hbm.at[pl.ds(dest_of(n, k), 1)], ybuf.at[s, pl.ds(k * tm + n, 1)], sem.at[s])
                    for k in range(TOP_K)]
        return make

    @pl.when(t == 0)
    def _():
        _row_dma_loop(tm, copies(0, lambda n, k: dest_ref[0, 0, n * TOP_K + k]), wait=False)

    if nt > 1:
        @pl.when(t + 1 < nt)
        def _():
            _row_dma_loop(tm, copies(1 - slot, lambda n, k: destn_ref[0, 0, n * TOP_K + k]), wait=False)

    _row_dma_loop(tm, copies(slot, lambda n, k: 0), wait=True)
    gates = gate_ref[...]
    acc = h2_ref[...]
    for k in range(TOP_K):
        acc = acc + gates[:, k:k + 1] * ybuf[slot, k * tm:(k + 1) * tm, :]
    o_ref[...] = _rms(acc, gain_ref[...])


def moe_combine(dest, y_sorted, h2, gates, norm_final):
    m, d = h2.shape
    tm = ROW_TILE
    assert m % tm == 0
    n_tiles = m // tm
    row = lambda t: (t, 0)
    dest3 = dest.reshape(n_tiles, 1, TOP_K * tm)
    return pl.pallas_call(
        functools.partial(_combine_kernel, nt=n_tiles),
        out_shape=jax.ShapeDtypeStruct((m, d), F32),
        grid=(n_tiles,),
        in_specs=[pl.BlockSpec((1, 1, TOP_K * tm), lambda t: (t, 0, 0), memory_space=pltpu.SMEM),
                  pl.BlockSpec((1, 1, TOP_K * tm), lambda t: (jnp.minimum(t + 1, n_tiles - 1), 0, 0),
                               memory_space=pltpu.SMEM),
                  pl.BlockSpec(memory_space=pl.ANY),
                  pl.BlockSpec((tm, d), row), pl.BlockSpec((tm, LANES), row),
                  pl.BlockSpec((1, d), lambda t: (0, 0))],
        out_specs=pl.BlockSpec((tm, d), row),
        scratch_shapes=[pltpu.VMEM((2, TOP_K * tm, d), F32), pltpu.SemaphoreType.DMA((2,))],
        compiler_params=_cparams("arbitrary"),
        name="moe_combine",
    )(dest3, dest3, y_sorted, h2, gates, norm_final.reshape(1, -1))


def _moe_plan(counts_row, n_assign):
    tm = MOE_TILE
    n_tiles = (n_assign + N_EXPERTS * (tm - 1) + tm - 1) // tm
    counts = counts_row[0, :N_EXPERTS].astype(I32)
    tiles_per = (counts + tm - 1) // tm
    tile_end = jnp.cumsum(tiles_per)
    pad_start = (tile_end - tiles_per) * tm
    tile_ids = jnp.arange(n_tiles, dtype=I32)
    used = tile_end[-1]
    tile_valid = (tile_ids < used).astype(I32)
    clamped = jnp.minimum(tile_ids, used - 1)
    tile_expert = jnp.sum((clamped[:, None] >= tile_end[None, :]).astype(I32), axis=1)
    return n_tiles, jnp.minimum(tile_expert, N_EXPERTS - 1), tile_valid, pad_start


def _dest_rows(ids, ranks, pad_start):
    sel = ids[:, :TOP_K, None] == jnp.arange(N_EXPERTS, dtype=I32)
    return ranks[:, :TOP_K] + jnp.sum(jnp.where(sel, pad_start, 0), axis=-1)


def _rope_tables(positions):
    half = MLA_ROPE // 2
    inv_freq = 1.0 / (ROPE_THETA ** (jnp.arange(half, dtype=F32) / half))
    ang = positions.astype(F32)[:, None] * inv_freq[None, :]
    cos, sin = jnp.cos(ang), jnp.sin(ang)
    zeros = jnp.zeros((positions.shape[0], LANES - MLA_ROPE), F32)
    return (jnp.concatenate([cos, cos, zeros], axis=1), jnp.concatenate([-sin, sin, zeros], axis=1))


def _in_proj_weight(w_in):
    o = np.cumsum([0, CONV_CH, GDN_HEADS * GDN_D, GDN_HEADS, GDN_HEADS, MLA_Q_RANK, MLA_KV_RANK, MLA_ROPE])
    conv, z, bl, al, cq, ckv, kpe = (w_in[:, o[i]:o[i + 1]] for i in range(7))
    pad = jnp.zeros((w_in.shape[0], LANES - MLA_ROPE - 2 * GDN_HEADS), w_in.dtype)
    return jnp.concatenate([conv, z, cq, ckv, kpe, bl, al, pad], axis=1).astype(BF16)


IN_SPLITS = (CONV_CH, GDN_HEADS * GDN_D, MLA_Q_RANK, MLA_KV_RANK, LANES)


def _lane_row(vals, lane0):
    return jnp.zeros((1, LANES), F32).at[0, lane0:lane0 + vals.shape[0]].set(vals.astype(F32))


def kernel(x_prompt, x_sample, state_gdn, state_conv, cache_latent, cache_krope, page_table, cache_mem_k, cache_mem_v, mem_prompt, norm_mix, w_in, conv_w, a_log, dt_bias, gdn_norm, q_norm, w_uq, kv_norm, w_uk, w_uv, w_out, norm_x, mem_norm, w_xq, w_xk, w_xv, w_xo, norm_ffn, w_router, b_router, w_e1, b_e1, w_e2, b_e2, norm_final):
    depth = w_in.shape[0]
    assert depth == 1
    L = 0
    bp, t, d = x_prompt.shape
    bs = x_sample.shape[0]
    assert x_sample.shape[1] == 1
    past_len = page_table.shape[1] * cache_latent.shape[2]
    mp = bp * t

    w_in_p = _in_proj_weight(w_in[L])
    alog_row = _lane_row(a_log[L], DECAY_LANE)
    dtb_row = _lane_row(dt_bias[L], DECAY_LANE)
    wuq = w_uq[L]
    wuq_p = jnp.concatenate(
        [wuq, jnp.zeros(wuq.shape[:2] + (MLA_QK_PAD - wuq.shape[2],), wuq.dtype)], axis=2
    ).reshape(wuq.shape[0], MLA_HEADS * MLA_QK_PAD).astype(BF16)
    wuk = w_uk[L].reshape(MLA_KV_RANK, MLA_HEADS * MLA_NOPE).astype(BF16)
    wuv = w_uv[L].reshape(MLA_KV_RANK, MLA_HEADS * MLA_V).astype(BF16)
    wuk_t = jnp.transpose(w_uk[L], (1, 2, 0)).astype(BF16)
    w_out_b = w_out[L].astype(BF16)
    n_gdn = GDN_HEADS * GDN_D
    w_xq_b, w_xo_b = w_xq[L].astype(BF16), w_xo[L].astype(BF16)
    w_xkv_b = jnp.concatenate([w_xk[L], w_xv[L]], axis=1).astype(BF16)
    wr = jnp.concatenate([w_router[L], jnp.zeros((d, LANES - N_EXPERTS), F32)], axis=1)
    wr_hi = wr.astype(BF16)
    wr_lo = (wr - wr_hi.astype(F32)).astype(BF16)
    br_row = _lane_row(b_router[L], 0)
    w2p = _pair_rows(w_e2[L]).astype(BF16)
    b1 = b_e1[L][:, None, :]
    b2 = b_e2[L][:, None, :]
    cos_p, sin_p = _rope_tables(jnp.arange(t, dtype=I32))
    cos_s, sin_s = _rope_tables(jnp.full((bs,), past_len, I32))

    xp = x_prompt.reshape(mp, d)
    xs = x_sample.reshape(bs, d)
    conv_p, z_p, cq_p, ckv_p, kba_p = fused_linear([xp], [w_in_p], gain=norm_mix[L], splits=IN_SPLITS, name="in_proj_prompt")
    conv_s, z_s, cq_s, ckv_s, kba_s = fused_linear([xs], [w_in_p], gain=norm_mix[L], splits=IN_SPLITS, name="in_proj_sample")

    y_gdn_p, gdn_state_p = gdn_prompt(conv_p.reshape(bp, t, CONV_CH), z_p.reshape(bp, t, n_gdn),
                                      kba_p.reshape(bp, t, LANES), conv_w[L], alog_row, dtb_row, gdn_norm[L])
    conv_state_p = conv_p.reshape(bp, t, CONV_CH)[:, t - (CONV_WIDTH - 1):, :]
    gdn_state_s, conv_state_s, y_gdn_s = gdn_sample(conv_s, state_conv[L], kba_s, z_s, state_gdn[L],
                                                    conv_w[L], alog_row, dtb_row, gdn_norm[L])

    q_p, k_p, v_p, lat_p, kpe_p = mla_prep(cq_p, ckv_p, kba_p, cos_p, sin_p, q_norm[L], kv_norm[L],
                                           wuq_p, wuk, wuv, seq=t, q_dtype=BF16)
    y_mla_p = mla_flash(q_p.reshape(bp, t, -1), k_p.reshape(bp, t, -1), v_p.reshape(bp, t, -1))
    q_s, _, _, lat_s, kpe_s = mla_prep(cq_s, ckv_s, kba_s, cos_s, sin_s, q_norm[L], kv_norm[L],
                                       wuq_p, wuk, wuv, seq=bs, q_dtype=F32)
    q_s4 = q_s.reshape(bs, MLA_HEADS, MLA_QK_PAD)
    qlat = jnp.concatenate(
        [fused_linear([q_s4[:, h, :MLA_NOPE]], [wuk_t[h]], name=f"absorb_q{h}")[0][:, None, :]
         for h in range(MLA_HEADS)], axis=1)
    head_pad = ((0, 0), (0, DEC_HEAD_PAD - MLA_HEADS), (0, 0))
    qlat8 = jnp.pad(qlat, head_pad)
    qpe8 = jnp.pad(q_s4[:, :, MLA_NOPE:MLA_NOPE + MLA_ROPE], head_pad)
    y_mla_s = mla_decode(page_table, qlat8, qpe8, lat_s, kpe_s, wuv,
                         cache_latent, jnp.swapaxes(cache_krope, 2, 3)).reshape(bs, -1)

    (h_p,) = fused_linear([y_gdn_p.reshape(mp, n_gdn), y_mla_p.reshape(mp, -1)], [w_out_b[:n_gdn], w_out_b[n_gdn:]],
                          residual=xp, name="out_proj_prompt")
    (h_s,) = fused_linear([y_gdn_s, y_mla_s], [w_out_b[:n_gdn], w_out_b[n_gdn:]], residual=xs, name="out_proj_sample")

    mem_tokens = mem_prompt.shape[1]
    mk_f, mv_f, mk_b, mv_b = fused_linear([mem_prompt.reshape(bp * mem_tokens, d)], [w_xkv_b], gain=mem_norm[L],
                                          splits=(d, d), bf16_copies=True, name="memory_kv")
    (qx_p,) = fused_linear([h_p], [w_xq_b], gain=norm_x[L], out_dtypes=(BF16,), name="xq_prompt")
    (qx_s,) = fused_linear([h_s], [w_xq_b], gain=norm_x[L], out_dtypes=(BF16,), name="xq_sample")
    ox_p = xattn_prompt(qx_p.reshape(bp, t, d), mk_b.reshape(bp, mem_tokens, d), mv_b.reshape(bp, mem_tokens, d))
    ox_s = xattn_sample(qx_s.reshape(bs, X_HEADS, d // X_HEADS), cache_mem_k[L], cache_mem_v[L])

    h2_p, hn_p, gate_p, ids_p, rank_p, cnt_p = post_xattn(ox_p.reshape(mp, d), w_xo_b, h_p, norm_ffn[L],
                                                          wr_hi, wr_lo, br_row, jnp.zeros((1, LANES), F32))
    h2_s, hn_s, gate_s, ids_s, rank_s, cnt_all = post_xattn(ox_s.reshape(bs, d), w_xo_b, h_s, norm_ffn[L],
                                                            wr_hi, wr_lo, br_row, cnt_p)

    n_tiles, tile_expert, tile_valid, pad_start = _moe_plan(cnt_all, TOP_K * (mp + bs))
    dest_p = _dest_rows(ids_p, rank_p, pad_start)
    dest_s = _dest_rows(ids_s, rank_s, pad_start)
    x_grouped = jnp.zeros((n_tiles * MOE_TILE, d), F32)
    x_grouped = moe_dispatch(dest_p, hn_p, x_grouped)
    x_grouped = moe_dispatch(dest_s, hn_s, x_grouped)
    y_sorted = moe_experts(tile_expert, tile_valid, x_grouped, w_e1[L], b1, w2p, b2)
    y_prompt = moe_combine(dest_p, y_sorted, h2_p, gate_p, norm_final)
    y_sample = moe_combine(dest_s, y_sorted, h2_s, gate_s, norm_final)

    x_heads = X_HEADS
    return (y_prompt.reshape(bp, t, d), y_sample.reshape(bs, 1, d),
            gdn_state_p[None], conv_state_p[None],
            lat_p.reshape(bp, 1, t, MLA_KV_RANK), kpe_p.reshape(bp, 1, t, MLA_ROPE),
            mk_f.reshape(1, bp, mem_tokens, x_heads, d // x_heads), mv_f.reshape(1, bp, mem_tokens, x_heads, d // x_heads),
            gdn_state_s[None], conv_state_s[None],
            lat_s.reshape(bs, 1, 1, MLA_KV_RANK), kpe_s.reshape(bs, 1, 1, MLA_ROPE))
```

```python
import functools

import jax
import jax.numpy as jnp
import numpy as np
from jax import lax
from jax.experimental import pallas as pl
from jax.experimental.pallas import tpu as pltpu

F32 = jnp.float32
BF16 = jnp.bfloat16
I32 = jnp.int32

NORM_EPS = 1e-6
LANES = 128
SUBLANES = 8
VMEM_LIMIT = 48 * 1024 * 1024
MOE_VMEM_LIMIT = 56 * 1024 * 1024

GDN_HEADS = 4
GDN_D = 128
CONV_WIDTH = 4
CONV_CH = 3 * GDN_HEADS * GDN_D
GDN_BLOCK = 128
MLA_HEADS = 4
MLA_NOPE = 128
MLA_ROPE = 64
MLA_V = 128
MLA_Q_RANK = 384
MLA_KV_RANK = 256
MLA_QK_PAD = 256
ROPE_THETA = 10000.0
MLA_SCALE = (MLA_NOPE + MLA_ROPE) ** -0.5
X_HEADS = 4
N_EXPERTS = 32
TOP_K = 4
SWIGLU_LIMIT = 7.0
SWIGLU_ALPHA = 1.702
BETA_LANE = MLA_ROPE
DECAY_LANE = MLA_ROPE + GDN_HEADS
NEG_BIG = -1e30


def _cparams(*sem):
    return pltpu.CompilerParams(dimension_semantics=sem, vmem_limit_bytes=VMEM_LIMIT)


def _rms(x, gain):
    return x * lax.rsqrt(jnp.mean(x * x, axis=-1, keepdims=True) + NORM_EPS) * gain


def _mm(a, b):
    return jnp.dot(a.astype(BF16), b.astype(BF16), preferred_element_type=F32)


def _mm_nt(a, b):
    return lax.dot_general(a.astype(BF16), b.astype(BF16), (((1,), (1,)), ((), ())),
                           preferred_element_type=F32)


def _mm3(a, b):
    a_hi = a.astype(BF16)
    b_hi = b.astype(BF16)
    a_lo = (a - a_hi.astype(F32)).astype(BF16)
    b_lo = (b - b_hi.astype(F32)).astype(BF16)
    return jnp.dot(jnp.concatenate([a_hi, a_lo, a_hi], axis=1), jnp.concatenate([b_hi, b_hi, b_lo], axis=0),
                   preferred_element_type=F32)


def _sigmoid(x):
    return 1.0 / (1.0 + jnp.exp(-x))


def _softplus(x):
    return jnp.maximum(x, 0.0) + jnp.log1p(jnp.exp(-jnp.abs(x)))


def _linear_kernel(*refs, n_in, has_gain, has_res, splits):
    a_refs = refs[:n_in]
    w_refs = refs[n_in:2 * n_in]
    pos = 2 * n_in
    g_ref = refs[pos] if has_gain else None
    pos += int(has_gain)
    r_ref = refs[pos] if has_res else None
    pos += int(has_res)
    out_refs = refs[pos:]
    a0 = a_refs[0][...]
    if has_gain:
        a0 = _rms(a0.astype(F32), g_ref[...])
    acts = [a0.astype(BF16)] + [a[...].astype(BF16) for a in a_refs[1:]]
    off = 0
    for i, width in enumerate(splits):
        acc = None
        for a, w in zip(acts, w_refs):
            d = jnp.dot(a, w[:, off:off + width], preferred_element_type=F32)
            acc = d if acc is None else acc + d
        if has_res:
            acc = acc + r_ref[:, off:off + width]
        for o_ref in out_refs[i::len(splits)]:
            o_ref[...] = acc.astype(o_ref.dtype)
        off += width


def fused_linear(acts, weights, *, gain=None, residual=None, splits=None, out_dtypes=None,
                 bf16_copies=False, tm=256, name="fused_linear"):
    m = acts[0].shape[0]
    n = weights[0].shape[1]
    tm = min(tm, m)
    assert m % tm == 0
    splits = tuple(splits) if splits is not None else (n,)
    assert sum(splits) == n and all(s % LANES == 0 for s in splits)
    out_dtypes = tuple(out_dtypes) if out_dtypes is not None else (F32,) * len(splits)
    out_widths = splits
    if bf16_copies:
        out_widths = splits + splits
        out_dtypes = out_dtypes + (BF16,) * len(splits)
    in_specs = [pl.BlockSpec((tm, a.shape[1]), lambda i: (i, 0)) for a in acts]
    in_specs += [pl.BlockSpec(w.shape, lambda i: (0, 0)) for w in weights]
    args = list(acts) + list(weights)
    if gain is not None:
        in_specs.append(pl.BlockSpec((1, gain.shape[-1]), lambda i: (0, 0)))
        args.append(gain.reshape(1, -1))
    if residual is not None:
        in_specs.append(pl.BlockSpec((tm, n), lambda i: (i, 0)))
        args.append(residual)
    outs = pl.pallas_call(
        functools.partial(_linear_kernel, n_in=len(acts), has_gain=gain is not None,
                          has_res=residual is not None, splits=splits),
        out_shape=[jax.ShapeDtypeStruct((m, s), dt) for s, dt in zip(out_widths, out_dtypes)],
        grid=(m // tm,),
        in_specs=in_specs,
        out_specs=[pl.BlockSpec((tm, s), lambda i: (i, 0)) for s in out_widths],
        compiler_params=_cparams("parallel"),
        name=name,
    )(*args)
    return outs


def _gate_values(kba, alog_row, dtb_row):
    beta = _sigmoid(kba)
    g = -jnp.exp(alog_row) * _softplus(kba + dtb_row)
    return beta, g


def _l2norm(x):
    return x * lax.rsqrt(jnp.sum(x * x, axis=-1, keepdims=True) + NORM_EPS)


GDN_PROMPT_BB = 2


def _gdn_prompt_kernel(x_ref, z_ref, kba_ref, cw_ref, alog_ref, dtb_ref, gn_ref,
                       y_ref, s_out_ref, xbuf, state):
    t = pl.program_id(1)
    nt = pl.num_programs(1)

    @pl.when(t == 0)
    def _():
        xbuf[:, 0:SUBLANES, :] = jnp.zeros((GDN_PROMPT_BB, SUBLANES, CONV_CH), F32)
        state[...] = jnp.zeros(state.shape, F32)

    for bi in range(GDN_PROMPT_BB):
        _gdn_prompt_block(x_ref.at[bi], z_ref.at[bi], kba_ref.at[bi], cw_ref, alog_ref, dtb_ref, gn_ref,
                          y_ref.at[bi], xbuf.at[bi], state.at[bi])

    @pl.when(t == nt - 1)
    def _():
        s_out_ref[...] = state[...]


def _gdn_prompt_block(x_ref, z_ref, kba_ref, cw_ref, alog_ref, dtb_ref, gn_ref, y_ref, xbuf, state):
    blk = GDN_BLOCK
    hist = CONV_WIDTH - 1
    xbuf[SUBLANES:SUBLANES + blk, :] = x_ref[...]
    cw = cw_ref[...]
    conv = xbuf[SUBLANES - hist:SUBLANES - hist + blk, :] * cw[0:1, :]
    for j in range(1, CONV_WIDTH):
        conv = conv + xbuf[SUBLANES - hist + j:SUBLANES - hist + j + blk, :] * cw[j:j + 1, :]
    xbuf[SUBLANES - hist:SUBLANES, :] = xbuf[SUBLANES + blk - hist:SUBLANES + blk, :]
    c = conv * _sigmoid(conv)

    kba = kba_ref[...]
    beta_all, g_all = _gate_values(kba, alog_ref[...], dtb_ref[...])
    row = lax.broadcasted_iota(I32, (blk, blk), 0)
    col = lax.broadcasted_iota(I32, (blk, blk), 1)
    gc = g_all
    shift = 1
    while shift < blk:
        rolled = pltpu.roll(gc, shift, 0)
        gc = gc + jnp.where(row >= shift, rolled, 0.0)
        shift *= 2
    gc_t = gc.T
    incl = row >= col
    strict = row > col
    eye = (row == col).astype(F32)
    z = z_ref[...]
    gn = gn_ref[...]
    nh = GDN_HEADS * GDN_D

    for h in range(GDN_HEADS):
        q = _l2norm(c[:, h * GDN_D:(h + 1) * GDN_D]) * (GDN_D ** -0.5)
        k = _l2norm(c[:, nh + h * GDN_D:nh + (h + 1) * GDN_D])
        v = c[:, 2 * nh + h * GDN_D:2 * nh + (h + 1) * GDN_D]
        bcol = beta_all[:, BETA_LANE + h:BETA_LANE + h + 1]
        gcol = gc[:, DECAY_LANE + h:DECAY_LANE + h + 1]
        grow = gc_t[DECAY_LANE + h:DECAY_LANE + h + 1, :]
        decay = jnp.exp(jnp.where(incl, gcol - grow, NEG_BIG))
        kb = k * bcol
        vb = v * bcol
        a = jnp.where(strict, _mm_nt(kb, k) * decay, 0.0)
        x = eye - a
        p = _mm3(a, a)
        x = x + _mm3(x, p)
        for _ in range(5):
            p = _mm3(p, p)
            x = x + _mm3(x, p)
        egc = jnp.exp(gcol)
        u = _mm(x, vb)
        w = _mm(x, kb * egc)
        intra = _mm_nt(q, k) * decay
        s_h = state[h]
        v_new = u - _mm(w, s_h)
        o = _mm(q * egc, s_h) + _mm(intra, v_new)
        g_last = gcol[blk - 1:blk, :]
        kd = k * jnp.exp(g_last - gcol)
        state[h] = s_h * jnp.exp(g_last) + _mm(kd.T, v_new)
        zz = z[:, h * GDN_D:(h + 1) * GDN_D]
        y_ref[:, h * GDN_D:(h + 1) * GDN_D] = (_rms(o, gn) * (zz * _sigmoid(zz))).astype(y_ref.dtype)


def gdn_prompt(conv_in, z, kba, conv_w, alog_row, dtb_row, gdn_norm):
    b, t, _ = conv_in.shape
    bb = GDN_PROMPT_BB
    assert t % GDN_BLOCK == 0 and b % bb == 0
    nt = t // GDN_BLOCK
    y, s = pl.pallas_call(
        _gdn_prompt_kernel,
        out_shape=[jax.ShapeDtypeStruct((b, t, GDN_HEADS * GDN_D), BF16),
                   jax.ShapeDtypeStruct((b, GDN_HEADS, GDN_D, GDN_D), F32)],
        grid=(b // bb, nt),
        in_specs=[pl.BlockSpec((bb, GDN_BLOCK, CONV_CH), lambda i, j: (i, j, 0)),
                  pl.BlockSpec((bb, GDN_BLOCK, GDN_HEADS * GDN_D), lambda i, j: (i, j, 0)),
                  pl.BlockSpec((bb, GDN_BLOCK, LANES), lambda i, j: (i, j, 0)),
                  pl.BlockSpec((CONV_WIDTH, CONV_CH), lambda i, j: (0, 0)),
                  pl.BlockSpec((1, LANES), lambda i, j: (0, 0)),
                  pl.BlockSpec((1, LANES), lambda i, j: (0, 0)),
                  pl.BlockSpec((1, GDN_D), lambda i, j: (0, 0))],
        out_specs=[pl.BlockSpec((bb, GDN_BLOCK, GDN_HEADS * GDN_D), lambda i, j: (i, j, 0)),
                   pl.BlockSpec((bb, GDN_HEADS, GDN_D, GDN_D), lambda i, j: (i, 0, 0, 0))],
        scratch_shapes=[pltpu.VMEM((bb, SUBLANES + GDN_BLOCK, CONV_CH), F32),
                        pltpu.VMEM((bb, GDN_HEADS, GDN_D, GDN_D), F32)],
        compiler_params=_cparams("parallel", "arbitrary"),
        name="gdn_prompt",
    )(conv_in, z, kba, conv_w, alog_row, dtb_row, gdn_norm.reshape(1, -1))
    return y, s


GDN_SAMPLE_BB = 8


def _gdn_sample_kernel(x_ref, sc_ref, kba_ref, z_ref, s_ref, cw_ref, alog_ref, dtb_ref, gn_ref,
                       s_out_ref, sc_out_ref, y_ref, tbuf):
    bb = GDN_SAMPLE_BB
    x = x_ref[...]
    cw = cw_ref[...]
    conv = x * cw[CONV_WIDTH - 1:CONV_WIDTH, :]
    for j in range(CONV_WIDTH - 1):
        conv = conv + sc_ref[:, j, :] * cw[j:j + 1, :]
    for j in range(CONV_WIDTH - 2):
        sc_out_ref[:, j, :] = sc_ref[:, j + 1, :]
    sc_out_ref[:, CONV_WIDTH - 2, :] = x
    c = conv * _sigmoid(conv)
    beta_all, g_all = _gate_values(kba_ref[...], alog_ref[...], dtb_ref[...])
    eg_all = jnp.exp(g_all)
    z = z_ref[...]
    gn = gn_ref[...]
    nh = GDN_HEADS * GDN_D
    tbuf[...] = jnp.zeros(tbuf.shape, F32)
    for h in range(GDN_HEADS):
        q = _l2norm(c[:, h * GDN_D:(h + 1) * GDN_D]) * (GDN_D ** -0.5)
        k = _l2norm(c[:, nh + h * GDN_D:nh + (h + 1) * GDN_D])
        v = c[:, 2 * nh + h * GDN_D:2 * nh + (h + 1) * GDN_D]
        tbuf[0:bb, :] = q
        q_t = tbuf[...].T
        tbuf[0:bb, :] = k
        k_t = tbuf[...].T
        for b in range(bb):
            qcol = q_t[:, b:b + 1]
            kcol = k_t[:, b:b + 1]
            eg = eg_all[b:b + 1, DECAY_LANE + h:DECAY_LANE + h + 1]
            beta = beta_all[b:b + 1, BETA_LANE + h:BETA_LANE + h + 1]
            s1 = s_ref[b, h] * eg
            pred = jnp.sum(s1 * kcol, axis=0, keepdims=True)
            u = (v[b:b + 1, :] - pred) * beta
            s2 = s1 + kcol * u
            s_out_ref[b, h] = s2
            o = jnp.sum(s2 * qcol, axis=0, keepdims=True)
            zz = z[b:b + 1, h * GDN_D:(h + 1) * GDN_D]
            y_ref[b:b + 1, h * GDN_D:(h + 1) * GDN_D] = (_rms(o, gn) * (zz * _sigmoid(zz))).astype(y_ref.dtype)


def gdn_sample(conv_in, state_conv, kba, z, state_gdn, conv_w, alog_row, dtb_row, gdn_norm):
    b = conv_in.shape[0]
    bb = GDN_SAMPLE_BB
    assert b % bb == 0
    hist = CONV_WIDTH - 1
    return pl.pallas_call(
        _gdn_sample_kernel,
        out_shape=[jax.ShapeDtypeStruct(state_gdn.shape, F32),
                   jax.ShapeDtypeStruct(state_conv.shape, F32),
                   jax.ShapeDtypeStruct((b, GDN_HEADS * GDN_D), BF16)],
        grid=(b // bb,),
        in_specs=[pl.BlockSpec((bb, CONV_CH), lambda i: (i, 0)),
                  pl.BlockSpec((bb, hist, CONV_CH), lambda i: (i, 0, 0)),
                  pl.BlockSpec((bb, LANES), lambda i: (i, 0)),
                  pl.BlockSpec((bb, GDN_HEADS * GDN_D), lambda i: (i, 0)),
                  pl.BlockSpec((bb, GDN_HEADS, GDN_D, GDN_D), lambda i: (i, 0, 0, 0)),
                  pl.BlockSpec((CONV_WIDTH, CONV_CH), lambda i: (0, 0)),
                  pl.BlockSpec((1, LANES), lambda i: (0, 0)),
                  pl.BlockSpec((1, LANES), lambda i: (0, 0)),
                  pl.BlockSpec((1, GDN_D), lambda i: (0, 0))],
        out_specs=[pl.BlockSpec((bb, GDN_HEADS, GDN_D, GDN_D), lambda i: (i, 0, 0, 0)),
                   pl.BlockSpec((bb, hist, CONV_CH), lambda i: (i, 0, 0)),
                   pl.BlockSpec((bb, GDN_HEADS * GDN_D), lambda i: (i, 0))],
        scratch_shapes=[pltpu.VMEM((GDN_D, GDN_D), F32)],
        compiler_params=_cparams("parallel"),
        name="gdn_sample",
    )(conv_in, state_conv, kba, z, state_gdn, conv_w, alog_row, dtb_row, gdn_norm.reshape(1, -1))


def _rope128(x, cos, sin):
    half = MLA_ROPE // 2
    lane = lax.broadcasted_iota(I32, x.shape, 1)
    swapped = jnp.where(lane < half, pltpu.roll(x, LANES - half, 1), pltpu.roll(x, half, 1))
    return x * cos + swapped * sin


def _mla_prep_kernel(cq_ref, ckv_ref, kba_ref, cos_ref, sin_ref, qn_ref, kvn_ref, wuq_ref, wuk_ref, wuv_ref,
                     q_ref, k_ref, v_ref, lat_ref, kpe_ref):
    cos = cos_ref[...]
    sin = sin_ref[...]
    qn = _rms(cq_ref[...], qn_ref[...]).astype(BF16)
    lat = _rms(ckv_ref[...], kvn_ref[...])
    lat_ref[...] = lat
    lat_b = lat.astype(BF16)
    kpe = _rope128(kba_ref[...], cos, sin)
    kpe_ref[...] = kpe[:, :MLA_ROPE]
    for h in range(MLA_HEADS):
        lo = h * MLA_QK_PAD
        q_ref[:, lo:lo + MLA_NOPE] = jnp.dot(
            qn, wuq_ref[:, lo:lo + MLA_NOPE], preferred_element_type=F32).astype(q_ref.dtype)
        q_pe = jnp.dot(qn, wuq_ref[:, lo + MLA_NOPE:lo + MLA_QK_PAD], preferred_element_type=F32)
        q_ref[:, lo + MLA_NOPE:lo + MLA_QK_PAD] = _rope128(q_pe, cos, sin).astype(q_ref.dtype)
        k_ref[:, lo:lo + MLA_NOPE] = jnp.dot(
            lat_b, wuk_ref[:, h * MLA_NOPE:(h + 1) * MLA_NOPE], preferred_element_type=F32).astype(k_ref.dtype)
        k_ref[:, lo + MLA_NOPE:lo + MLA_QK_PAD] = kpe.astype(k_ref.dtype)
    v_ref[...] = jnp.dot(lat_b, wuv_ref[...], preferred_element_type=F32).astype(v_ref.dtype)


def mla_prep(c_q, c_kv, kba, cos_tab, sin_tab, q_norm, kv_norm, wuq_p, wuk, wuv, *, seq, q_dtype, tm=256):
    m = c_q.shape[0]
    tm = min(tm, m, seq)
    assert m % tm == 0 and seq % tm == 0
    nseq = seq // tm
    hq = MLA_HEADS * MLA_QK_PAD
    row = lambda i: (i, 0)
    const = lambda i: (0, 0)
    return pl.pallas_call(
        _mla_prep_kernel,
        out_shape=[jax.ShapeDtypeStruct((m, hq), q_dtype),
                   jax.ShapeDtypeStruct((m, hq), BF16),
                   jax.ShapeDtypeStruct((m, MLA_HEADS * MLA_V), BF16),
                   jax.ShapeDtypeStruct((m, MLA_KV_RANK), F32),
                   jax.ShapeDtypeStruct((m, MLA_ROPE), F32)],
        grid=(m // tm,),
        in_specs=[pl.BlockSpec((tm, MLA_Q_RANK), row),
                  pl.BlockSpec((tm, MLA_KV_RANK), row),
                  pl.BlockSpec((tm, LANES), row),
                  pl.BlockSpec((tm, LANES), lambda i: (i % nseq, 0)),
                  pl.BlockSpec((tm, LANES), lambda i: (i % nseq, 0)),
                  pl.BlockSpec((1, MLA_Q_RANK), const),
                  pl.BlockSpec((1, MLA_KV_RANK), const),
                  pl.BlockSpec(wuq_p.shape, const),
                  pl.BlockSpec(wuk.shape, const),
                  pl.BlockSpec(wuv.shape, const)],
        out_specs=[pl.BlockSpec((tm, hq), row),
                   pl.BlockSpec((tm, hq), row),
                   pl.BlockSpec((tm, MLA_HEADS * MLA_V), row),
                   pl.BlockSpec((tm, MLA_KV_RANK), row),
                   pl.BlockSpec((tm, MLA_ROPE), row)],
        compiler_params=_cparams("parallel"),
        name="mla_prep",
    )(c_q, c_kv, kba, cos_tab, sin_tab, q_norm.reshape(1, -1), kv_norm.reshape(1, -1), wuq_p, wuk, wuv)


def _flash_kernel(q_ref, k_ref, v_ref, o_ref, m_s, l_s, acc_s, *, tq):
    qi = pl.program_id(2)
    q = q_ref[0]
    m_s[...] = jnp.full(m_s.shape, NEG_BIG, F32)
    l_s[...] = jnp.zeros(l_s.shape, F32)
    acc_s[...] = jnp.zeros(acc_s.shape, F32)

    def block(j, masked):
        k0 = pl.multiple_of(j * tq, tq)
        s = lax.dot_general(q, k_ref[0, pl.ds(k0, tq), :], (((1,), (1,)), ((), ())),
                            preferred_element_type=F32) * MLA_SCALE
        if masked:
            s = jnp.where(lax.broadcasted_iota(I32, (tq, tq), 1) <= lax.broadcasted_iota(I32, (tq, tq), 0),
                          s, NEG_BIG)
        m_prev = m_s[...]
        m_new = jnp.maximum(m_prev, jnp.max(s, axis=-1, keepdims=True))
        corr = jnp.exp(m_prev - m_new)
        p = jnp.exp(s - m_new)
        l_s[...] = l_s[...] * corr + jnp.sum(p, axis=-1, keepdims=True)
        acc_s[...] = acc_s[...] * corr + jnp.dot(p.astype(BF16), v_ref[0, pl.ds(k0, tq), :],
                                                 preferred_element_type=F32)
        m_s[...] = m_new

    def below_diagonal(j, _):
        block(j, masked=False)
        return 0

    lax.fori_loop(0, qi, below_diagonal, 0)
    block(qi, masked=True)
    o_ref[0] = (acc_s[...] / l_s[...]).astype(o_ref.dtype)


def mla_flash(q, k, v, *, tq=512):
    b, t, _ = q.shape
    tq = min(tq, t)
    assert t % tq == 0
    seq_map = lambda bi, h, qi: (bi, 0, h)
    return pl.pallas_call(
        functools.partial(_flash_kernel, tq=tq),
        out_shape=jax.ShapeDtypeStruct((b, t, MLA_HEADS * MLA_V), BF16),
        grid=(b, MLA_HEADS, t // tq),
        in_specs=[pl.BlockSpec((1, tq, MLA_QK_PAD), lambda bi, h, qi: (bi, qi, h)),
                  pl.BlockSpec((1, t, MLA_QK_PAD), seq_map),
                  pl.BlockSpec((1, t, MLA_V), seq_map)],
        out_specs=pl.BlockSpec((1, tq, MLA_V), lambda bi, h, qi: (bi, qi, h)),
        scratch_shapes=[pltpu.VMEM((tq, 1), F32), pltpu.VMEM((tq, 1), F32), pltpu.VMEM((tq, MLA_V), F32)],
        compiler_params=_cparams("parallel", "parallel", "arbitrary"),
        name="mla_flash",
    )(q, k, v)


DEC_HEAD_PAD = 8
DEC_GROUP = 16
DEC_SLOTS = 4


def _decode_kernel(pt_ref, ptn_ref, qlat_ref, qpe_ref, latn_ref, kpen_ref, wuv_ref, lat_hbm, kpe_hbm,
                   o_ref, latbuf, kpebuf, sem, *, nb, n_pages, page):
    b = pl.program_id(0)
    grp = DEC_GROUP
    ns = DEC_SLOTS
    n_groups = n_pages // grp
    n_outer = n_groups // ns

    def page_copies(pg, slot, j):
        return (pltpu.make_async_copy(lat_hbm.at[pg, 0], latbuf.at[slot, j], sem.at[0, slot]),
                pltpu.make_async_copy(kpe_hbm.at[pg, 0], kpebuf.at[slot, j], sem.at[1, slot]))

    def start_group(tbl_ref, g, slot):
        for j in range(grp):
            for cp in page_copies(tbl_ref[0, 0, g * grp + j], slot, j):
                cp.start()

    def wait_group(slot):
        for j in range(grp):
            for cp in page_copies(0, slot, j):
                cp.wait()

    @pl.when(b == 0)
    def _():
        for s in range(ns - 1):
            start_group(pt_ref, s, s)

    qlat = qlat_ref[0]
    qpe = qpe_ref[0]
    latn = latn_ref[0]
    kpen = kpen_ref[0]
    qlat_b = qlat.astype(BF16)
    qpe_b = qpe.astype(BF16)
    s_new = (jnp.sum(qlat * latn, axis=-1, keepdims=True)
             + jnp.sum(qpe * kpen, axis=-1, keepdims=True)) * MLA_SCALE
    m0 = s_new
    l0 = jnp.ones_like(s_new)
    acc0 = jnp.broadcast_to(latn, qlat.shape)

    def consume(slot, carry):
        m, l, acc = carry
        lat = latbuf[slot].reshape(grp * page, MLA_KV_RANK).astype(BF16)
        kpe_t = jnp.concatenate([kpebuf[slot, j] for j in range(grp)], axis=1).astype(BF16)
        s = (lax.dot_general(qlat_b, lat, (((1,), (1,)), ((), ())), preferred_element_type=F32)
             + jnp.dot(qpe_b, kpe_t, preferred_element_type=F32)) * MLA_SCALE
        m_new = jnp.maximum(m, jnp.max(s, axis=-1, keepdims=True))
        corr = jnp.exp(m - m_new)
        p = jnp.exp(s - m_new)
        l = l * corr + jnp.sum(p, axis=-1, keepdims=True)
        acc = acc * corr + jnp.dot(p.astype(BF16), lat, preferred_element_type=F32)
        return m_new, l, acc

    def ring_body(i, carry):
        for j in range(ns):
            ahead = (j + ns - 1) % ns
            if j == 0:
                start_group(pt_ref, i * ns + ns - 1, ahead)
            else:
                @pl.when(i + 1 < n_outer)
                def _():
                    start_group(pt_ref, (i + 1) * ns + ahead, ahead)

                @pl.when(jnp.logical_and(i + 1 >= n_outer, b + 1 < nb))
                def _():
                    start_group(ptn_ref, ahead, ahead)
            wait_group(j)
            carry = consume(j, carry)
        return carry

    m, l, acc = lax.fori_loop(0, n_outer, ring_body, (m0, l0, acc0))
    o_lat = (acc / l).astype(BF16)
    res = jnp.dot(o_lat, wuv_ref[...], preferred_element_type=F32)
    o_ref[0] = jnp.concatenate(
        [res[h:h + 1, h * MLA_V:(h + 1) * MLA_V] for h in range(MLA_HEADS)], axis=1).astype(o_ref.dtype)


def mla_decode(page_table, qlat8, qpe8, lat_new, kpe_new, wuv, cache_latent, cache_krope_t):
    b, n_pages = page_table.shape
    page = cache_latent.shape[2]
    assert n_pages % (DEC_SLOTS * DEC_GROUP) == 0
    pt3 = page_table.reshape(b, 1, n_pages)
    smem_row = lambda f: pl.BlockSpec((1, 1, n_pages), f, memory_space=pltpu.SMEM)
    return pl.pallas_call(
        functools.partial(_decode_kernel, nb=b, n_pages=n_pages, page=page),
        out_shape=jax.ShapeDtypeStruct((b, 1, MLA_HEADS * MLA_V), BF16),
        grid=(b,),
        in_specs=[smem_row(lambda i: (i, 0, 0)),
                  smem_row(lambda i: (jnp.minimum(i + 1, b - 1), 0, 0)),
                  pl.BlockSpec((1, DEC_HEAD_PAD, MLA_KV_RANK), lambda i: (i, 0, 0)),
                  pl.BlockSpec((1, DEC_HEAD_PAD, MLA_ROPE), lambda i: (i, 0, 0)),
                  pl.BlockSpec((1, 1, MLA_KV_RANK), lambda i: (i, 0, 0)),
                  pl.BlockSpec((1, 1, MLA_ROPE), lambda i: (i, 0, 0)),
                  pl.BlockSpec(wuv.shape, lambda i: (0, 0)),
                  pl.BlockSpec(memory_space=pl.ANY),
                  pl.BlockSpec(memory_space=pl.ANY)],
        out_specs=pl.BlockSpec((1, 1, MLA_HEADS * MLA_V), lambda i: (i, 0, 0)),
        scratch_shapes=[pltpu.VMEM((DEC_SLOTS, DEC_GROUP, page, MLA_KV_RANK), F32),
                        pltpu.VMEM((DEC_SLOTS, DEC_GROUP, MLA_ROPE, page), F32),
                        pltpu.SemaphoreType.DMA((2, DEC_SLOTS))],
        compiler_params=_cparams("arbitrary"),
        name="mla_decode",
    )(pt3, pt3, qlat8, qpe8, lat_new.reshape(b, 1, -1), kpe_new.reshape(b, 1, -1), wuv,
      cache_latent, cache_krope_t)


def _softmax_rows(s):
    m = jnp.max(s, axis=-1, keepdims=True)
    p = jnp.exp(s - m)
    return p / jnp.sum(p, axis=-1, keepdims=True)


def _xattn_prompt_kernel(q_ref, k_ref, v_ref, o_ref, *, dh):
    scale = dh ** -0.5
    for h in range(X_HEADS):
        q = q_ref[0, :, h * dh:(h + 1) * dh]
        s = lax.dot_general(q, k_ref[0, :, h * dh:(h + 1) * dh], (((1,), (1,)), ((), ())),
                            preferred_element_type=F32) * scale
        p = _softmax_rows(s).astype(BF16)
        o_ref[0, :, h * dh:(h + 1) * dh] = jnp.dot(
            p, v_ref[0, :, h * dh:(h + 1) * dh], preferred_element_type=F32).astype(o_ref.dtype)


def xattn_prompt(q, k, v, *, tq=512):
    b, t, d = q.shape
    mem = k.shape[1]
    tq = min(tq, t)
    return pl.pallas_call(
        functools.partial(_xattn_prompt_kernel, dh=d // X_HEADS),
        out_shape=jax.ShapeDtypeStruct((b, t, d), BF16),
        grid=(b, t // tq),
        in_specs=[pl.BlockSpec((1, tq, d), lambda i, j: (i, j, 0)),
                  pl.BlockSpec((1, mem, d), lambda i, j: (i, 0, 0)),
                  pl.BlockSpec((1, mem, d), lambda i, j: (i, 0, 0))],
        out_specs=pl.BlockSpec((1, tq, d), lambda i, j: (i, j, 0)),
        compiler_params=_cparams("parallel", "parallel"),
        name="xattn_prompt",
    )(q, k, v)


XATTN_SAMPLE_BB = 2


def _xattn_sample_kernel(q_ref, k_ref, v_ref, o_ref, *, dh):
    scale = dh ** -0.5
    for b in range(XATTN_SAMPLE_BB):
        q = q_ref[b].astype(F32)
        s = jnp.sum(k_ref[b] * q[None], axis=-1, keepdims=True) * scale
        m = jnp.max(s, axis=0, keepdims=True)
        p = jnp.exp(s - m)
        denom = jnp.sum(p, axis=0, keepdims=True)
        o = jnp.sum(p * v_ref[b], axis=0) / denom[0]
        o_ref[b] = o.astype(o_ref.dtype)


def xattn_sample(q, k, v):
    b, heads, dh = q.shape
    mem = k.shape[1]
    bb = XATTN_SAMPLE_BB
    assert b % bb == 0
    kv_spec = pl.BlockSpec((bb, mem, heads, dh), lambda i: (i, 0, 0, 0))
    return pl.pallas_call(
        functools.partial(_xattn_sample_kernel, dh=dh),
        out_shape=jax.ShapeDtypeStruct((b, heads, dh), BF16),
        grid=(b // bb,),
        in_specs=[pl.BlockSpec((bb, heads, dh), lambda i: (i, 0, 0)), kv_spec, kv_spec],
        out_specs=pl.BlockSpec((bb, heads, dh), lambda i: (i, 0, 0)),
        compiler_params=_cparams("parallel"),
        name="xattn_sample",
    )(q, k, v)


def _post_xattn_kernel(ox_ref, wxo_ref, h_ref, g_ref, wr_hi_ref, wr_lo_ref, br_ref, cnt_in_ref,
                       h2_ref, hn_ref, gate_ref, idx_ref, rank_ref, cnt_ref, cnt_s):
    @pl.when(pl.program_id(0) == 0)
    def _():
        cnt_s[...] = cnt_in_ref[...]

    h2 = h_ref[...] + jnp.dot(ox_ref[...], wxo_ref[...], preferred_element_type=F32)
    h2_ref[...] = h2
    hn = _rms(h2, g_ref[...])
    hn_ref[...] = hn.reshape(hn_ref.shape)
    hn_hi = hn.astype(BF16)
    hn_lo = (hn - hn_hi.astype(F32)).astype(BF16)
    logits = (jnp.dot(hn_hi, wr_hi_ref[...], preferred_element_type=F32)
              + jnp.dot(hn_hi, wr_lo_ref[...], preferred_element_type=F32)
              + jnp.dot(hn_lo, wr_hi_ref[...], preferred_element_type=F32)) + br_ref[...]
    lane = lax.broadcasted_iota(I32, logits.shape, 1)
    logits = jnp.where(lane < N_EXPERTS, logits, NEG_BIG)
    vals, picks = [], []
    gates = jnp.zeros(logits.shape, F32)
    ids = jnp.zeros(logits.shape, I32)
    member = jnp.zeros(logits.shape, F32)
    for k in range(TOP_K):
        m = jnp.max(logits, axis=-1, keepdims=True)
        idx = jnp.min(jnp.where(logits == m, lane, LANES), axis=-1, keepdims=True)
        vals.append(m)
        picks.append(idx)
        ids = jnp.where(lane == k, idx, ids)
        member = jnp.where(lane == idx, 1.0, member)
        logits = jnp.where(lane == idx, NEG_BIG, logits)
    exps = [jnp.exp(v - vals[0]) for v in vals]
    denom = exps[0]
    for e in exps[1:]:
        denom = denom + e
    for k in range(TOP_K):
        gates = jnp.where(lane == k, exps[k] / denom, gates)
    gate_ref[...] = gates
    idx_ref[...] = ids
    tm = logits.shape[0]
    before = (lax.broadcasted_iota(I32, (tm, tm), 0) > lax.broadcasted_iota(I32, (tm, tm), 1)).astype(BF16)
    prior = jnp.dot(before, member.astype(BF16), preferred_element_type=F32) + cnt_s[...]
    ranks = jnp.zeros(logits.shape, I32)
    for k in range(TOP_K):
        rk = jnp.sum(jnp.where(lane == picks[k], prior, 0.0), axis=-1, keepdims=True)
        ranks = jnp.where(lane == k, rk.astype(I32), ranks)
    rank_ref[...] = ranks
    cnt = cnt_s[...] + jnp.sum(member, axis=0, keepdims=True)
    cnt_s[...] = cnt
    cnt_ref[...] = cnt


def post_xattn(ox, wxo, h, norm_ffn, wr_hi, wr_lo, br_row, cnt_in, *, tm=256):
    m, d = h.shape
    tm = min(tm, m)
    row = lambda i: (i, 0)
    const = lambda i: (0, 0)
    return pl.pallas_call(
        _post_xattn_kernel,
        out_shape=[jax.ShapeDtypeStruct((m, d), F32), jax.ShapeDtypeStruct((m, d // LANES, LANES), F32),
                   jax.ShapeDtypeStruct((m, LANES), F32), jax.ShapeDtypeStruct((m, LANES), I32),
                   jax.ShapeDtypeStruct((m, LANES), I32), jax.ShapeDtypeStruct((1, LANES), F32)],
        grid=(m // tm,),
        in_specs=[pl.BlockSpec((tm, d), row), pl.BlockSpec(wxo.shape, const), pl.BlockSpec((tm, d), row),
                  pl.BlockSpec((1, d), const), pl.BlockSpec(wr_hi.shape, const), pl.BlockSpec(wr_lo.shape, const),
                  pl.BlockSpec((1, LANES), const), pl.BlockSpec((1, LANES), const)],
        out_specs=[pl.BlockSpec((tm, d), row), pl.BlockSpec((tm, d // LANES, LANES), lambda i: (i, 0, 0)),
                   pl.BlockSpec((tm, LANES), row), pl.BlockSpec((tm, LANES), row),
                   pl.BlockSpec((tm, LANES), row), pl.BlockSpec((1, LANES), const)],
        scratch_shapes=[pltpu.VMEM((1, LANES), F32)],
        compiler_params=_cparams("arbitrary"),
        name="post_xattn_router",
    )(ox, wxo, h, norm_ffn.reshape(1, -1), wr_hi, wr_lo, br_row, cnt_in)


MOE_TILE = 256


ROW_TILE = 128
MOE_FF_CHUNK = 512


def _row_dma_loop(n_tokens, make_copies, wait):
    def body(n, _):
        for k, cp in enumerate(make_copies(n)):
            cp.wait() if wait else cp.start(priority=k % 2)
        return 0
    lax.fori_loop(0, n_tokens, body, 0, unroll=4)


def _dispatch_kernel(dest_ref, x_hbm, xs_in_ref, xs_ref, sem, *, nt):
    del xs_in_ref
    t = pl.program_id(0)
    tm = ROW_TILE
    slot = t % 2

    def copies(s, tok0, dest_of):
        def make(n):
            return [pltpu.make_async_copy(x_hbm.at[tok0 + n], xs_ref.at[dest_of(n, k)], sem.at[s])
                    for k in range(TOP_K)]
        return make

    def wait_slot(s):
        _row_dma_loop(tm, copies(s, 0, lambda n, k: 0), wait=True)

    if nt > 1:
        @pl.when(t >= 1)
        def _():
            wait_slot(1 - slot)

    _row_dma_loop(tm, copies(slot, t * tm, lambda n, k: dest_ref[0, 0, n * TOP_K + k]), wait=False)

    @pl.when(t == nt - 1)
    def _():
        wait_slot(slot)


def moe_dispatch(dest, x3, xs):
    m = x3.shape[0]
    tm = ROW_TILE
    assert m % tm == 0
    n_tiles = m // tm
    return pl.pallas_call(
        functools.partial(_dispatch_kernel, nt=n_tiles),
        out_shape=jax.ShapeDtypeStruct(xs.shape, xs.dtype),
        grid=(n_tiles,),
        in_specs=[pl.BlockSpec((1, 1, TOP_K * tm), lambda t: (t, 0, 0), memory_space=pltpu.SMEM),
                  pl.BlockSpec(memory_space=pl.ANY),
                  pl.BlockSpec(memory_space=pl.ANY)],
        out_specs=pl.BlockSpec(memory_space=pl.ANY),
        scratch_shapes=[pltpu.SemaphoreType.DMA((2,))],
        input_output_aliases={2: 0},
        compiler_params=_cparams("arbitrary"),
        name="moe_dispatch",
    )(dest.reshape(n_tiles, 1, TOP_K * tm), x3, xs)


def _moe_kernel(te_ref, tv_ref, x_ref, w1_ref, b1_ref, w2_ref, b2_ref, y_ref, w1b, w2s, w2p_ref):
    t = pl.program_id(0)

    @pl.when(jnp.logical_or(t == 0, te_ref[t] != te_ref[jnp.maximum(t - 1, 0)]))
    def _():
        w1b[...] = w1_ref[0].astype(BF16)
        half = LANES // 2
        n_chunks = w2s.shape[0]
        for c in range(n_chunks):
            for blk in range(w2s.shape[1] // LANES):
                r0 = blk * LANES
                w2s[c, pl.ds(r0, half, stride=2), :] = w2_ref[0, r0:r0 + half, c * LANES:(c + 1) * LANES]
                w2s[c, pl.ds(r0 + 1, half, stride=2), :] = w2_ref[0, r0 + half:r0 + LANES, c * LANES:(c + 1) * LANES]
        w2p_ref[...] = jnp.concatenate([w2s[c] for c in range(n_chunks)], axis=1).astype(BF16)

    @pl.when(tv_ref[t] > 0)
    def _():
        tm = x_ref.shape[0]
        x = x_ref[...].reshape(tm, w1b.shape[0]).astype(BF16)
        acc = jnp.broadcast_to(b2_ref[0], (tm, w1b.shape[0]))
        ff2 = w1b.shape[1]
        even = lax.broadcasted_iota(I32, (x.shape[0], LANES), 1) % 2 == 0
        for c in range(ff2 // MOE_FF_CHUNK):
            lo = c * MOE_FF_CHUNK
            hh = jnp.dot(x, w1b[:, lo:lo + MOE_FF_CHUNK], preferred_element_type=F32) + b1_ref[0, :, lo:lo + MOE_FF_CHUNK]
            parts = []
            for j in range(MOE_FF_CHUNK // (2 * LANES)):
                a = hh[:, (2 * j) * LANES:(2 * j + 1) * LANES]
                b = hh[:, (2 * j + 1) * LANES:(2 * j + 2) * LANES]
                glu = jnp.where(even, a, pltpu.roll(b, 1, 1))
                lin = jnp.where(even, pltpu.roll(a, LANES - 1, 1), b)
                glu = jnp.minimum(glu, SWIGLU_LIMIT)
                lin = jnp.clip(lin, -SWIGLU_LIMIT, SWIGLU_LIMIT)
                parts.append((lin + 1.0) * (glu * _sigmoid(SWIGLU_ALPHA * glu)))
            act = jnp.concatenate(parts, axis=1).astype(BF16)
            acc = acc + jnp.dot(act, w2p_ref[lo // 2:(lo + MOE_FF_CHUNK) // 2, :], preferred_element_type=F32)
        y_ref[...] = acc.reshape(y_ref.shape)

    @pl.when(tv_ref[t] == 0)
    def _():
        y_ref[...] = jnp.zeros(y_ref.shape, F32)


def moe_experts(tile_expert, tile_valid, xs, w1, b1, w2, b2):
    n_tiles = tile_expert.shape[0]
    tm = MOE_TILE
    d = w1.shape[1]
    ff2 = w1.shape[2]
    assert ff2 % MOE_FF_CHUNK == 0 and MOE_FF_CHUNK % (2 * LANES) == 0 and xs.shape[1:] == (d // LANES, LANES)
    wmap = lambda t, te, tv: (te[t], 0, 0)
    row_tiles = pl.BlockSpec((tm, d // LANES, LANES), lambda t, te, tv: (t, 0, 0))
    grid_spec = pltpu.PrefetchScalarGridSpec(
        num_scalar_prefetch=2,
        grid=(n_tiles,),
        in_specs=[row_tiles,
                  pl.BlockSpec((1, d, ff2), wmap), pl.BlockSpec((1, 1, ff2), wmap),
                  pl.BlockSpec((1, ff2 // 2, d), wmap), pl.BlockSpec((1, 1, d), wmap)],
        out_specs=row_tiles,
        scratch_shapes=[pltpu.VMEM((d, ff2), BF16), pltpu.VMEM((d // LANES, ff2 // 2, LANES), F32),
                        pltpu.VMEM((ff2 // 2, d), BF16)],
    )
    return pl.pallas_call(
        _moe_kernel,
        out_shape=jax.ShapeDtypeStruct(xs.shape, F32),
        grid_spec=grid_spec,
        compiler_params=pltpu.CompilerParams(dimension_semantics=("arbitrary",), vmem_limit_bytes=MOE_VMEM_LIMIT),
        name="moe_experts",
    )(tile_expert, tile_valid, xs, w1, b1, w2, b2)


def _combine_kernel(dest_ref, destn_ref, y_hbm, h2_ref, gate_ref, gain_ref, o_ref, ybuf, sem, *, nt):
    t = pl.program_id(0)
    tm = ROW_TILE
    slot = t % 2

    def copies(s, dest_of):
        def make(n):
            return [pltpu.make_async_copy(y_hbm.at[dest_of(n, k)], ybuf.at[s, k * tm + n], sem.at[s])
                    for k in range(TOP_K)]
        return make

    @pl.when(t == 0)
    def _():
        _row_dma_loop(tm, copies(0, lambda n, k: dest_ref[0, 0, n * TOP_K + k]), wait=False)

    if nt > 1:
        @pl.when(t + 1 < nt)
        def _():
            _row_dma_loop(tm, copies(1 - slot, lambda n, k: destn_ref[0, 0, n * TOP_K + k]), wait=False)

    _row_dma_loop(tm, copies(slot, lambda n, k: 0), wait=True)
    gates = gate_ref[...]
    acc = h2_ref[...]
    for k in range(TOP_K):
        yk = ybuf[slot, k * tm:(k + 1) * tm].reshape(acc.shape)
        acc = acc + gates[:, k:k + 1] * yk
    o_ref[...] = _rms(acc, gain_ref[...])


def moe_combine(dest, y_sorted, h2, gates, norm_final):
    m, d = h2.shape
    tm = ROW_TILE
    assert m % tm == 0
    n_tiles = m // tm
    row = lambda t: (t, 0)
    dest3 = dest.reshape(n_tiles, 1, TOP_K * tm)
    return pl.pallas_call(
        functools.partial(_combine_kernel, nt=n_tiles),
        out_shape=jax.ShapeDtypeStruct((m, d), F32),
        grid=(n_tiles,),
        in_specs=[pl.BlockSpec((1, 1, TOP_K * tm), lambda t: (t, 0, 0), memory_space=pltpu.SMEM),
                  pl.BlockSpec((1, 1, TOP_K * tm), lambda t: (jnp.minimum(t + 1, n_tiles - 1), 0, 0),
                               memory_space=pltpu.SMEM),
                  pl.BlockSpec(memory_space=pl.ANY),
                  pl.BlockSpec((tm, d), row), pl.BlockSpec((tm, LANES), row),
                  pl.BlockSpec((1, d), lambda t: (0, 0))],
        out_specs=pl.BlockSpec((tm, d), row),
        scratch_shapes=[pltpu.VMEM((2, TOP_K * tm, d // LANES, LANES), F32), pltpu.SemaphoreType.DMA((2,))],
        compiler_params=_cparams("arbitrary"),
        name="moe_combine",
    )(dest3, dest3, y_sorted, h2, gates, norm_final.reshape(1, -1))


def _moe_plan(counts_row, n_assign):
    tm = MOE_TILE
    n_tiles = (n_assign + N_EXPERTS * (tm - 1) + tm - 1) // tm
    counts = counts_row[0, :N_EXPERTS].astype(I32)
    tiles_per = (counts + tm - 1) // tm
    tile_end = jnp.cumsum(tiles_per)
    pad_start = (tile_end - tiles_per) * tm
    tile_ids = jnp.arange(n_tiles, dtype=I32)
    used = tile_end[-1]
    tile_valid = (tile_ids < used).astype(I32)
    clamped = jnp.minimum(tile_ids, used - 1)
    tile_expert = jnp.sum((clamped[:, None] >= tile_end[None, :]).astype(I32), axis=1)
    return n_tiles, jnp.minimum(tile_expert, N_EXPERTS - 1), tile_valid, pad_start


def _dest_rows(ids, ranks, pad_start):
    sel = ids[:, :TOP_K, None] == jnp.arange(N_EXPERTS, dtype=I32)
    return ranks[:, :TOP_K] + jnp.sum(jnp.where(sel, pad_start, 0), axis=-1)


def _rope_tables(positions):
    half = MLA_ROPE // 2
    inv_freq = 1.0 / (ROPE_THETA ** (jnp.arange(half, dtype=F32) / half))
    ang = positions.astype(F32)[:, None] * inv_freq[None, :]
    cos, sin = jnp.cos(ang), jnp.sin(ang)
    zeros = jnp.zeros((positions.shape[0], LANES - MLA_ROPE), F32)
    return (jnp.concatenate([cos, cos, zeros], axis=1), jnp.concatenate([-sin, sin, zeros], axis=1))


def _in_proj_weight(w_in):
    o = np.cumsum([0, CONV_CH, GDN_HEADS * GDN_D, GDN_HEADS, GDN_HEADS, MLA_Q_RANK, MLA_KV_RANK, MLA_ROPE])
    conv, z, bl, al, cq, ckv, kpe = (w_in[:, o[i]:o[i + 1]] for i in range(7))
    pad = jnp.zeros((w_in.shape[0], LANES - MLA_ROPE - 2 * GDN_HEADS), w_in.dtype)
    return jnp.concatenate([conv, z, cq, ckv, kpe, bl, al, pad], axis=1).astype(BF16)


IN_SPLITS = (CONV_CH, GDN_HEADS * GDN_D, MLA_Q_RANK, MLA_KV_RANK, LANES)


def _lane_row(vals, lane0):
    return jnp.zeros((1, LANES), F32).at[0, lane0:lane0 + vals.shape[0]].set(vals.astype(F32))


def kernel(x_prompt, x_sample, state_gdn, state_conv, cache_latent, cache_krope, page_table, cache_mem_k, cache_mem_v, mem_prompt, norm_mix, w_in, conv_w, a_log, dt_bias, gdn_norm, q_norm, w_uq, kv_norm, w_uk, w_uv, w_out, norm_x, mem_norm, w_xq, w_xk, w_xv, w_xo, norm_ffn, w_router, b_router, w_e1, b_e1, w_e2, b_e2, norm_final):
    depth = w_in.shape[0]
    assert depth == 1
    L = 0
    bp, t, d = x_prompt.shape
    bs = x_sample.shape[0]
    assert x_sample.shape[1] == 1
    past_len = page_table.shape[1] * cache_latent.shape[2]
    mp = bp * t

    w_in_p = _in_proj_weight(w_in[L])
    alog_row = _lane_row(a_log[L], DECAY_LANE)
    dtb_row = _lane_row(dt_bias[L], DECAY_LANE)
    wuq = w_uq[L]
    wuq_p = jnp.concatenate(
        [wuq, jnp.zeros(wuq.shape[:2] + (MLA_QK_PAD - wuq.shape[2],), wuq.dtype)], axis=2
    ).reshape(wuq.shape[0], MLA_HEADS * MLA_QK_PAD).astype(BF16)
    wuk = w_uk[L].reshape(MLA_KV_RANK, MLA_HEADS * MLA_NOPE).astype(BF16)
    wuv = w_uv[L].reshape(MLA_KV_RANK, MLA_HEADS * MLA_V).astype(BF16)
    wuk_t = jnp.transpose(w_uk[L], (1, 2, 0)).astype(BF16)
    w_out_b = w_out[L].astype(BF16)
    n_gdn = GDN_HEADS * GDN_D
    w_xq_b, w_xo_b = w_xq[L].astype(BF16), w_xo[L].astype(BF16)
    w_xkv_b = jnp.concatenate([w_xk[L], w_xv[L]], axis=1).astype(BF16)
    wr = jnp.concatenate([w_router[L], jnp.zeros((d, LANES - N_EXPERTS), F32)], axis=1)
    wr_hi = wr.astype(BF16)
    wr_lo = (wr - wr_hi.astype(F32)).astype(BF16)
    br_row = _lane_row(b_router[L], 0)
    b1 = b_e1[L][:, None, :]
    b2 = b_e2[L][:, None, :]
    cos_p, sin_p = _rope_tables(jnp.arange(t, dtype=I32))
    cos_s, sin_s = _rope_tables(jnp.full((bs,), past_len, I32))

    xp = x_prompt.reshape(mp, d)
    xs = x_sample.reshape(bs, d)
    conv_p, z_p, cq_p, ckv_p, kba_p = fused_linear([xp], [w_in_p], gain=norm_mix[L], splits=IN_SPLITS, name="in_proj_prompt")
    conv_s, z_s, cq_s, ckv_s, kba_s = fused_linear([xs], [w_in_p], gain=norm_mix[L], splits=IN_SPLITS, name="in_proj_sample")

    y_gdn_p, gdn_state_p = gdn_prompt(conv_p.reshape(bp, t, CONV_CH), z_p.reshape(bp, t, n_gdn),
                                      kba_p.reshape(bp, t, LANES), conv_w[L], alog_row, dtb_row, gdn_norm[L])
    conv_state_p = conv_p.reshape(bp, t, CONV_CH)[:, t - (CONV_WIDTH - 1):, :]
    gdn_state_s, conv_state_s, y_gdn_s = gdn_sample(conv_s, state_conv[L], kba_s, z_s, state_gdn[L],
                                                    conv_w[L], alog_row, dtb_row, gdn_norm[L])

    q_p, k_p, v_p, lat_p, kpe_p = mla_prep(cq_p, ckv_p, kba_p, cos_p, sin_p, q_norm[L], kv_norm[L],
                                           wuq_p, wuk, wuv, seq=t, q_dtype=BF16)
    y_mla_p = mla_flash(q_p.reshape(bp, t, -1), k_p.reshape(bp, t, -1), v_p.reshape(bp, t, -1))
    q_s, _, _, lat_s, kpe_s = mla_prep(cq_s, ckv_s, kba_s, cos_s, sin_s, q_norm[L], kv_norm[L],
                                       wuq_p, wuk, wuv, seq=bs, q_dtype=F32)
    q_s4 = q_s.reshape(bs, MLA_HEADS, MLA_QK_PAD)
    qlat = jnp.concatenate(
        [fused_linear([q_s4[:, h, :MLA_NOPE]], [wuk_t[h]], name=f"absorb_q{h}")[0][:, None, :]
         for h in range(MLA_HEADS)], axis=1)
    head_pad = ((0, 0), (0, DEC_HEAD_PAD - MLA_HEADS), (0, 0))
    qlat8 = jnp.pad(qlat, head_pad)
    qpe8 = jnp.pad(q_s4[:, :, MLA_NOPE:MLA_NOPE + MLA_ROPE], head_pad)
    y_mla_s = mla_decode(page_table, qlat8, qpe8, lat_s, kpe_s, wuv,
                         cache_latent, jnp.swapaxes(cache_krope, 2, 3)).reshape(bs, -1)

    (h_p,) = fused_linear([y_gdn_p.reshape(mp, n_gdn), y_mla_p.reshape(mp, -1)], [w_out_b[:n_gdn], w_out_b[n_gdn:]],
                          residual=xp, name="out_proj_prompt")
    (h_s,) = fused_linear([y_gdn_s, y_mla_s], [w_out_b[:n_gdn], w_out_b[n_gdn:]], residual=xs, name="out_proj_sample")

    mem_tokens = mem_prompt.shape[1]
    mk_f, mv_f, mk_b, mv_b = fused_linear([mem_prompt.reshape(bp * mem_tokens, d)], [w_xkv_b], gain=mem_norm[L],
                                          splits=(d, d), bf16_copies=True, name="memory_kv")
    (qx_p,) = fused_linear([h_p], [w_xq_b], gain=norm_x[L], out_dtypes=(BF16,), name="xq_prompt")
    (qx_s,) = fused_linear([h_s], [w_xq_b], gain=norm_x[L], out_dtypes=(BF16,), name="xq_sample")
    ox_p = xattn_prompt(qx_p.reshape(bp, t, d), mk_b.reshape(bp, mem_tokens, d), mv_b.reshape(bp, mem_tokens, d))
    ox_s = xattn_sample(qx_s.reshape(bs, X_HEADS, d // X_HEADS), cache_mem_k[L], cache_mem_v[L])

    h2_p, hn_p, gate_p, ids_p, rank_p, cnt_p = post_xattn(ox_p.reshape(mp, d), w_xo_b, h_p, norm_ffn[L],
                                                          wr_hi, wr_lo, br_row, jnp.zeros((1, LANES), F32))
    h2_s, hn_s, gate_s, ids_s, rank_s, cnt_all = post_xattn(ox_s.reshape(bs, d), w_xo_b, h_s, norm_ffn[L],
                                                            wr_hi, wr_lo, br_row, cnt_p)

    n_tiles, tile_expert, tile_valid, pad_start = _moe_plan(cnt_all, TOP_K * (mp + bs))
    dest_p = _dest_rows(ids_p, rank_p, pad_start)
    dest_s = _dest_rows(ids_s, rank_s, pad_start)
    x_grouped = jnp.zeros((n_tiles * MOE_TILE, d // LANES, LANES), F32)
    x_grouped = moe_dispatch(dest_p, hn_p, x_grouped)
    x_grouped = moe_dispatch(dest_s, hn_s, x_grouped)
    y_sorted = moe_experts(tile_expert, tile_valid, x_grouped, w_e1[L], b1, w_e2[L], b2)
    y_prompt = moe_combine(dest_p, y_sorted, h2_p, gate_p, norm_final)
    y_sample = moe_combine(dest_s, y_sorted, h2_s, gate_s, norm_final)

    x_heads = X_HEADS
    return (y_prompt.reshape(bp, t, d), y_sample.reshape(bs, 1, d),
            gdn_state_p[None], conv_state_p[None],
            lat_p.reshape(bp, 1, t, MLA_KV_RANK), kpe_p.reshape(bp, 1, t, MLA_ROPE),
            mk_f.reshape(1, bp, mem_tokens, x_heads, d // x_heads), mv_f.reshape(1, bp, mem_tokens, x_heads, d // x_heads),
            gdn_state_s[None], conv_state_s[None],
            lat_s.reshape(bs, 1, 1, MLA_KV_RANK), kpe_s.reshape(bs, 1, 1, MLA_ROPE))
```

```python
import functools

import jax
import jax.numpy as jnp
import numpy as np
from jax import lax
from jax.experimental import pallas as pl
from jax.experimental.pallas import tpu as pltpu

F32 = jnp.float32
BF16 = jnp.bfloat16
I32 = jnp.int32

NORM_EPS = 1e-6
LANES = 128
SUBLANES = 8
VMEM_LIMIT = 48 * 1024 * 1024
MOE_VMEM_LIMIT = 56 * 1024 * 1024

GDN_HEADS = 4
GDN_D = 128
CONV_WIDTH = 4
CONV_CH = 3 * GDN_HEADS * GDN_D
GDN_BLOCK = 128
MLA_HEADS = 4
MLA_NOPE = 128
MLA_ROPE = 64
MLA_V = 128
MLA_Q_RANK = 384
MLA_KV_RANK = 256
MLA_QK_PAD = 256
ROPE_THETA = 10000.0
MLA_SCALE = (MLA_NOPE + MLA_ROPE) ** -0.5
X_HEADS = 4
N_EXPERTS = 32
TOP_K = 4
SWIGLU_LIMIT = 7.0
SWIGLU_ALPHA = 1.702
BETA_LANE = MLA_ROPE
DECAY_LANE = MLA_ROPE + GDN_HEADS
NEG_BIG = -1e30


def _cparams(*sem):
    return pltpu.CompilerParams(dimension_semantics=sem, vmem_limit_bytes=VMEM_LIMIT)


def _rms(x, gain):
    return x * lax.rsqrt(jnp.mean(x * x, axis=-1, keepdims=True) + NORM_EPS) * gain


def _mm(a, b):
    return jnp.dot(a.astype(BF16), b.astype(BF16), preferred_element_type=F32)


def _mm_nt(a, b):
    return lax.dot_general(a.astype(BF16), b.astype(BF16), (((1,), (1,)), ((), ())),
                           preferred_element_type=F32)


def _mm3(a, b):
    a_hi = a.astype(BF16)
    b_hi = b.astype(BF16)
    a_lo = (a - a_hi.astype(F32)).astype(BF16)
    b_lo = (b - b_hi.astype(F32)).astype(BF16)
    return jnp.dot(jnp.concatenate([a_hi, a_lo, a_hi], axis=1), jnp.concatenate([b_hi, b_hi, b_lo], axis=0),
                   preferred_element_type=F32)


def _bmm(a, b):
    return lax.dot_general(a.astype(BF16), b.astype(BF16), (((2,), (1,)), ((0,), (0,))), preferred_element_type=F32)


def _bmm_nt(a, b):
    return lax.dot_general(a.astype(BF16), b.astype(BF16), (((2,), (2,)), ((0,), (0,))), preferred_element_type=F32)


def _bmm3(a, b):
    a_hi = a.astype(BF16)
    b_hi = b.astype(BF16)
    a_lo = (a - a_hi.astype(F32)).astype(BF16)
    b_lo = (b - b_hi.astype(F32)).astype(BF16)
    return lax.dot_general(jnp.concatenate([a_hi, a_lo, a_hi], axis=2), jnp.concatenate([b_hi, b_hi, b_lo], axis=1),
                           (((2,), (1,)), ((0,), (0,))), preferred_element_type=F32)


def _sigmoid(x):
    return 1.0 / (1.0 + jnp.exp(-x))


def _softplus(x):
    return jnp.maximum(x, 0.0) + jnp.log1p(jnp.exp(-jnp.abs(x)))


def _linear_kernel(*refs, n_in, has_gain, has_res, splits):
    a_refs = refs[:n_in]
    w_refs = refs[n_in:2 * n_in]
    pos = 2 * n_in
    g_ref = refs[pos] if has_gain else None
    pos += int(has_gain)
    r_ref = refs[pos] if has_res else None
    pos += int(has_res)
    out_refs = refs[pos:]
    a0 = a_refs[0][...]
    if has_gain:
        a0 = _rms(a0.astype(F32), g_ref[...])
    acts = [a0.astype(BF16)] + [a[...].astype(BF16) for a in a_refs[1:]]
    off = 0
    for i, width in enumerate(splits):
        acc = None
        for a, w in zip(acts, w_refs):
            d = jnp.dot(a, w[:, off:off + width], preferred_element_type=F32)
            acc = d if acc is None else acc + d
        if has_res:
            acc = acc + r_ref[:, off:off + width]
        for o_ref in out_refs[i::len(splits)]:
            o_ref[...] = acc.astype(o_ref.dtype)
        off += width


def fused_linear(acts, weights, *, gain=None, residual=None, splits=None, out_dtypes=None,
                 bf16_copies=False, tm=256, name="fused_linear"):
    m = acts[0].shape[0]
    n = weights[0].shape[1]
    tm = min(tm, m)
    assert m % tm == 0
    splits = tuple(splits) if splits is not None else (n,)
    assert sum(splits) == n and all(s % LANES == 0 for s in splits)
    out_dtypes = tuple(out_dtypes) if out_dtypes is not None else (F32,) * len(splits)
    out_widths = splits
    if bf16_copies:
        out_widths = splits + splits
        out_dtypes = out_dtypes + (BF16,) * len(splits)
    in_specs = [pl.BlockSpec((tm, a.shape[1]), lambda i: (i, 0)) for a in acts]
    in_specs += [pl.BlockSpec(w.shape, lambda i: (0, 0)) for w in weights]
    args = list(acts) + list(weights)
    if gain is not None:
        in_specs.append(pl.BlockSpec((1, gain.shape[-1]), lambda i: (0, 0)))
        args.append(gain.reshape(1, -1))
    if residual is not None:
        in_specs.append(pl.BlockSpec((tm, n), lambda i: (i, 0)))
        args.append(residual)
    outs = pl.pallas_call(
        functools.partial(_linear_kernel, n_in=len(acts), has_gain=gain is not None,
                          has_res=residual is not None, splits=splits),
        out_shape=[jax.ShapeDtypeStruct((m, s), dt) for s, dt in zip(out_widths, out_dtypes)],
        grid=(m // tm,),
        in_specs=in_specs,
        out_specs=[pl.BlockSpec((tm, s), lambda i: (i, 0)) for s in out_widths],
        compiler_params=_cparams("parallel"),
        name=name,
    )(*args)
    return outs


def _gate_values(kba, alog_row, dtb_row):
    beta = _sigmoid(kba)
    g = -jnp.exp(alog_row) * _softplus(kba + dtb_row)
    return beta, g


def _l2norm(x):
    return x * lax.rsqrt(jnp.sum(x * x, axis=-1, keepdims=True) + NORM_EPS)


GDN_PROMPT_BB = 2


def _gdn_prompt_kernel(x_ref, z_ref, kba_ref, cw_ref, alog_ref, dtb_ref, gn_ref,
                       y_ref, s_out_ref, xbuf, state):
    t = pl.program_id(1)
    nt = pl.num_programs(1)

    @pl.when(t == 0)
    def _():
        xbuf[:, 0:SUBLANES, :] = jnp.zeros((GDN_PROMPT_BB, SUBLANES, CONV_CH), F32)
        state[...] = jnp.zeros(state.shape, F32)

    blk = GDN_BLOCK
    nh = GDN_HEADS * GDN_D
    row = lax.broadcasted_iota(I32, (blk, blk), 0)
    col = lax.broadcasted_iota(I32, (blk, blk), 1)
    qs, ks, vs, betas, gcols, grows = [], [], [], [], [], []
    for bi in range(GDN_PROMPT_BB):
        c, beta_all, gc = _gdn_conv_and_gates(x_ref.at[bi], kba_ref.at[bi], cw_ref, alog_ref, dtb_ref, xbuf.at[bi], row)
        gc_t = gc.T
        for h in range(GDN_HEADS):
            qs.append(c[:, h * GDN_D:(h + 1) * GDN_D])
            ks.append(c[:, nh + h * GDN_D:nh + (h + 1) * GDN_D])
            vs.append(c[:, 2 * nh + h * GDN_D:2 * nh + (h + 1) * GDN_D])
            betas.append(beta_all[:, BETA_LANE + h:BETA_LANE + h + 1])
            gcols.append(gc[:, DECAY_LANE + h:DECAY_LANE + h + 1])
            grows.append(gc_t[DECAY_LANE + h:DECAY_LANE + h + 1, :])

    q = _l2norm(jnp.stack(qs)) * (GDN_D ** -0.5)
    k = _l2norm(jnp.stack(ks))
    v = jnp.stack(vs)
    bcol = jnp.stack(betas)
    gcol = jnp.stack(gcols)
    grow = jnp.stack(grows)
    decay = jnp.exp(jnp.where((row >= col)[None], gcol - grow, NEG_BIG))
    kb = k * bcol
    vb = v * bcol
    a = jnp.where((row > col)[None], _bmm_nt(kb, k) * decay, 0.0)
    x = (row == col).astype(F32)[None] - a
    p = _bmm3(a, a)
    x = x + _bmm3(x, p)
    for _ in range(5):
        p = _bmm3(p, p)
        x = x + _bmm3(x, p)
    egc = jnp.exp(gcol)
    u = _bmm(x, vb)
    w = _bmm(x, kb * egc)
    intra = _bmm_nt(q, k) * decay
    s_all = state[...].reshape(GDN_PROMPT_BB * GDN_HEADS, GDN_D, GDN_D)
    v_new = u - _bmm(w, s_all)
    o = _bmm(q * egc, s_all) + _bmm(intra, v_new)
    g_last = gcol[:, blk - 1:blk, :]
    kd = k * jnp.exp(g_last - gcol)
    s_new = s_all * jnp.exp(g_last) + _bmm(jnp.swapaxes(kd, 1, 2), v_new)
    state[...] = s_new.reshape(state.shape)
    o = _rms(o, gn_ref[...])
    for bi in range(GDN_PROMPT_BB):
        z = z_ref[bi]
        for h in range(GDN_HEADS):
            zz = z[:, h * GDN_D:(h + 1) * GDN_D]
            y_ref[bi, :, h * GDN_D:(h + 1) * GDN_D] = (o[bi * GDN_HEADS + h] * (zz * _sigmoid(zz))).astype(y_ref.dtype)

    @pl.when(t == nt - 1)
    def _():
        s_out_ref[...] = state[...]


def _gdn_conv_and_gates(x_ref, kba_ref, cw_ref, alog_ref, dtb_ref, xbuf, row):
    blk = GDN_BLOCK
    hist = CONV_WIDTH - 1
    xbuf[SUBLANES:SUBLANES + blk, :] = x_ref[...]
    cw = cw_ref[...]
    conv = xbuf[SUBLANES - hist:SUBLANES - hist + blk, :] * cw[0:1, :]
    for j in range(1, CONV_WIDTH):
        conv = conv + xbuf[SUBLANES - hist + j:SUBLANES - hist + j + blk, :] * cw[j:j + 1, :]
    xbuf[SUBLANES - hist:SUBLANES, :] = xbuf[SUBLANES + blk - hist:SUBLANES + blk, :]
    c = conv * _sigmoid(conv)
    beta_all, g_all = _gate_values(kba_ref[...], alog_ref[...], dtb_ref[...])
    gc = g_all
    shift = 1
    while shift < blk:
        gc = gc + jnp.where(row >= shift, pltpu.roll(gc, shift, 0), 0.0)
        shift *= 2
    return c, beta_all, gc


def gdn_prompt(conv_in, z, kba, conv_w, alog_row, dtb_row, gdn_norm):
    b, t, _ = conv_in.shape
    bb = GDN_PROMPT_BB
    assert t % GDN_BLOCK == 0 and b % bb == 0
    nt = t // GDN_BLOCK
    y, s = pl.pallas_call(
        _gdn_prompt_kernel,
        out_shape=[jax.ShapeDtypeStruct((b, t, GDN_HEADS * GDN_D), BF16),
                   jax.ShapeDtypeStruct((b, GDN_HEADS, GDN_D, GDN_D), F32)],
        grid=(b // bb, nt),
        in_specs=[pl.BlockSpec((bb, GDN_BLOCK, CONV_CH), lambda i, j: (i, j, 0)),
                  pl.BlockSpec((bb, GDN_BLOCK, GDN_HEADS * GDN_D), lambda i, j: (i, j, 0)),
                  pl.BlockSpec((bb, GDN_BLOCK, LANES), lambda i, j: (i, j, 0)),
                  pl.BlockSpec((CONV_WIDTH, CONV_CH), lambda i, j: (0, 0)),
                  pl.BlockSpec((1, LANES), lambda i, j: (0, 0)),
                  pl.BlockSpec((1, LANES), lambda i, j: (0, 0)),
                  pl.BlockSpec((1, GDN_D), lambda i, j: (0, 0))],
        out_specs=[pl.BlockSpec((bb, GDN_BLOCK, GDN_HEADS * GDN_D), lambda i, j: (i, j, 0)),
                   pl.BlockSpec((bb, GDN_HEADS, GDN_D, GDN_D), lambda i, j: (i, 0, 0, 0))],
        scratch_shapes=[pltpu.VMEM((bb, SUBLANES + GDN_BLOCK, CONV_CH), F32),
                        pltpu.VMEM((bb, GDN_HEADS, GDN_D, GDN_D), F32)],
        compiler_params=_cparams("parallel", "arbitrary"),
        name="gdn_prompt",
    )(conv_in, z, kba, conv_w, alog_row, dtb_row, gdn_norm.reshape(1, -1))
    return y, s


GDN_SAMPLE_BB = 8


def _gdn_sample_kernel(x_ref, sc_ref, kba_ref, z_ref, s_ref, cw_ref, alog_ref, dtb_ref, gn_ref,
                       s_out_ref, sc_out_ref, y_ref, tbuf):
    bb = GDN_SAMPLE_BB
    x = x_ref[...]
    cw = cw_ref[...]
    conv = x * cw[CONV_WIDTH - 1:CONV_WIDTH, :]
    for j in range(CONV_WIDTH - 1):
        conv = conv + sc_ref[:, j, :] * cw[j:j + 1, :]
    for j in range(CONV_WIDTH - 2):
        sc_out_ref[:, j, :] = sc_ref[:, j + 1, :]
    sc_out_ref[:, CONV_WIDTH - 2, :] = x
    c = conv * _sigmoid(conv)
    beta_all, g_all = _gate_values(kba_ref[...], alog_ref[...], dtb_ref[...])
    eg_all = jnp.exp(g_all)
    z = z_ref[...]
    gn = gn_ref[...]
    nh = GDN_HEADS * GDN_D
    tbuf[...] = jnp.zeros(tbuf.shape, F32)
    for h in range(GDN_HEADS):
        q = _l2norm(c[:, h * GDN_D:(h + 1) * GDN_D]) * (GDN_D ** -0.5)
        k = _l2norm(c[:, nh + h * GDN_D:nh + (h + 1) * GDN_D])
        v = c[:, 2 * nh + h * GDN_D:2 * nh + (h + 1) * GDN_D]
        tbuf[0:bb, :] = q
        q_t = tbuf[...].T
        tbuf[0:bb, :] = k
        k_t = tbuf[...].T
        for b in range(bb):
            qcol = q_t[:, b:b + 1]
            kcol = k_t[:, b:b + 1]
            eg = eg_all[b:b + 1, DECAY_LANE + h:DECAY_LANE + h + 1]
            beta = beta_all[b:b + 1, BETA_LANE + h:BETA_LANE + h + 1]
            s1 = s_ref[b, h] * eg
            pred = jnp.sum(s1 * kcol, axis=0, keepdims=True)
            u = (v[b:b + 1, :] - pred) * beta
            s2 = s1 + kcol * u
            s_out_ref[b, h] = s2
            o = jnp.sum(s2 * qcol, axis=0, keepdims=True)
            zz = z[b:b + 1, h * GDN_D:(h + 1) * GDN_D]
            y_ref[b:b + 1, h * GDN_D:(h + 1) * GDN_D] = (_rms(o, gn) * (zz * _sigmoid(zz))).astype(y_ref.dtype)


def gdn_sample(conv_in, state_conv, kba, z, state_gdn, conv_w, alog_row, dtb_row, gdn_norm):
    b = conv_in.shape[0]
    bb = GDN_SAMPLE_BB
    assert b % bb == 0
    hist = CONV_WIDTH - 1
    return pl.pallas_call(
        _gdn_sample_kernel,
        out_shape=[jax.ShapeDtypeStruct(state_gdn.shape, F32),
                   jax.ShapeDtypeStruct(state_conv.shape, F32),
                   jax.ShapeDtypeStruct((b, GDN_HEADS * GDN_D), BF16)],
        grid=(b // bb,),
        in_specs=[pl.BlockSpec((bb, CONV_CH), lambda i: (i, 0)),
                  pl.BlockSpec((bb, hist, CONV_CH), lambda i: (i, 0, 0)),
                  pl.BlockSpec((bb, LANES), lambda i: (i, 0)),
                  pl.BlockSpec((bb, GDN_HEADS * GDN_D), lambda i: (i, 0)),
                  pl.BlockSpec((bb, GDN_HEADS, GDN_D, GDN_D), lambda i: (i, 0, 0, 0)),
                  pl.BlockSpec((CONV_WIDTH, CONV_CH), lambda i: (0, 0)),
                  pl.BlockSpec((1, LANES), lambda i: (0, 0)),
                  pl.BlockSpec((1, LANES), lambda i: (0, 0)),
                  pl.BlockSpec((1, GDN_D), lambda i: (0, 0))],
        out_specs=[pl.BlockSpec((bb, GDN_HEADS, GDN_D, GDN_D), lambda i: (i, 0, 0, 0)),
                   pl.BlockSpec((bb, hist, CONV_CH), lambda i: (i, 0, 0)),
                   pl.BlockSpec((bb, GDN_HEADS * GDN_D), lambda i: (i, 0))],
        scratch_shapes=[pltpu.VMEM((GDN_D, GDN_D), F32)],
        compiler_params=_cparams("parallel"),
        name="gdn_sample",
    )(conv_in, state_conv, kba, z, state_gdn, conv_w, alog_row, dtb_row, gdn_norm.reshape(1, -1))


def _rope128(x, cos, sin):
    half = MLA_ROPE // 2
    lane = lax.broadcasted_iota(I32, x.shape, 1)
    swapped = jnp.where(lane < half, pltpu.roll(x, LANES - half, 1), pltpu.roll(x, half, 1))
    return x * cos + swapped * sin


def _mla_prep_kernel(cq_ref, ckv_ref, kba_ref, cos_ref, sin_ref, qn_ref, kvn_ref, wuq_ref, wuk_ref, wuv_ref,
                     q_ref, k_ref, v_ref, lat_ref, kpe_ref):
    cos = cos_ref[...]
    sin = sin_ref[...]
    qn = _rms(cq_ref[...], qn_ref[...]).astype(BF16)
    lat = _rms(ckv_ref[...], kvn_ref[...])
    lat_ref[...] = lat
    lat_b = lat.astype(BF16)
    kpe = _rope128(kba_ref[...], cos, sin)
    kpe_ref[...] = kpe[:, :MLA_ROPE]
    for h in range(MLA_HEADS):
        lo = h * MLA_QK_PAD
        q_ref[:, lo:lo + MLA_NOPE] = jnp.dot(
            qn, wuq_ref[:, lo:lo + MLA_NOPE], preferred_element_type=F32).astype(q_ref.dtype)
        q_pe = jnp.dot(qn, wuq_ref[:, lo + MLA_NOPE:lo + MLA_QK_PAD], preferred_element_type=F32)
        q_ref[:, lo + MLA_NOPE:lo + MLA_QK_PAD] = _rope128(q_pe, cos, sin).astype(q_ref.dtype)
        k_ref[:, lo:lo + MLA_NOPE] = jnp.dot(
            lat_b, wuk_ref[:, h * MLA_NOPE:(h + 1) * MLA_NOPE], preferred_element_type=F32).astype(k_ref.dtype)
        k_ref[:, lo + MLA_NOPE:lo + MLA_QK_PAD] = kpe.astype(k_ref.dtype)
    v_ref[...] = jnp.dot(lat_b, wuv_ref[...], preferred_element_type=F32).astype(v_ref.dtype)


def mla_prep(c_q, c_kv, kba, cos_tab, sin_tab, q_norm, kv_norm, wuq_p, wuk, wuv, *, seq, q_dtype, tm=256):
    m = c_q.shape[0]
    tm = min(tm, m, seq)
    assert m % tm == 0 and seq % tm == 0
    nseq = seq // tm
    hq = MLA_HEADS * MLA_QK_PAD
    row = lambda i: (i, 0)
    const = lambda i: (0, 0)
    return pl.pallas_call(
        _mla_prep_kernel,
        out_shape=[jax.ShapeDtypeStruct((m, hq), q_dtype),
                   jax.ShapeDtypeStruct((m, hq), BF16),
                   jax.ShapeDtypeStruct((m, MLA_HEADS * MLA_V), BF16),
                   jax.ShapeDtypeStruct((m, MLA_KV_RANK), F32),
                   jax.ShapeDtypeStruct((m, MLA_ROPE), F32)],
        grid=(m // tm,),
        in_specs=[pl.BlockSpec((tm, MLA_Q_RANK), row),
                  pl.BlockSpec((tm, MLA_KV_RANK), row),
                  pl.BlockSpec((tm, LANES), row),
                  pl.BlockSpec((tm, LANES), lambda i: (i % nseq, 0)),
                  pl.BlockSpec((tm, LANES), lambda i: (i % nseq, 0)),
                  pl.BlockSpec((1, MLA_Q_RANK), const),
                  pl.BlockSpec((1, MLA_KV_RANK), const),
                  pl.BlockSpec(wuq_p.shape, const),
                  pl.BlockSpec(wuk.shape, const),
                  pl.BlockSpec(wuv.shape, const)],
        out_specs=[pl.BlockSpec((tm, hq), row),
                   pl.BlockSpec((tm, hq), row),
                   pl.BlockSpec((tm, MLA_HEADS * MLA_V), row),
                   pl.BlockSpec((tm, MLA_KV_RANK), row),
                   pl.BlockSpec((tm, MLA_ROPE), row)],
        compiler_params=_cparams("parallel"),
        name="mla_prep",
    )(c_q, c_kv, kba, cos_tab, sin_tab, q_norm.reshape(1, -1), kv_norm.reshape(1, -1), wuq_p, wuk, wuv)


def _flash_kernel(q_ref, k_ref, v_ref, o_ref, m_s, l_s, acc_s, *, tq):
    qi = pl.program_id(2)
    q = q_ref[0]
    m_s[...] = jnp.full(m_s.shape, NEG_BIG, F32)
    l_s[...] = jnp.zeros(l_s.shape, F32)
    acc_s[...] = jnp.zeros(acc_s.shape, F32)

    def block(j, masked):
        k0 = pl.multiple_of(j * tq, tq)
        s = lax.dot_general(q, k_ref[0, pl.ds(k0, tq), :], (((1,), (1,)), ((), ())),
                            preferred_element_type=F32) * MLA_SCALE
        if masked:
            s = jnp.where(lax.broadcasted_iota(I32, (tq, tq), 1) <= lax.broadcasted_iota(I32, (tq, tq), 0),
                          s, NEG_BIG)
        m_prev = m_s[...]
        m_new = jnp.maximum(m_prev, jnp.max(s, axis=-1, keepdims=True))
        corr = jnp.exp(m_prev - m_new)
        p = jnp.exp(s - m_new)
        l_s[...] = l_s[...] * corr + jnp.sum(p, axis=-1, keepdims=True)
        acc_s[...] = acc_s[...] * corr + jnp.dot(p.astype(BF16), v_ref[0, pl.ds(k0, tq), :],
                                                 preferred_element_type=F32)
        m_s[...] = m_new

    def below_diagonal(j, _):
        block(j, masked=False)
        return 0

    lax.fori_loop(0, qi, below_diagonal, 0)
    block(qi, masked=True)
    o_ref[0] = (acc_s[...] / l_s[...]).astype(o_ref.dtype)


def mla_flash(q, k, v, *, tq=512):
    b, t, _ = q.shape
    tq = min(tq, t)
    assert t % tq == 0
    seq_map = lambda bi, h, qi: (bi, 0, h)
    return pl.pallas_call(
        functools.partial(_flash_kernel, tq=tq),
        out_shape=jax.ShapeDtypeStruct((b, t, MLA_HEADS * MLA_V), BF16),
        grid=(b, MLA_HEADS, t // tq),
        in_specs=[pl.BlockSpec((1, tq, MLA_QK_PAD), lambda bi, h, qi: (bi, qi, h)),
                  pl.BlockSpec((1, t, MLA_QK_PAD), seq_map),
                  pl.BlockSpec((1, t, MLA_V), seq_map)],
        out_specs=pl.BlockSpec((1, tq, MLA_V), lambda bi, h, qi: (bi, qi, h)),
        scratch_shapes=[pltpu.VMEM((tq, 1), F32), pltpu.VMEM((tq, 1), F32), pltpu.VMEM((tq, MLA_V), F32)],
        compiler_params=_cparams("parallel", "parallel", "arbitrary"),
        name="mla_flash",
    )(q, k, v)


DEC_HEAD_PAD = 8
DEC_GROUP = 16
DEC_SLOTS = 4


def _decode_kernel(pt_ref, ptn_ref, qlat_ref, qpe_ref, latn_ref, kpen_ref, wuv_ref, lat_hbm, kpe_hbm,
                   o_ref, latbuf, kpebuf, sem, *, nb, n_pages, page):
    b = pl.program_id(0)
    grp = DEC_GROUP
    ns = DEC_SLOTS
    n_groups = n_pages // grp
    n_outer = n_groups // ns

    def page_copies(pg, slot, j):
        return (pltpu.make_async_copy(lat_hbm.at[pg, 0], latbuf.at[slot, j], sem.at[0, slot]),
                pltpu.make_async_copy(kpe_hbm.at[pg, 0], kpebuf.at[slot, j], sem.at[1, slot]))

    def start_group(tbl_ref, g, slot):
        for j in range(grp):
            for cp in page_copies(tbl_ref[0, 0, g * grp + j], slot, j):
                cp.start()

    def wait_group(slot):
        for j in range(grp):
            for cp in page_copies(0, slot, j):
                cp.wait()

    @pl.when(b == 0)
    def _():
        for s in range(ns - 1):
            start_group(pt_ref, s, s)

    qlat = qlat_ref[0]
    qpe = qpe_ref[0]
    latn = latn_ref[0]
    kpen = kpen_ref[0]
    qlat_b = qlat.astype(BF16)
    qpe_b = qpe.astype(BF16)
    s_new = (jnp.sum(qlat * latn, axis=-1, keepdims=True)
             + jnp.sum(qpe * kpen, axis=-1, keepdims=True)) * MLA_SCALE
    m0 = s_new
    l0 = jnp.ones_like(s_new)
    acc0 = jnp.broadcast_to(latn, qlat.shape)

    def consume(slot, carry):
        m, l, acc = carry
        lat = latbuf[slot].reshape(grp * page, MLA_KV_RANK).astype(BF16)
        kpe_t = jnp.concatenate([kpebuf[slot, j] for j in range(grp)], axis=1).astype(BF16)
        s = (lax.dot_general(qlat_b, lat, (((1,), (1,)), ((), ())), preferred_element_type=F32)
             + jnp.dot(qpe_b, kpe_t, preferred_element_type=F32)) * MLA_SCALE
        m_new = jnp.maximum(m, jnp.max(s, axis=-1, keepdims=True))
        corr = jnp.exp(m - m_new)
        p = jnp.exp(s - m_new)
        l = l * corr + jnp.sum(p, axis=-1, keepdims=True)
        acc = acc * corr + jnp.dot(p.astype(BF16), lat, preferred_element_type=F32)
        return m_new, l, acc

    def ring_body(i, carry):
        for j in range(ns):
            ahead = (j + ns - 1) % ns
            if j == 0:
                start_group(pt_ref, i * ns + ns - 1, ahead)
            else:
                @pl.when(i + 1 < n_outer)
                def _():
                    start_group(pt_ref, (i + 1) * ns + ahead, ahead)

                @pl.when(jnp.logical_and(i + 1 >= n_outer, b + 1 < nb))
                def _():
                    start_group(ptn_ref, ahead, ahead)
            wait_group(j)
            carry = consume(j, carry)
        return carry

    m, l, acc = lax.fori_loop(0, n_outer, ring_body, (m0, l0, acc0))
    o_lat = (acc / l).astype(BF16)
    res = jnp.dot(o_lat, wuv_ref[...], preferred_element_type=F32)
    o_ref[0] = jnp.concatenate(
        [res[h:h + 1, h * MLA_V:(h + 1) * MLA_V] for h in range(MLA_HEADS)], axis=1).astype(o_ref.dtype)


def mla_decode(page_table, qlat8, qpe8, lat_new, kpe_new, wuv, cache_latent, cache_krope_t):
    b, n_pages = page_table.shape
    page = cache_latent.shape[2]
    assert n_pages % (DEC_SLOTS * DEC_GROUP) == 0
    pt3 = page_table.reshape(b, 1, n_pages)
    smem_row = lambda f: pl.BlockSpec((1, 1, n_pages), f, memory_space=pltpu.SMEM)
    return pl.pallas_call(
        functools.partial(_decode_kernel, nb=b, n_pages=n_pages, page=page),
        out_shape=jax.ShapeDtypeStruct((b, 1, MLA_HEADS * MLA_V), BF16),
        grid=(b,),
        in_specs=[smem_row(lambda i: (i, 0, 0)),
                  smem_row(lambda i: (jnp.minimum(i + 1, b - 1), 0, 0)),
                  pl.BlockSpec((1, DEC_HEAD_PAD, MLA_KV_RANK), lambda i: (i, 0, 0)),
                  pl.BlockSpec((1, DEC_HEAD_PAD, MLA_ROPE), lambda i: (i, 0, 0)),
                  pl.BlockSpec((1, 1, MLA_KV_RANK), lambda i: (i, 0, 0)),
                  pl.BlockSpec((1, 1, MLA_ROPE), lambda i: (i, 0, 0)),
                  pl.BlockSpec(wuv.shape, lambda i: (0, 0)),
                  pl.BlockSpec(memory_space=pl.ANY),
                  pl.BlockSpec(memory_space=pl.ANY)],
        out_specs=pl.BlockSpec((1, 1, MLA_HEADS * MLA_V), lambda i: (i, 0, 0)),
        scratch_shapes=[pltpu.VMEM((DEC_SLOTS, DEC_GROUP, page, MLA_KV_RANK), F32),
                        pltpu.VMEM((DEC_SLOTS, DEC_GROUP, MLA_ROPE, page), F32),
                        pltpu.SemaphoreType.DMA((2, DEC_SLOTS))],
        compiler_params=_cparams("arbitrary"),
        name="mla_decode",
    )(pt3, pt3, qlat8, qpe8, lat_new.reshape(b, 1, -1), kpe_new.reshape(b, 1, -1), wuv,
      cache_latent, cache_krope_t)


def _softmax_rows(s):
    m = jnp.max(s, axis=-1, keepdims=True)
    p = jnp.exp(s - m)
    return p / jnp.sum(p, axis=-1, keepdims=True)


def _xattn_prompt_kernel(q_ref, k_ref, v_ref, o_ref, *, dh):
    scale = dh ** -0.5
    for h in range(X_HEADS):
        q = q_ref[0, :, h * dh:(h + 1) * dh]
        s = lax.dot_general(q, k_ref[0, :, h * dh:(h + 1) * dh], (((1,), (1,)), ((), ())),
                            preferred_element_type=F32) * scale
        p = _softmax_rows(s).astype(BF16)
        o_ref[0, :, h * dh:(h + 1) * dh] = jnp.dot(
            p, v_ref[0, :, h * dh:(h + 1) * dh], preferred_element_type=F32).astype(o_ref.dtype)


def xattn_prompt(q, k, v, *, tq=512):
    b, t, d = q.shape
    mem = k.shape[1]
    tq = min(tq, t)
    return pl.pallas_call(
        functools.partial(_xattn_prompt_kernel, dh=d // X_HEADS),
        out_shape=jax.ShapeDtypeStruct((b, t, d), BF16),
        grid=(b, t // tq),
        in_specs=[pl.BlockSpec((1, tq, d), lambda i, j: (i, j, 0)),
                  pl.BlockSpec((1, mem, d), lambda i, j: (i, 0, 0)),
                  pl.BlockSpec((1, mem, d), lambda i, j: (i, 0, 0))],
        out_specs=pl.BlockSpec((1, tq, d), lambda i, j: (i, j, 0)),
        compiler_params=_cparams("parallel", "parallel"),
        name="xattn_prompt",
    )(q, k, v)


XATTN_SAMPLE_BB = 2


def _xattn_sample_kernel(q_ref, k_ref, v_ref, o_ref, *, dh):
    scale = dh ** -0.5
    for b in range(XATTN_SAMPLE_BB):
        q = q_ref[b].astype(F32)
        s = jnp.sum(k_ref[b] * q[None], axis=-1, keepdims=True) * scale
        m = jnp.max(s, axis=0, keepdims=True)
        p = jnp.exp(s - m)
        denom = jnp.sum(p, axis=0, keepdims=True)
        o = jnp.sum(p * v_ref[b], axis=0) / denom[0]
        o_ref[b] = o.astype(o_ref.dtype)


def xattn_sample(q, k, v):
    b, heads, dh = q.shape
    mem = k.shape[1]
    bb = XATTN_SAMPLE_BB
    assert b % bb == 0
    kv_spec = pl.BlockSpec((bb, mem, heads, dh), lambda i: (i, 0, 0, 0))
    return pl.pallas_call(
        functools.partial(_xattn_sample_kernel, dh=dh),
        out_shape=jax.ShapeDtypeStruct((b, heads, dh), BF16),
        grid=(b // bb,),
        in_specs=[pl.BlockSpec((bb, heads, dh), lambda i: (i, 0, 0)), kv_spec, kv_spec],
        out_specs=pl.BlockSpec((bb, heads, dh), lambda i: (i, 0, 0)),
        compiler_params=_cparams("parallel"),
        name="xattn_sample",
    )(q, k, v)


def _post_xattn_kernel(ox_ref, wxo_ref, h_ref, g_ref, wr_hi_ref, wr_lo_ref, br_ref, cnt_in_ref,
                       h2_ref, hn_ref, gate_ref, idx_ref, rank_ref, cnt_ref, cnt_s):
    @pl.when(pl.program_id(0) == 0)
    def _():
        cnt_s[...] = cnt_in_ref[...]

    h2 = h_ref[...] + jnp.dot(ox_ref[...], wxo_ref[...], preferred_element_type=F32)
    h2_ref[...] = h2
    hn = _rms(h2, g_ref[...])
    hn_ref[...] = hn.reshape(hn_ref.shape)
    hn_hi = hn.astype(BF16)
    hn_lo = (hn - hn_hi.astype(F32)).astype(BF16)
    logits = (jnp.dot(hn_hi, wr_hi_ref[...], preferred_element_type=F32)
              + jnp.dot(hn_hi, wr_lo_ref[...], preferred_element_type=F32)
              + jnp.dot(hn_lo, wr_hi_ref[...], preferred_element_type=F32)) + br_ref[...]
    lane = lax.broadcasted_iota(I32, logits.shape, 1)
    logits = jnp.where(lane < N_EXPERTS, logits, NEG_BIG)
    vals, picks = [], []
    gates = jnp.zeros(logits.shape, F32)
    ids = jnp.zeros(logits.shape, I32)
    member = jnp.zeros(logits.shape, F32)
    for k in range(TOP_K):
        m = jnp.max(logits, axis=-1, keepdims=True)
        idx = jnp.min(jnp.where(logits == m, lane, LANES), axis=-1, keepdims=True)
        vals.append(m)
        picks.append(idx)
        ids = jnp.where(lane == k, idx, ids)
        member = jnp.where(lane == idx, 1.0, member)
        logits = jnp.where(lane == idx, NEG_BIG, logits)
    exps = [jnp.exp(v - vals[0]) for v in vals]
    denom = exps[0]
    for e in exps[1:]:
        denom = denom + e
    for k in range(TOP_K):
        gates = jnp.where(lane == k, exps[k] / denom, gates)
    gate_ref[...] = gates
    idx_ref[...] = ids
    tm = logits.shape[0]
    before = (lax.broadcasted_iota(I32, (tm, tm), 0) > lax.broadcasted_iota(I32, (tm, tm), 1)).astype(BF16)
    prior = jnp.dot(before, member.astype(BF16), preferred_element_type=F32) + cnt_s[...]
    ranks = jnp.zeros(logits.shape, I32)
    for k in range(TOP_K):
        rk = jnp.sum(jnp.where(lane == picks[k], prior, 0.0), axis=-1, keepdims=True)
        ranks = jnp.where(lane == k, rk.astype(I32), ranks)
    rank_ref[...] = ranks
    cnt = cnt_s[...] + jnp.sum(member, axis=0, keepdims=True)
    cnt_s[...] = cnt
    cnt_ref[...] = cnt


def post_xattn(ox, wxo, h, norm_ffn, wr_hi, wr_lo, br_row, cnt_in, *, tm=256):
    m, d = h.shape
    tm = min(tm, m)
    row = lambda i: (i, 0)
    const = lambda i: (0, 0)
    return pl.pallas_call(
        _post_xattn_kernel,
        out_shape=[jax.ShapeDtypeStruct((m, d), F32), jax.ShapeDtypeStruct((m, d // LANES, LANES), F32),
                   jax.ShapeDtypeStruct((m, LANES), F32), jax.ShapeDtypeStruct((m, LANES), I32),
                   jax.ShapeDtypeStruct((m, LANES), I32), jax.ShapeDtypeStruct((1, LANES), F32)],
        grid=(m // tm,),
        in_specs=[pl.BlockSpec((tm, d), row), pl.BlockSpec(wxo.shape, const), pl.BlockSpec((tm, d), row),
                  pl.BlockSpec((1, d), const), pl.BlockSpec(wr_hi.shape, const), pl.BlockSpec(wr_lo.shape, const),
                  pl.BlockSpec((1, LANES), const), pl.BlockSpec((1, LANES), const)],
        out_specs=[pl.BlockSpec((tm, d), row), pl.BlockSpec((tm, d // LANES, LANES), lambda i: (i, 0, 0)),
                   pl.BlockSpec((tm, LANES), row), pl.BlockSpec((tm, LANES), row),
                   pl.BlockSpec((tm, LANES), row), pl.BlockSpec((1, LANES), const)],
        scratch_shapes=[pltpu.VMEM((1, LANES), F32)],
        compiler_params=_cparams("arbitrary"),
        name="post_xattn_router",
    )(ox, wxo, h, norm_ffn.reshape(1, -1), wr_hi, wr_lo, br_row, cnt_in)


MOE_TILE = 256


ROW_TILE = 128
MOE_FF_CHUNK = 512


def _row_dma_loop(n_tokens, make_copies, wait):
    def body(n, _):
        for k, cp in enumerate(make_copies(n)):
            cp.wait() if wait else cp.start(priority=k % 2)
        return 0
    lax.fori_loop(0, n_tokens, body, 0, unroll=4)


def _dispatch_kernel(dest_ref, x_ref, xs_in_ref, xs_ref, buf, sem, *, nt):
    del xs_in_ref
    t = pl.program_id(0)
    tm = ROW_TILE
    slot = t % 2

    def copies(s, dest_of):
        def make(n):
            return [pltpu.make_async_copy(buf.at[s, n], xs_ref.at[dest_of(n, k)], sem.at[s]) for k in range(TOP_K)]
        return make

    def wait_slot(s):
        _row_dma_loop(tm, copies(s, lambda n, k: 0), wait=True)

    if nt > 2:
        @pl.when(t >= 2)
        def _():
            wait_slot(slot)

    buf[slot] = x_ref[...]
    _row_dma_loop(tm, copies(slot, lambda n, k: dest_ref[0, 0, n * TOP_K + k]), wait=False)

    @pl.when(t == nt - 1)
    def _():
        wait_slot(slot)
        if nt > 1:
            wait_slot(1 - slot)


def moe_dispatch(dest, x3, xs):
    m = x3.shape[0]
    tm = ROW_TILE
    assert m % tm == 0
    n_tiles = m // tm
    return pl.pallas_call(
        functools.partial(_dispatch_kernel, nt=n_tiles),
        out_shape=jax.ShapeDtypeStruct(xs.shape, xs.dtype),
        grid=(n_tiles,),
        in_specs=[pl.BlockSpec((1, 1, TOP_K * tm), lambda t: (t, 0, 0), memory_space=pltpu.SMEM),
                  pl.BlockSpec((tm,) + x3.shape[1:], lambda t: (t, 0, 0)),
                  pl.BlockSpec(memory_space=pl.ANY)],
        out_specs=pl.BlockSpec(memory_space=pl.ANY),
        scratch_shapes=[pltpu.VMEM((2, tm) + x3.shape[1:], x3.dtype), pltpu.SemaphoreType.DMA((2,))],
        input_output_aliases={2: 0},
        compiler_params=_cparams("arbitrary"),
        name="moe_dispatch",
    )(dest.reshape(n_tiles, 1, TOP_K * tm), x3, xs)


def _moe_kernel(te_ref, tv_ref, x_ref, w1_ref, b1_ref, w2_ref, b2_ref, y_ref, w1b, w2s, w2p_ref):
    t = pl.program_id(0)

    @pl.when(jnp.logical_or(t == 0, te_ref[t] != te_ref[jnp.maximum(t - 1, 0)]))
    def _():
        w1b[...] = w1_ref[0].astype(BF16)
        half = LANES // 2
        n_chunks = w2s.shape[0]
        for c in range(n_chunks):
            for blk in range(w2s.shape[1] // LANES):
                r0 = blk * LANES
                w2s[c, pl.ds(r0, half, stride=2), :] = w2_ref[0, r0:r0 + half, c * LANES:(c + 1) * LANES]
                w2s[c, pl.ds(r0 + 1, half, stride=2), :] = w2_ref[0, r0 + half:r0 + LANES, c * LANES:(c + 1) * LANES]
        w2p_ref[...] = jnp.concatenate([w2s[c] for c in range(n_chunks)], axis=1).astype(BF16)

    @pl.when(tv_ref[t] > 0)
    def _():
        tm = x_ref.shape[0]
        x = x_ref[...].reshape(tm, w1b.shape[0]).astype(BF16)
        acc = jnp.broadcast_to(b2_ref[0], (tm, w1b.shape[0]))
        ff2 = w1b.shape[1]
        even = lax.broadcasted_iota(I32, (x.shape[0], LANES), 1) % 2 == 0
        for c in range(ff2 // MOE_FF_CHUNK):
            lo = c * MOE_FF_CHUNK
            hh = jnp.dot(x, w1b[:, lo:lo + MOE_FF_CHUNK], preferred_element_type=F32) + b1_ref[0, :, lo:lo + MOE_FF_CHUNK]
            parts = []
            for j in range(MOE_FF_CHUNK // (2 * LANES)):
                a = hh[:, (2 * j) * LANES:(2 * j + 1) * LANES]
                b = hh[:, (2 * j + 1) * LANES:(2 * j + 2) * LANES]
                glu = jnp.where(even, a, pltpu.roll(b, 1, 1))
                lin = jnp.where(even, pltpu.roll(a, LANES - 1, 1), b)
                glu = jnp.minimum(glu, SWIGLU_LIMIT)
                lin = jnp.clip(lin, -SWIGLU_LIMIT, SWIGLU_LIMIT)
                parts.append((lin + 1.0) * (glu * _sigmoid(SWIGLU_ALPHA * glu)))
            act = jnp.concatenate(parts, axis=1).astype(BF16)
            acc = acc + jnp.dot(act, w2p_ref[lo // 2:(lo + MOE_FF_CHUNK) // 2, :], preferred_element_type=F32)
        y_ref[...] = acc.reshape(y_ref.shape)

    @pl.when(tv_ref[t] == 0)
    def _():
        y_ref[...] = jnp.zeros(y_ref.shape, F32)


def moe_experts(tile_expert, tile_valid, xs, w1, b1, w2, b2):
    n_tiles = tile_expert.shape[0]
    tm = MOE_TILE
    d = w1.shape[1]
    ff2 = w1.shape[2]
    assert ff2 % MOE_FF_CHUNK == 0 and MOE_FF_CHUNK % (2 * LANES) == 0 and xs.shape[1:] == (d // LANES, LANES)
    wmap = lambda t, te, tv: (te[t], 0, 0)
    row_tiles = pl.BlockSpec((tm, d // LANES, LANES), lambda t, te, tv: (t, 0, 0))
    grid_spec = pltpu.PrefetchScalarGridSpec(
        num_scalar_prefetch=2,
        grid=(n_tiles,),
        in_specs=[row_tiles,
                  pl.BlockSpec((1, d, ff2), wmap), pl.BlockSpec((1, 1, ff2), wmap),
                  pl.BlockSpec((1, ff2 // 2, d), wmap), pl.BlockSpec((1, 1, d), wmap)],
        out_specs=row_tiles,
        scratch_shapes=[pltpu.VMEM((d, ff2), BF16), pltpu.VMEM((d // LANES, ff2 // 2, LANES), F32),
                        pltpu.VMEM((ff2 // 2, d), BF16)],
    )
    return pl.pallas_call(
        _moe_kernel,
        out_shape=jax.ShapeDtypeStruct(xs.shape, F32),
        grid_spec=grid_spec,
        compiler_params=pltpu.CompilerParams(dimension_semantics=("arbitrary",), vmem_limit_bytes=MOE_VMEM_LIMIT),
        name="moe_experts",
    )(tile_expert, tile_valid, xs, w1, b1, w2, b2)


def _combine_kernel(dest_ref, destn_ref, y_hbm, h2_ref, gate_ref, gain_ref, o_ref, ybuf, sem, *, nt):
    t = pl.program_id(0)
    tm = ROW_TILE
    slot = t % 2

    def copies(s, dest_of):
        def make(n):
            return [pltpu.make_async_copy(y_hbm.at[dest_of(n, k)], ybuf.at[s, k * tm + n], sem.at[s])
                    for k in range(TOP_K)]
        return make

    @pl.when(t == 0)
    def _():
        _row_dma_loop(tm, copies(0, lambda n, k: dest_ref[0, 0, n * TOP_K + k]), wait=False)

    if nt > 1:
        @pl.when(t + 1 < nt)
        def _():
            _row_dma_loop(tm, copies(1 - slot, lambda n, k: destn_ref[0, 0, n * TOP_K + k]), wait=False)

    _row_dma_loop(tm, copies(slot, lambda n, k: 0), wait=True)
    gates = gate_ref[...]
    acc = h2_ref[...]
    for k in range(TOP_K):
        yk = ybuf[slot, k * tm:(k + 1) * tm].reshape(acc.shape)
        acc = acc + gates[:, k:k + 1] * yk
    o_ref[...] = _rms(acc, gain_ref[...])


def moe_combine(dest, y_sorted, h2, gates, norm_final):
    m, d = h2.shape
    tm = ROW_TILE
    assert m % tm == 0
    n_tiles = m // tm
    row = lambda t: (t, 0)
    dest3 = dest.reshape(n_tiles, 1, TOP_K * tm)
    return pl.pallas_call(
        functools.partial(_combine_kernel, nt=n_tiles),
        out_shape=jax.ShapeDtypeStruct((m, d), F32),
        grid=(n_tiles,),
        in_specs=[pl.BlockSpec((1, 1, TOP_K * tm), lambda t: (t, 0, 0), memory_space=pltpu.SMEM),
                  pl.BlockSpec((1, 1, TOP_K * tm), lambda t: (jnp.minimum(t + 1, n_tiles - 1), 0, 0),
                               memory_space=pltpu.SMEM),
                  pl.BlockSpec(memory_space=pl.ANY),
                  pl.BlockSpec((tm, d), row), pl.BlockSpec((tm, LANES), row),
                  pl.BlockSpec((1, d), lambda t: (0, 0))],
        out_specs=pl.BlockSpec((tm, d), row),
        scratch_shapes=[pltpu.VMEM((2, TOP_K * tm, d // LANES, LANES), F32), pltpu.SemaphoreType.DMA((2,))],
        compiler_params=_cparams("arbitrary"),
        name="moe_combine",
    )(dest3, dest3, y_sorted, h2, gates, norm_final.reshape(1, -1))


def _moe_plan(counts_row, n_assign):
    tm = MOE_TILE
    n_tiles = (n_assign + N_EXPERTS * (tm - 1) + tm - 1) // tm
    counts = counts_row[0, :N_EXPERTS].astype(I32)
    tiles_per = (counts + tm - 1) // tm
    tile_end = jnp.cumsum(tiles_per)
    pad_start = (tile_end - tiles_per) * tm
    tile_ids = jnp.arange(n_tiles, dtype=I32)
    used = tile_end[-1]
    tile_valid = (tile_ids < used).astype(I32)
    clamped = jnp.minimum(tile_ids, used - 1)
    tile_expert = jnp.sum((clamped[:, None] >= tile_end[None, :]).astype(I32), axis=1)
    return n_tiles, jnp.minimum(tile_expert, N_EXPERTS - 1), tile_valid, pad_start


def _dest_rows(ids, ranks, pad_start):
    sel = ids[:, :TOP_K, None] == jnp.arange(N_EXPERTS, dtype=I32)
    return ranks[:, :TOP_K] + jnp.sum(jnp.where(sel, pad_start, 0), axis=-1)


def _rope_tables(positions):
    half = MLA_ROPE // 2
    inv_freq = 1.0 / (ROPE_THETA ** (jnp.arange(half, dtype=F32) / half))
    ang = positions.astype(F32)[:, None] * inv_freq[None, :]
    cos, sin = jnp.cos(ang), jnp.sin(ang)
    zeros = jnp.zeros((positions.shape[0], LANES - MLA_ROPE), F32)
    return (jnp.concatenate([cos, cos, zeros], axis=1), jnp.concatenate([-sin, sin, zeros], axis=1))


def _in_proj_weight(w_in):
    o = np.cumsum([0, CONV_CH, GDN_HEADS * GDN_D, GDN_HEADS, GDN_HEADS, MLA_Q_RANK, MLA_KV_RANK, MLA_ROPE])
    conv, z, bl, al, cq, ckv, kpe = (w_in[:, o[i]:o[i + 1]] for i in range(7))
    pad = jnp.zeros((w_in.shape[0], LANES - MLA_ROPE - 2 * GDN_HEADS), w_in.dtype)
    return jnp.concatenate([conv, z, cq, ckv, kpe, bl, al, pad], axis=1).astype(BF16)


IN_SPLITS = (CONV_CH, GDN_HEADS * GDN_D, MLA_Q_RANK, MLA_KV_RANK, LANES)


def _lane_row(vals, lane0):
    return jnp.zeros((1, LANES), F32).at[0, lane0:lane0 + vals.shape[0]].set(vals.astype(F32))


def kernel(x_prompt, x_sample, state_gdn, state_conv, cache_latent, cache_krope, page_table, cache_mem_k, cache_mem_v, mem_prompt, norm_mix, w_in, conv_w, a_log, dt_bias, gdn_norm, q_norm, w_uq, kv_norm, w_uk, w_uv, w_out, norm_x, mem_norm, w_xq, w_xk, w_xv, w_xo, norm_ffn, w_router, b_router, w_e1, b_e1, w_e2, b_e2, norm_final):
    depth = w_in.shape[0]
    assert depth == 1
    L = 0
    bp, t, d = x_prompt.shape
    bs = x_sample.shape[0]
    assert x_sample.shape[1] == 1
    past_len = page_table.shape[1] * cache_latent.shape[2]
    mp = bp * t

    w_in_p = _in_proj_weight(w_in[L])
    alog_row = _lane_row(a_log[L], DECAY_LANE)
    dtb_row = _lane_row(dt_bias[L], DECAY_LANE)
    wuq = w_uq[L]
    wuq_p = jnp.concatenate(
        [wuq, jnp.zeros(wuq.shape[:2] + (MLA_QK_PAD - wuq.shape[2],), wuq.dtype)], axis=2
    ).reshape(wuq.shape[0], MLA_HEADS * MLA_QK_PAD).astype(BF16)
    wuk = w_uk[L].reshape(MLA_KV_RANK, MLA_HEADS * MLA_NOPE).astype(BF16)
    wuv = w_uv[L].reshape(MLA_KV_RANK, MLA_HEADS * MLA_V).astype(BF16)
    wuk_t = jnp.transpose(w_uk[L], (1, 2, 0)).astype(BF16)
    w_out_b = w_out[L].astype(BF16)
    n_gdn = GDN_HEADS * GDN_D
    w_xq_b, w_xo_b = w_xq[L].astype(BF16), w_xo[L].astype(BF16)
    w_xkv_b = jnp.concatenate([w_xk[L], w_xv[L]], axis=1).astype(BF16)
    wr = jnp.concatenate([w_router[L], jnp.zeros((d, LANES - N_EXPERTS), F32)], axis=1)
    wr_hi = wr.astype(BF16)
    wr_lo = (wr - wr_hi.astype(F32)).astype(BF16)
    br_row = _lane_row(b_router[L], 0)
    b1 = b_e1[L][:, None, :]
    b2 = b_e2[L][:, None, :]
    cos_p, sin_p = _rope_tables(jnp.arange(t, dtype=I32))
    cos_s, sin_s = _rope_tables(jnp.full((bs,), past_len, I32))

    xp = x_prompt.reshape(mp, d)
    xs = x_sample.reshape(bs, d)
    conv_p, z_p, cq_p, ckv_p, kba_p = fused_linear([xp], [w_in_p], gain=norm_mix[L], splits=IN_SPLITS, name="in_proj_prompt")
    conv_s, z_s, cq_s, ckv_s, kba_s = fused_linear([xs], [w_in_p], gain=norm_mix[L], splits=IN_SPLITS, name="in_proj_sample")

    y_gdn_p, gdn_state_p = gdn_prompt(conv_p.reshape(bp, t, CONV_CH), z_p.reshape(bp, t, n_gdn),
                                      kba_p.reshape(bp, t, LANES), conv_w[L], alog_row, dtb_row, gdn_norm[L])
    conv_state_p = conv_p.reshape(bp, t, CONV_CH)[:, t - (CONV_WIDTH - 1):, :]
    gdn_state_s, conv_state_s, y_gdn_s = gdn_sample(conv_s, state_conv[L], kba_s, z_s, state_gdn[L],
                                                    conv_w[L], alog_row, dtb_row, gdn_norm[L])

    q_p, k_p, v_p, lat_p, kpe_p = mla_prep(cq_p, ckv_p, kba_p, cos_p, sin_p, q_norm[L], kv_norm[L],
                                           wuq_p, wuk, wuv, seq=t, q_dtype=BF16)
    y_mla_p = mla_flash(q_p.reshape(bp, t, -1), k_p.reshape(bp, t, -1), v_p.reshape(bp, t, -1))
    q_s, _, _, lat_s, kpe_s = mla_prep(cq_s, ckv_s, kba_s, cos_s, sin_s, q_norm[L], kv_norm[L],
                                       wuq_p, wuk, wuv, seq=bs, q_dtype=F32)
    q_s4 = q_s.reshape(bs, MLA_HEADS, MLA_QK_PAD)
    qlat = jnp.concatenate(
        [fused_linear([q_s4[:, h, :MLA_NOPE]], [wuk_t[h]], name=f"absorb_q{h}")[0][:, None, :]
         for h in range(MLA_HEADS)], axis=1)
    head_pad = ((0, 0), (0, DEC_HEAD_PAD - MLA_HEADS), (0, 0))
    qlat8 = jnp.pad(qlat, head_pad)
    qpe8 = jnp.pad(q_s4[:, :, MLA_NOPE:MLA_NOPE + MLA_ROPE], head_pad)
    y_mla_s = mla_decode(page_table, qlat8, qpe8, lat_s, kpe_s, wuv,
                         cache_latent, jnp.swapaxes(cache_krope, 2, 3)).reshape(bs, -1)

    (h_p,) = fused_linear([y_gdn_p.reshape(mp, n_gdn), y_mla_p.reshape(mp, -1)], [w_out_b[:n_gdn], w_out_b[n_gdn:]],
                          residual=xp, name="out_proj_prompt")
    (h_s,) = fused_linear([y_gdn_s, y_mla_s], [w_out_b[:n_gdn], w_out_b[n_gdn:]], residual=xs, name="out_proj_sample")

    mem_tokens = mem_prompt.shape[1]
    mk_f, mv_f, mk_b, mv_b = fused_linear([mem_prompt.reshape(bp * mem_tokens, d)], [w_xkv_b], gain=mem_norm[L],
                                          splits=(d, d), bf16_copies=True, name="memory_kv")
    (qx_p,) = fused_linear([h_p], [w_xq_b], gain=norm_x[L], out_dtypes=(BF16,), name="xq_prompt")
    (qx_s,) = fused_linear([h_s], [w_xq_b], gain=norm_x[L], out_dtypes=(BF16,), name="xq_sample")
    ox_p = xattn_prompt(qx_p.reshape(bp, t, d), mk_b.reshape(bp, mem_tokens, d), mv_b.reshape(bp, mem_tokens, d))
    ox_s = xattn_sample(qx_s.reshape(bs, X_HEADS, d // X_HEADS), cache_mem_k[L], cache_mem_v[L])

    h2_p, hn_p, gate_p, ids_p, rank_p, cnt_p = post_xattn(ox_p.reshape(mp, d), w_xo_b, h_p, norm_ffn[L],
                                                          wr_hi, wr_lo, br_row, jnp.zeros((1, LANES), F32))
    h2_s, hn_s, gate_s, ids_s, rank_s, cnt_all = post_xattn(ox_s.reshape(bs, d), w_xo_b, h_s, norm_ffn[L],
                                                            wr_hi, wr_lo, br_row, cnt_p)

    n_tiles, tile_expert, tile_valid, pad_start = _moe_plan(cnt_all, TOP_K * (mp + bs))
    dest_p = _dest_rows(ids_p, rank_p, pad_start)
    dest_s = _dest_rows(ids_s, rank_s, pad_start)
    x_grouped = jnp.zeros((n_tiles * MOE_TILE, d // LANES, LANES), F32)
    x_grouped = moe_dispatch(dest_p, hn_p, x_grouped)
    x_grouped = moe_dispatch(dest_s, hn_s, x_grouped)
    y_sorted = moe_experts(tile_expert, tile_valid, x_grouped, w_e1[L], b1, w_e2[L], b2)
    y_prompt = moe_combine(dest_p, y_sorted, h2_p, gate_p, norm_final)
    y_sample = moe_combine(dest_s, y_sorted, h2_s, gate_s, norm_final)

    x_heads = X_HEADS
    return (y_prompt.reshape(bp, t, d), y_sample.reshape(bs, 1, d),
            gdn_state_p[None], conv_state_p[None],
            lat_p.reshape(bp, 1, t, MLA_KV_RANK), kpe_p.reshape(bp, 1, t, MLA_ROPE),
            mk_f.reshape(1, bp, mem_tokens, x_heads, d // x_heads), mv_f.reshape(1, bp, mem_tokens, x_heads, d // x_heads),
            gdn_state_s[None], conv_state_s[None],
            lat_s.reshape(bs, 1, 1, MLA_KV_RANK), kpe_s.reshape(bs, 1, 1, MLA_ROPE))
```

```python
import functools

import jax
import jax.numpy as jnp
import numpy as np
from jax import lax
from jax.experimental import pallas as pl
from jax.experimental.pallas import tpu as pltpu

F32 = jnp.float32
BF16 = jnp.bfloat16
I32 = jnp.int32

NORM_EPS = 1e-6
LANES = 128
SUBLANES = 8
VMEM_LIMIT = 48 * 1024 * 1024
MOE_VMEM_LIMIT = 56 * 1024 * 1024

GDN_HEADS = 4
GDN_D = 128
CONV_WIDTH = 4
CONV_CH = 3 * GDN_HEADS * GDN_D
GDN_BLOCK = 128
MLA_HEADS = 4
MLA_NOPE = 128
MLA_ROPE = 64
MLA_V = 128
MLA_Q_RANK = 384
MLA_KV_RANK = 256
MLA_QK_PAD = 256
ROPE_THETA = 10000.0
MLA_SCALE = (MLA_NOPE + MLA_ROPE) ** -0.5
X_HEADS = 4
N_EXPERTS = 32
TOP_K = 4
SWIGLU_LIMIT = 7.0
SWIGLU_ALPHA = 1.702
BETA_LANE = MLA_ROPE
DECAY_LANE = MLA_ROPE + GDN_HEADS
NEG_BIG = -1e30


def _cparams(*sem):
    return pltpu.CompilerParams(dimension_semantics=sem, vmem_limit_bytes=VMEM_LIMIT)


def _rms(x, gain):
    return x * lax.rsqrt(jnp.mean(x * x, axis=-1, keepdims=True) + NORM_EPS) * gain


def _mm(a, b):
    return jnp.dot(a.astype(BF16), b.astype(BF16), preferred_element_type=F32)


def _mm_nt(a, b):
    return lax.dot_general(a.astype(BF16), b.astype(BF16), (((1,), (1,)), ((), ())),
                           preferred_element_type=F32)


def _mm3(a, b):
    a_hi = a.astype(BF16)
    b_hi = b.astype(BF16)
    a_lo = (a - a_hi.astype(F32)).astype(BF16)
    b_lo = (b - b_hi.astype(F32)).astype(BF16)
    return jnp.dot(jnp.concatenate([a_hi, a_lo, a_hi], axis=1), jnp.concatenate([b_hi, b_hi, b_lo], axis=0),
                   preferred_element_type=F32)


def _bmm(a, b):
    return lax.dot_general(a.astype(BF16), b.astype(BF16), (((2,), (1,)), ((0,), (0,))), preferred_element_type=F32)


def _bmm_nt(a, b):
    return lax.dot_general(a.astype(BF16), b.astype(BF16), (((2,), (2,)), ((0,), (0,))), preferred_element_type=F32)


def _bmm3(a, b):
    a_hi = a.astype(BF16)
    b_hi = b.astype(BF16)
    a_lo = (a - a_hi.astype(F32)).astype(BF16)
    b_lo = (b - b_hi.astype(F32)).astype(BF16)
    return lax.dot_general(jnp.concatenate([a_hi, a_lo, a_hi], axis=2), jnp.concatenate([b_hi, b_hi, b_lo], axis=1),
                           (((2,), (1,)), ((0,), (0,))), preferred_element_type=F32)


def _sigmoid(x):
    return 1.0 / (1.0 + jnp.exp(-x))


def _softplus(x):
    return jnp.maximum(x, 0.0) + jnp.log1p(jnp.exp(-jnp.abs(x)))


def _linear_kernel(*refs, n_in, has_gain, has_res, splits):
    a_refs = refs[:n_in]
    w_refs = refs[n_in:2 * n_in]
    pos = 2 * n_in
    g_ref = refs[pos] if has_gain else None
    pos += int(has_gain)
    r_ref = refs[pos] if has_res else None
    pos += int(has_res)
    out_refs = refs[pos:]
    a0 = a_refs[0][...]
    if has_gain:
        a0 = _rms(a0.astype(F32), g_ref[...])
    acts = [a0.astype(BF16)] + [a[...].astype(BF16) for a in a_refs[1:]]
    off = 0
    for i, width in enumerate(splits):
        acc = None
        for a, w in zip(acts, w_refs):
            d = jnp.dot(a, w[:, off:off + width], preferred_element_type=F32)
            acc = d if acc is None else acc + d
        if has_res:
            acc = acc + r_ref[:, off:off + width]
        for o_ref in out_refs[i::len(splits)]:
            o_ref[...] = acc.astype(o_ref.dtype)
        off += width


def fused_linear(acts, weights, *, gain=None, residual=None, splits=None, out_dtypes=None,
                 bf16_copies=False, tm=256, name="fused_linear"):
    m = acts[0].shape[0]
    n = weights[0].shape[1]
    tm = min(tm, m)
    assert m % tm == 0
    splits = tuple(splits) if splits is not None else (n,)
    assert sum(splits) == n and all(s % LANES == 0 for s in splits)
    out_dtypes = tuple(out_dtypes) if out_dtypes is not None else (F32,) * len(splits)
    out_widths = splits
    if bf16_copies:
        out_widths = splits + splits
        out_dtypes = out_dtypes + (BF16,) * len(splits)
    in_specs = [pl.BlockSpec((tm, a.shape[1]), lambda i: (i, 0)) for a in acts]
    in_specs += [pl.BlockSpec(w.shape, lambda i: (0, 0)) for w in weights]
    args = list(acts) + list(weights)
    if gain is not None:
        in_specs.append(pl.BlockSpec((1, gain.shape[-1]), lambda i: (0, 0)))
        args.append(gain.reshape(1, -1))
    if residual is not None:
        in_specs.append(pl.BlockSpec((tm, n), lambda i: (i, 0)))
        args.append(residual)
    outs = pl.pallas_call(
        functools.partial(_linear_kernel, n_in=len(acts), has_gain=gain is not None,
                          has_res=residual is not None, splits=splits),
        out_shape=[jax.ShapeDtypeStruct((m, s), dt) for s, dt in zip(out_widths, out_dtypes)],
        grid=(m // tm,),
        in_specs=in_specs,
        out_specs=[pl.BlockSpec((tm, s), lambda i: (i, 0)) for s in out_widths],
        compiler_params=_cparams("parallel"),
        name=name,
    )(*args)
    return outs


def _gate_values(kba, alog_row, dtb_row):
    beta = _sigmoid(kba)
    g = -jnp.exp(alog_row) * _softplus(kba + dtb_row)
    return beta, g


def _l2norm(x):
    return x * lax.rsqrt(jnp.sum(x * x, axis=-1, keepdims=True) + NORM_EPS)


GDN_PROMPT_BB = 2


def _gdn_prompt_kernel(x_ref, z_ref, kba_ref, cw_ref, alog_ref, dtb_ref, gn_ref,
                       y_ref, s_out_ref, xbuf, state):
    t = pl.program_id(1)
    nt = pl.num_programs(1)

    @pl.when(t == 0)
    def _():
        xbuf[:, 0:SUBLANES, :] = jnp.zeros((GDN_PROMPT_BB, SUBLANES, CONV_CH), F32)
        state[...] = jnp.zeros(state.shape, F32)

    blk = GDN_BLOCK
    nh = GDN_HEADS * GDN_D
    row = lax.broadcasted_iota(I32, (blk, blk), 0)
    col = lax.broadcasted_iota(I32, (blk, blk), 1)
    qs, ks, vs, betas, gcols, grows = [], [], [], [], [], []
    for bi in range(GDN_PROMPT_BB):
        c, beta_all, gc = _gdn_conv_and_gates(x_ref.at[bi], kba_ref.at[bi], cw_ref, alog_ref, dtb_ref, xbuf.at[bi], row)
        gc_t = gc.T
        for h in range(GDN_HEADS):
            qs.append(c[:, h * GDN_D:(h + 1) * GDN_D])
            ks.append(c[:, nh + h * GDN_D:nh + (h + 1) * GDN_D])
            vs.append(c[:, 2 * nh + h * GDN_D:2 * nh + (h + 1) * GDN_D])
            betas.append(beta_all[:, BETA_LANE + h:BETA_LANE + h + 1])
            gcols.append(gc[:, DECAY_LANE + h:DECAY_LANE + h + 1])
            grows.append(gc_t[DECAY_LANE + h:DECAY_LANE + h + 1, :])

    q = _l2norm(jnp.stack(qs)) * (GDN_D ** -0.5)
    k = _l2norm(jnp.stack(ks))
    v = jnp.stack(vs)
    bcol = jnp.stack(betas)
    gcol = jnp.stack(gcols)
    grow = jnp.stack(grows)
    decay = jnp.exp(jnp.where((row >= col)[None], gcol - grow, NEG_BIG))
    kb = k * bcol
    vb = v * bcol
    a = jnp.where((row > col)[None], _bmm_nt(kb, k) * decay, 0.0)
    x = (row == col).astype(F32)[None] - a
    p = _bmm3(a, a)
    x = x + _bmm3(x, p)
    for _ in range(5):
        p = _bmm3(p, p)
        x = x + _bmm3(x, p)
    egc = jnp.exp(gcol)
    u = _bmm(x, vb)
    w = _bmm(x, kb * egc)
    intra = _bmm_nt(q, k) * decay
    s_all = state[...].reshape(GDN_PROMPT_BB * GDN_HEADS, GDN_D, GDN_D)
    v_new = u - _bmm(w, s_all)
    o = _bmm(q * egc, s_all) + _bmm(intra, v_new)
    g_last = gcol[:, blk - 1:blk, :]
    kd = k * jnp.exp(g_last - gcol)
    s_new = s_all * jnp.exp(g_last) + _bmm(jnp.swapaxes(kd, 1, 2), v_new)
    state[...] = s_new.reshape(state.shape)
    o = _rms(o, gn_ref[...])
    for bi in range(GDN_PROMPT_BB):
        z = z_ref[bi]
        for h in range(GDN_HEADS):
            zz = z[:, h * GDN_D:(h + 1) * GDN_D]
            y_ref[bi, :, h * GDN_D:(h + 1) * GDN_D] = (o[bi * GDN_HEADS + h] * (zz * _sigmoid(zz))).astype(y_ref.dtype)

    @pl.when(t == nt - 1)
    def _():
        s_out_ref[...] = state[...]


def _gdn_conv_and_gates(x_ref, kba_ref, cw_ref, alog_ref, dtb_ref, xbuf, row):
    blk = GDN_BLOCK
    hist = CONV_WIDTH - 1
    xbuf[SUBLANES:SUBLANES + blk, :] = x_ref[...]
    cw = cw_ref[...]
    conv = xbuf[SUBLANES - hist:SUBLANES - hist + blk, :] * cw[0:1, :]
    for j in range(1, CONV_WIDTH):
        conv = conv + xbuf[SUBLANES - hist + j:SUBLANES - hist + j + blk, :] * cw[j:j + 1, :]
    xbuf[SUBLANES - hist:SUBLANES, :] = xbuf[SUBLANES + blk - hist:SUBLANES + blk, :]
    c = conv * _sigmoid(conv)
    beta_all, g_all = _gate_values(kba_ref[...], alog_ref[...], dtb_ref[...])
    gc = g_all
    shift = 1
    while shift < blk:
        gc = gc + jnp.where(row >= shift, pltpu.roll(gc, shift, 0), 0.0)
        shift *= 2
    return c, beta_all, gc


def gdn_prompt(conv_in, z, kba, conv_w, alog_row, dtb_row, gdn_norm):
    b, t, _ = conv_in.shape
    bb = GDN_PROMPT_BB
    assert t % GDN_BLOCK == 0 and b % bb == 0
    nt = t // GDN_BLOCK
    y, s = pl.pallas_call(
        _gdn_prompt_kernel,
        out_shape=[jax.ShapeDtypeStruct((b, t, GDN_HEADS * GDN_D), BF16),
                   jax.ShapeDtypeStruct((b, GDN_HEADS, GDN_D, GDN_D), F32)],
        grid=(b // bb, nt),
        in_specs=[pl.BlockSpec((bb, GDN_BLOCK, CONV_CH), lambda i, j: (i, j, 0)),
                  pl.BlockSpec((bb, GDN_BLOCK, GDN_HEADS * GDN_D), lambda i, j: (i, j, 0)),
                  pl.BlockSpec((bb, GDN_BLOCK, LANES), lambda i, j: (i, j, 0)),
                  pl.BlockSpec((CONV_WIDTH, CONV_CH), lambda i, j: (0, 0)),
                  pl.BlockSpec((1, LANES), lambda i, j: (0, 0)),
                  pl.BlockSpec((1, LANES), lambda i, j: (0, 0)),
                  pl.BlockSpec((1, GDN_D), lambda i, j: (0, 0))],
        out_specs=[pl.BlockSpec((bb, GDN_BLOCK, GDN_HEADS * GDN_D), lambda i, j: (i, j, 0)),
                   pl.BlockSpec((bb, GDN_HEADS, GDN_D, GDN_D), lambda i, j: (i, 0, 0, 0))],
        scratch_shapes=[pltpu.VMEM((bb, SUBLANES + GDN_BLOCK, CONV_CH), F32),
                        pltpu.VMEM((bb, GDN_HEADS, GDN_D, GDN_D), F32)],
        compiler_params=_cparams("parallel", "arbitrary"),
        name="gdn_prompt",
    )(conv_in, z, kba, conv_w, alog_row, dtb_row, gdn_norm.reshape(1, -1))
    return y, s


GDN_SAMPLE_BB = 8


def _gdn_sample_kernel(x_ref, sc_ref, kba_ref, z_ref, s_ref, cw_ref, alog_ref, dtb_ref, gn_ref,
                       s_out_ref, sc_out_ref, y_ref, tbuf):
    bb = GDN_SAMPLE_BB
    x = x_ref[...]
    cw = cw_ref[...]
    conv = x * cw[CONV_WIDTH - 1:CONV_WIDTH, :]
    for j in range(CONV_WIDTH - 1):
        conv = conv + sc_ref[:, j, :] * cw[j:j + 1, :]
    for j in range(CONV_WIDTH - 2):
        sc_out_ref[:, j, :] = sc_ref[:, j + 1, :]
    sc_out_ref[:, CONV_WIDTH - 2, :] = x
    c = conv * _sigmoid(conv)
    beta_all, g_all = _gate_values(kba_ref[...], alog_ref[...], dtb_ref[...])
    eg_all = jnp.exp(g_all)
    z = z_ref[...]
    gn = gn_ref[...]
    nh = GDN_HEADS * GDN_D
    tbuf[...] = jnp.zeros(tbuf.shape, F32)
    for h in range(GDN_HEADS):
        q = _l2norm(c[:, h * GDN_D:(h + 1) * GDN_D]) * (GDN_D ** -0.5)
        k = _l2norm(c[:, nh + h * GDN_D:nh + (h + 1) * GDN_D])
        v = c[:, 2 * nh + h * GDN_D:2 * nh + (h + 1) * GDN_D]
        tbuf[0:bb, :] = q
        q_t = tbuf[...].T
        tbuf[0:bb, :] = k
        k_t = tbuf[...].T
        for b in range(bb):
            qcol = q_t[:, b:b + 1]
            kcol = k_t[:, b:b + 1]
            eg = eg_all[b:b + 1, DECAY_LANE + h:DECAY_LANE + h + 1]
            beta = beta_all[b:b + 1, BETA_LANE + h:BETA_LANE + h + 1]
            s1 = s_ref[b, h] * eg
            pred = jnp.sum(s1 * kcol, axis=0, keepdims=True)
            u = (v[b:b + 1, :] - pred) * beta
            s2 = s1 + kcol * u
            s_out_ref[b, h] = s2
            o = jnp.sum(s2 * qcol, axis=0, keepdims=True)
            zz = z[b:b + 1, h * GDN_D:(h + 1) * GDN_D]
            y_ref[b:b + 1, h * GDN_D:(h + 1) * GDN_D] = (_rms(o, gn) * (zz * _sigmoid(zz))).astype(y_ref.dtype)


def gdn_sample(conv_in, state_conv, kba, z, state_gdn, conv_w, alog_row, dtb_row, gdn_norm):
    b = conv_in.shape[0]
    bb = GDN_SAMPLE_BB
    assert b % bb == 0
    hist = CONV_WIDTH - 1
    return pl.pallas_call(
        _gdn_sample_kernel,
        out_shape=[jax.ShapeDtypeStruct(state_gdn.shape, F32),
                   jax.ShapeDtypeStruct(state_conv.shape, F32),
                   jax.ShapeDtypeStruct((b, GDN_HEADS * GDN_D), BF16)],
        grid=(b // bb,),
        in_specs=[pl.BlockSpec((bb, CONV_CH), lambda i: (i, 0)),
                  pl.BlockSpec((bb, hist, CONV_CH), lambda i: (i, 0, 0)),
                  pl.BlockSpec((bb, LANES), lambda i: (i, 0)),
                  pl.BlockSpec((bb, GDN_HEADS * GDN_D), lambda i: (i, 0)),
                  pl.BlockSpec((bb, GDN_HEADS, GDN_D, GDN_D), lambda i: (i, 0, 0, 0)),
                  pl.BlockSpec((CONV_WIDTH, CONV_CH), lambda i: (0, 0)),
                  pl.BlockSpec((1, LANES), lambda i: (0, 0)),
                  pl.BlockSpec((1, LANES), lambda i: (0, 0)),
                  pl.BlockSpec((1, GDN_D), lambda i: (0, 0))],
        out_specs=[pl.BlockSpec((bb, GDN_HEADS, GDN_D, GDN_D), lambda i: (i, 0, 0, 0)),
                   pl.BlockSpec((bb, hist, CONV_CH), lambda i: (i, 0, 0)),
                   pl.BlockSpec((bb, GDN_HEADS * GDN_D), lambda i: (i, 0))],
        scratch_shapes=[pltpu.VMEM((GDN_D, GDN_D), F32)],
        compiler_params=_cparams("parallel"),
        name="gdn_sample",
    )(conv_in, state_conv, kba, z, state_gdn, conv_w, alog_row, dtb_row, gdn_norm.reshape(1, -1))


def _rope128(x, cos, sin):
    half = MLA_ROPE // 2
    lane = lax.broadcasted_iota(I32, x.shape, 1)
    swapped = jnp.where(lane < half, pltpu.roll(x, LANES - half, 1), pltpu.roll(x, half, 1))
    return x * cos + swapped * sin


def _mla_prep_kernel(cq_ref, ckv_ref, kba_ref, cos_ref, sin_ref, qn_ref, kvn_ref, wuq_ref, wuk_ref, wuv_ref,
                     q_ref, k_ref, v_ref, lat_ref, kpe_ref):
    cos = cos_ref[...]
    sin = sin_ref[...]
    qn = _rms(cq_ref[...], qn_ref[...]).astype(BF16)
    lat = _rms(ckv_ref[...], kvn_ref[...])
    lat_ref[...] = lat
    lat_b = lat.astype(BF16)
    kpe = _rope128(kba_ref[...], cos, sin)
    kpe_ref[...] = kpe[:, :MLA_ROPE]
    for h in range(MLA_HEADS):
        lo = h * MLA_QK_PAD
        q_ref[:, lo:lo + MLA_NOPE] = jnp.dot(
            qn, wuq_ref[:, lo:lo + MLA_NOPE], preferred_element_type=F32).astype(q_ref.dtype)
        q_pe = jnp.dot(qn, wuq_ref[:, lo + MLA_NOPE:lo + MLA_QK_PAD], preferred_element_type=F32)
        q_ref[:, lo + MLA_NOPE:lo + MLA_QK_PAD] = _rope128(q_pe, cos, sin).astype(q_ref.dtype)
        k_ref[:, lo:lo + MLA_NOPE] = jnp.dot(
            lat_b, wuk_ref[:, h * MLA_NOPE:(h + 1) * MLA_NOPE], preferred_element_type=F32).astype(k_ref.dtype)
        k_ref[:, lo + MLA_NOPE:lo + MLA_QK_PAD] = kpe.astype(k_ref.dtype)
    v_ref[...] = jnp.dot(lat_b, wuv_ref[...], preferred_element_type=F32).astype(v_ref.dtype)


def mla_prep(c_q, c_kv, kba, cos_tab, sin_tab, q_norm, kv_norm, wuq_p, wuk, wuv, *, seq, q_dtype, tm=256):
    m = c_q.shape[0]
    tm = min(tm, m, seq)
    assert m % tm == 0 and seq % tm == 0
    nseq = seq // tm
    hq = MLA_HEADS * MLA_QK_PAD
    row = lambda i: (i, 0)
    const = lambda i: (0, 0)
    return pl.pallas_call(
        _mla_prep_kernel,
        out_shape=[jax.ShapeDtypeStruct((m, hq), q_dtype),
                   jax.ShapeDtypeStruct((m, hq), BF16),
                   jax.ShapeDtypeStruct((m, MLA_HEADS * MLA_V), BF16),
                   jax.ShapeDtypeStruct((m, MLA_KV_RANK), F32),
                   jax.ShapeDtypeStruct((m, MLA_ROPE), F32)],
        grid=(m // tm,),
        in_specs=[pl.BlockSpec((tm, MLA_Q_RANK), row),
                  pl.BlockSpec((tm, MLA_KV_RANK), row),
                  pl.BlockSpec((tm, LANES), row),
                  pl.BlockSpec((tm, LANES), lambda i: (i % nseq, 0)),
                  pl.BlockSpec((tm, LANES), lambda i: (i % nseq, 0)),
                  pl.BlockSpec((1, MLA_Q_RANK), const),
                  pl.BlockSpec((1, MLA_KV_RANK), const),
                  pl.BlockSpec(wuq_p.shape, const),
                  pl.BlockSpec(wuk.shape, const),
                  pl.BlockSpec(wuv.shape, const)],
        out_specs=[pl.BlockSpec((tm, hq), row),
                   pl.BlockSpec((tm, hq), row),
                   pl.BlockSpec((tm, MLA_HEADS * MLA_V), row),
                   pl.BlockSpec((tm, MLA_KV_RANK), row),
                   pl.BlockSpec((tm, MLA_ROPE), row)],
        compiler_params=_cparams("parallel"),
        name="mla_prep",
    )(c_q, c_kv, kba, cos_tab, sin_tab, q_norm.reshape(1, -1), kv_norm.reshape(1, -1), wuq_p, wuk, wuv)


def _flash_kernel(q_ref, k_ref, v_ref, o_ref, m_s, l_s, acc_s, *, tq):
    qi = pl.program_id(2)
    m_s[...] = jnp.full(m_s.shape, NEG_BIG, F32)
    l_s[...] = jnp.zeros(l_s.shape, F32)
    acc_s[...] = jnp.zeros(acc_s.shape, F32)

    def block(j, masked):
        k0 = pl.multiple_of(j * tq, tq)
        for h in range(FLASH_HEADS):
            q = q_ref[0, :, h * MLA_QK_PAD:(h + 1) * MLA_QK_PAD]
            s = lax.dot_general(q, k_ref[0, pl.ds(k0, tq), h * MLA_QK_PAD:(h + 1) * MLA_QK_PAD],
                                (((1,), (1,)), ((), ())), preferred_element_type=F32) * MLA_SCALE
            if masked:
                s = jnp.where(lax.broadcasted_iota(I32, (tq, tq), 1) <= lax.broadcasted_iota(I32, (tq, tq), 0),
                              s, NEG_BIG)
            m_prev = m_s[h]
            m_new = jnp.maximum(m_prev, jnp.max(s, axis=-1, keepdims=True))
            corr = jnp.exp(m_prev - m_new)
            p = jnp.exp(s - m_new)
            l_s[h] = l_s[h] * corr + jnp.sum(p, axis=-1, keepdims=True)
            acc_s[h] = acc_s[h] * corr + jnp.dot(p.astype(BF16), v_ref[0, pl.ds(k0, tq), h * MLA_V:(h + 1) * MLA_V],
                                                 preferred_element_type=F32)
            m_s[h] = m_new

    def below_diagonal(j, _):
        block(j, masked=False)
        return 0

    lax.fori_loop(0, qi, below_diagonal, 0)
    block(qi, masked=True)
    for h in range(FLASH_HEADS):
        o_ref[0, :, h * MLA_V:(h + 1) * MLA_V] = (acc_s[h] / l_s[h]).astype(o_ref.dtype)


FLASH_HEADS = 2


def mla_flash(q, k, v, *, tq=512):
    b, t, _ = q.shape
    tq = min(tq, t)
    hs = FLASH_HEADS
    assert t % tq == 0 and MLA_HEADS % hs == 0
    seq_map = lambda bi, h, qi: (bi, 0, h)
    return pl.pallas_call(
        functools.partial(_flash_kernel, tq=tq),
        out_shape=jax.ShapeDtypeStruct((b, t, MLA_HEADS * MLA_V), BF16),
        grid=(b, MLA_HEADS // hs, t // tq),
        in_specs=[pl.BlockSpec((1, tq, hs * MLA_QK_PAD), lambda bi, h, qi: (bi, qi, h)),
                  pl.BlockSpec((1, t, hs * MLA_QK_PAD), seq_map),
                  pl.BlockSpec((1, t, hs * MLA_V), seq_map)],
        out_specs=pl.BlockSpec((1, tq, hs * MLA_V), lambda bi, h, qi: (bi, qi, h)),
        scratch_shapes=[pltpu.VMEM((hs, tq, 1), F32), pltpu.VMEM((hs, tq, 1), F32), pltpu.VMEM((hs, tq, MLA_V), F32)],
        compiler_params=_cparams("parallel", "parallel", "arbitrary"),
        name="mla_flash",
    )(q, k, v)


DEC_HEAD_PAD = 8
DEC_GROUP = 16
DEC_SLOTS = 4


def _decode_kernel(pt_ref, ptn_ref, qlat_ref, qpe_ref, latn_ref, kpen_ref, wuv_ref, lat_hbm, kpe_hbm,
                   o_ref, latbuf, kpebuf, sem, *, nb, n_pages, page):
    b = pl.program_id(0)
    grp = DEC_GROUP
    ns = DEC_SLOTS
    n_groups = n_pages // grp
    n_outer = n_groups // ns

    def page_copies(pg, slot, j):
        return (pltpu.make_async_copy(lat_hbm.at[pg, 0], latbuf.at[slot, j], sem.at[0, slot]),
                pltpu.make_async_copy(kpe_hbm.at[pg, 0], kpebuf.at[slot, j], sem.at[1, slot]))

    def start_group(tbl_ref, g, slot):
        for j in range(grp):
            for cp in page_copies(tbl_ref[0, 0, g * grp + j], slot, j):
                cp.start()

    def wait_group(slot):
        for j in range(grp):
            for cp in page_copies(0, slot, j):
                cp.wait()

    @pl.when(b == 0)
    def _():
        for s in range(ns - 1):
            start_group(pt_ref, s, s)

    qlat = qlat_ref[0]
    qpe = qpe_ref[0]
    latn = latn_ref[0]
    kpen = kpen_ref[0]
    qlat_b = qlat.astype(BF16)
    qpe_b = qpe.astype(BF16)
    s_new = (jnp.sum(qlat * latn, axis=-1, keepdims=True)
             + jnp.sum(qpe * kpen, axis=-1, keepdims=True)) * MLA_SCALE
    m0 = s_new
    l0 = jnp.ones_like(s_new)
    acc0 = jnp.broadcast_to(latn, qlat.shape)

    def consume(slot, carry):
        m, l, acc = carry
        lat = latbuf[slot].reshape(grp * page, MLA_KV_RANK).astype(BF16)
        kpe_t = jnp.concatenate([kpebuf[slot, j] for j in range(grp)], axis=1).astype(BF16)
        s = (lax.dot_general(qlat_b, lat, (((1,), (1,)), ((), ())), preferred_element_type=F32)
             + jnp.dot(qpe_b, kpe_t, preferred_element_type=F32)) * MLA_SCALE
        m_new = jnp.maximum(m, jnp.max(s, axis=-1, keepdims=True))
        corr = jnp.exp(m - m_new)
        p = jnp.exp(s - m_new)
        l = l * corr + jnp.sum(p, axis=-1, keepdims=True)
        acc = acc * corr + jnp.dot(p.astype(BF16), lat, preferred_element_type=F32)
        return m_new, l, acc

    def ring_body(i, carry):
        for j in range(ns):
            ahead = (j + ns - 1) % ns
            if j == 0:
                start_group(pt_ref, i * ns + ns - 1, ahead)
            else:
                @pl.when(i + 1 < n_outer)
                def _():
                    start_group(pt_ref, (i + 1) * ns + ahead, ahead)

                @pl.when(jnp.logical_and(i + 1 >= n_outer, b + 1 < nb))
                def _():
                    start_group(ptn_ref, ahead, ahead)
            wait_group(j)
            carry = consume(j, carry)
        return carry

    m, l, acc = lax.fori_loop(0, n_outer, ring_body, (m0, l0, acc0))
    o_lat = (acc / l).astype(BF16)
    res = jnp.dot(o_lat, wuv_ref[...], preferred_element_type=F32)
    o_ref[0] = jnp.concatenate(
        [res[h:h + 1, h * MLA_V:(h + 1) * MLA_V] for h in range(MLA_HEADS)], axis=1).astype(o_ref.dtype)


def mla_decode(page_table, qlat8, qpe8, lat_new, kpe_new, wuv, cache_latent, cache_krope_t):
    b, n_pages = page_table.shape
    page = cache_latent.shape[2]
    assert n_pages % (DEC_SLOTS * DEC_GROUP) == 0
    pt3 = page_table.reshape(b, 1, n_pages)
    smem_row = lambda f: pl.BlockSpec((1, 1, n_pages), f, memory_space=pltpu.SMEM)
    return pl.pallas_call(
        functools.partial(_decode_kernel, nb=b, n_pages=n_pages, page=page),
        out_shape=jax.ShapeDtypeStruct((b, 1, MLA_HEADS * MLA_V), BF16),
        grid=(b,),
        in_specs=[smem_row(lambda i: (i, 0, 0)),
                  smem_row(lambda i: (jnp.minimum(i + 1, b - 1), 0, 0)),
                  pl.BlockSpec((1, DEC_HEAD_PAD, MLA_KV_RANK), lambda i: (i, 0, 0)),
                  pl.BlockSpec((1, DEC_HEAD_PAD, MLA_ROPE), lambda i: (i, 0, 0)),
                  pl.BlockSpec((1, 1, MLA_KV_RANK), lambda i: (i, 0, 0)),
                  pl.BlockSpec((1, 1, MLA_ROPE), lambda i: (i, 0, 0)),
                  pl.BlockSpec(wuv.shape, lambda i: (0, 0)),
                  pl.BlockSpec(memory_space=pl.ANY),
                  pl.BlockSpec(memory_space=pl.ANY)],
        out_specs=pl.BlockSpec((1, 1, MLA_HEADS * MLA_V), lambda i: (i, 0, 0)),
        scratch_shapes=[pltpu.VMEM((DEC_SLOTS, DEC_GROUP, page, MLA_KV_RANK), F32),
                        pltpu.VMEM((DEC_SLOTS, DEC_GROUP, MLA_ROPE, page), F32),
                        pltpu.SemaphoreType.DMA((2, DEC_SLOTS))],
        compiler_params=_cparams("arbitrary"),
        name="mla_decode",
    )(pt3, pt3, qlat8, qpe8, lat_new.reshape(b, 1, -1), kpe_new.reshape(b, 1, -1), wuv,
      cache_latent, cache_krope_t)


def _softmax_rows(s):
    m = jnp.max(s, axis=-1, keepdims=True)
    p = jnp.exp(s - m)
    return p / jnp.sum(p, axis=-1, keepdims=True)


def _xattn_prompt_kernel(q_ref, k_ref, v_ref, o_ref, *, dh):
    scale = dh ** -0.5
    for h in range(X_HEADS):
        q = q_ref[0, :, h * dh:(h + 1) * dh]
        s = lax.dot_general(q, k_ref[0, :, h * dh:(h + 1) * dh], (((1,), (1,)), ((), ())),
                            preferred_element_type=F32) * scale
        p = _softmax_rows(s).astype(BF16)
        o_ref[0, :, h * dh:(h + 1) * dh] = jnp.dot(
            p, v_ref[0, :, h * dh:(h + 1) * dh], preferred_element_type=F32).astype(o_ref.dtype)


def xattn_prompt(q, k, v, *, tq=512):
    b, t, d = q.shape
    mem = k.shape[1]
    tq = min(tq, t)
    return pl.pallas_call(
        functools.partial(_xattn_prompt_kernel, dh=d // X_HEADS),
        out_shape=jax.ShapeDtypeStruct((b, t, d), BF16),
        grid=(b, t // tq),
        in_specs=[pl.BlockSpec((1, tq, d), lambda i, j: (i, j, 0)),
                  pl.BlockSpec((1, mem, d), lambda i, j: (i, 0, 0)),
                  pl.BlockSpec((1, mem, d), lambda i, j: (i, 0, 0))],
        out_specs=pl.BlockSpec((1, tq, d), lambda i, j: (i, j, 0)),
        compiler_params=_cparams("parallel", "parallel"),
        name="xattn_prompt",
    )(q, k, v)


XATTN_SAMPLE_BB = 2


def _xattn_sample_kernel(q_ref, k_ref, v_ref, o_ref, *, dh):
    scale = dh ** -0.5
    for b in range(XATTN_SAMPLE_BB):
        q = q_ref[b].astype(F32)
        s = jnp.sum(k_ref[b] * q[None], axis=-1, keepdims=True) * scale
        m = jnp.max(s, axis=0, keepdims=True)
        p = jnp.exp(s - m)
        denom = jnp.sum(p, axis=0, keepdims=True)
        o = jnp.sum(p * v_ref[b], axis=0) / denom[0]
        o_ref[b] = o.astype(o_ref.dtype)


def xattn_sample(q, k, v):
    b, heads, dh = q.shape
    mem = k.shape[1]
    bb = XATTN_SAMPLE_BB
    assert b % bb == 0
    kv_spec = pl.BlockSpec((bb, mem, heads, dh), lambda i: (i, 0, 0, 0))
    return pl.pallas_call(
        functools.partial(_xattn_sample_kernel, dh=dh),
        out_shape=jax.ShapeDtypeStruct((b, heads, dh), BF16),
        grid=(b // bb,),
        in_specs=[pl.BlockSpec((bb, heads, dh), lambda i: (i, 0, 0)), kv_spec, kv_spec],
        out_specs=pl.BlockSpec((bb, heads, dh), lambda i: (i, 0, 0)),
        compiler_params=_cparams("parallel"),
        name="xattn_sample",
    )(q, k, v)


def _post_xattn_kernel(ox_ref, wxo_ref, h_ref, g_ref, wr_hi_ref, wr_lo_ref, br_ref, cnt_in_ref,
                       h2_ref, hn_ref, gate_ref, idx_ref, rank_ref, cnt_ref, cnt_s):
    @pl.when(pl.program_id(0) == 0)
    def _():
        cnt_s[...] = cnt_in_ref[...]

    h2 = h_ref[...] + jnp.dot(ox_ref[...], wxo_ref[...], preferred_element_type=F32)
    h2_ref[...] = h2
    hn = _rms(h2, g_ref[...])
    hn_ref[...] = hn.reshape(hn_ref.shape)
    hn_hi = hn.astype(BF16)
    hn_lo = (hn - hn_hi.astype(F32)).astype(BF16)
    logits = (jnp.dot(hn_hi, wr_hi_ref[...], preferred_element_type=F32)
              + jnp.dot(hn_hi, wr_lo_ref[...], preferred_element_type=F32)
              + jnp.dot(hn_lo, wr_hi_ref[...], preferred_element_type=F32)) + br_ref[...]
    lane = lax.broadcasted_iota(I32, logits.shape, 1)
    logits = jnp.where(lane < N_EXPERTS, logits, NEG_BIG)
    vals, picks = [], []
    gates = jnp.zeros(logits.shape, F32)
    ids = jnp.zeros(logits.shape, I32)
    member = jnp.zeros(logits.shape, F32)
    for k in range(TOP_K):
        m = jnp.max(logits, axis=-1, keepdims=True)
        idx = jnp.min(jnp.where(logits == m, lane, LANES), axis=-1, keepdims=True)
        vals.append(m)
        picks.append(idx)
        ids = jnp.where(lane == k, idx, ids)
        member = jnp.where(lane == idx, 1.0, member)
        logits = jnp.where(lane == idx, NEG_BIG, logits)
    exps = [jnp.exp(v - vals[0]) for v in vals]
    denom = exps[0]
    for e in exps[1:]:
        denom = denom + e
    for k in range(TOP_K):
        gates = jnp.where(lane == k, exps[k] / denom, gates)
    gate_ref[...] = gates
    idx_ref[...] = ids
    tm = logits.shape[0]
    before = (lax.broadcasted_iota(I32, (tm, tm), 0) > lax.broadcasted_iota(I32, (tm, tm), 1)).astype(BF16)
    prior = jnp.dot(before, member.astype(BF16), preferred_element_type=F32) + cnt_s[...]
    ranks = jnp.zeros(logits.shape, I32)
    for k in range(TOP_K):
        rk = jnp.sum(jnp.where(lane == picks[k], prior, 0.0), axis=-1, keepdims=True)
        ranks = jnp.where(lane == k, rk.astype(I32), ranks)
    rank_ref[...] = ranks
    cnt = cnt_s[...] + jnp.sum(member, axis=0, keepdims=True)
    cnt_s[...] = cnt
    cnt_ref[...] = cnt


def post_xattn(ox, wxo, h, norm_ffn, wr_hi, wr_lo, br_row, cnt_in, *, tm=256):
    m, d = h.shape
    tm = min(tm, m)
    row = lambda i: (i, 0)
    const = lambda i: (0, 0)
    return pl.pallas_call(
        _post_xattn_kernel,
        out_shape=[jax.ShapeDtypeStruct((m, d), F32), jax.ShapeDtypeStruct((m, d // LANES, LANES), F32),
                   jax.ShapeDtypeStruct((m, LANES), F32), jax.ShapeDtypeStruct((m, LANES), I32),
                   jax.ShapeDtypeStruct((m, LANES), I32), jax.ShapeDtypeStruct((1, LANES), F32)],
        grid=(m // tm,),
        in_specs=[pl.BlockSpec((tm, d), row), pl.BlockSpec(wxo.shape, const), pl.BlockSpec((tm, d), row),
                  pl.BlockSpec((1, d), const), pl.BlockSpec(wr_hi.shape, const), pl.BlockSpec(wr_lo.shape, const),
                  pl.BlockSpec((1, LANES), const), pl.BlockSpec((1, LANES), const)],
        out_specs=[pl.BlockSpec((tm, d), row), pl.BlockSpec((tm, d // LANES, LANES), lambda i: (i, 0, 0)),
                   pl.BlockSpec((tm, LANES), row), pl.BlockSpec((tm, LANES), row),
                   pl.BlockSpec((tm, LANES), row), pl.BlockSpec((1, LANES), const)],
        scratch_shapes=[pltpu.VMEM((1, LANES), F32)],
        compiler_params=_cparams("arbitrary"),
        name="post_xattn_router",
    )(ox, wxo, h, norm_ffn.reshape(1, -1), wr_hi, wr_lo, br_row, cnt_in)


MOE_TILE = 256


ROW_TILE = 128
MOE_FF_CHUNK = 512


def _row_dma_loop(n_tokens, make_copies, wait):
    def body(n, _):
        for k, cp in enumerate(make_copies(n)):
            cp.wait() if wait else cp.start(priority=k % 2)
        return 0
    lax.fori_loop(0, n_tokens, body, 0, unroll=4)


def _dispatch_kernel(dest_ref, x_ref, xs_in_ref, xs_ref, buf, sem, *, nt):
    del xs_in_ref
    t = pl.program_id(0)
    tm = ROW_TILE
    slot = t % 2

    def copies(s, dest_of):
        def make(n):
            return [pltpu.make_async_copy(buf.at[s, n], xs_ref.at[dest_of(n, k)], sem.at[s]) for k in range(TOP_K)]
        return make

    def wait_slot(s):
        _row_dma_loop(tm, copies(s, lambda n, k: 0), wait=True)

    if nt > 2:
        @pl.when(t >= 2)
        def _():
            wait_slot(slot)

    buf[slot] = x_ref[...]
    _row_dma_loop(tm, copies(slot, lambda n, k: dest_ref[0, 0, n * TOP_K + k]), wait=False)

    @pl.when(t == nt - 1)
    def _():
        wait_slot(slot)
        if nt > 1:
            wait_slot(1 - slot)


def moe_dispatch(dest, x3, xs):
    m = x3.shape[0]
    tm = ROW_TILE
    assert m % tm == 0
    n_tiles = m // tm
    return pl.pallas_call(
        functools.partial(_dispatch_kernel, nt=n_tiles),
        out_shape=jax.ShapeDtypeStruct(xs.shape, xs.dtype),
        grid=(n_tiles,),
        in_specs=[pl.BlockSpec((1, 1, TOP_K * tm), lambda t: (t, 0, 0), memory_space=pltpu.SMEM),
                  pl.BlockSpec((tm,) + x3.shape[1:], lambda t: (t, 0, 0)),
                  pl.BlockSpec(memory_space=pl.ANY)],
        out_specs=pl.BlockSpec(memory_space=pl.ANY),
        scratch_shapes=[pltpu.VMEM((2, tm) + x3.shape[1:], x3.dtype), pltpu.SemaphoreType.DMA((2,))],
        input_output_aliases={2: 0},
        compiler_params=_cparams("arbitrary"),
        name="moe_dispatch",
    )(dest.reshape(n_tiles, 1, TOP_K * tm), x3, xs)


def _moe_kernel(te_ref, tv_ref, x_ref, w1_ref, b1_ref, w2_ref, b2_ref, y_ref, w1b, w2s, w2p_ref, act_s):
    t = pl.program_id(0)

    @pl.when(jnp.logical_or(t == 0, te_ref[t] != te_ref[jnp.maximum(t - 1, 0)]))
    def _():
        w1b[...] = w1_ref[0].astype(BF16)
        half = LANES // 2
        n_chunks = w2s.shape[0]
        for c in range(n_chunks):
            for blk in range(w2s.shape[1] // LANES):
                r0 = blk * LANES
                w2s[c, pl.ds(r0, half, stride=2), :] = w2_ref[0, r0:r0 + half, c * LANES:(c + 1) * LANES]
                w2s[c, pl.ds(r0 + 1, half, stride=2), :] = w2_ref[0, r0 + half:r0 + LANES, c * LANES:(c + 1) * LANES]
        w2p_ref[...] = jnp.concatenate([w2s[c] for c in range(n_chunks)], axis=1).astype(BF16)

    @pl.when(tv_ref[t] > 0)
    def _():
        tm = x_ref.shape[0]
        x = x_ref[...].reshape(tm, w1b.shape[0]).astype(BF16)
        ff2 = w1b.shape[1]
        even = lax.broadcasted_iota(I32, (x.shape[0], LANES), 1) % 2 == 0
        for c in range(ff2 // MOE_FF_CHUNK):
            lo = c * MOE_FF_CHUNK
            hh = jnp.dot(x, w1b[:, lo:lo + MOE_FF_CHUNK], preferred_element_type=F32) + b1_ref[0, :, lo:lo + MOE_FF_CHUNK]
            parts = []
            for j in range(MOE_FF_CHUNK // (2 * LANES)):
                a = hh[:, (2 * j) * LANES:(2 * j + 1) * LANES]
                b = hh[:, (2 * j + 1) * LANES:(2 * j + 2) * LANES]
                glu = jnp.where(even, a, pltpu.roll(b, 1, 1))
                lin = jnp.where(even, pltpu.roll(a, LANES - 1, 1), b)
                glu = jnp.minimum(glu, SWIGLU_LIMIT)
                lin = jnp.clip(lin, -SWIGLU_LIMIT, SWIGLU_LIMIT)
                parts.append((lin + 1.0) * (glu * _sigmoid(SWIGLU_ALPHA * glu)))
            act_s[:, lo // 2:(lo + MOE_FF_CHUNK) // 2] = jnp.concatenate(parts, axis=1).astype(BF16)
        y = jnp.dot(act_s[...], w2p_ref[...], preferred_element_type=F32) + b2_ref[0]
        y_ref[...] = y.reshape(y_ref.shape)

    @pl.when(tv_ref[t] == 0)
    def _():
        y_ref[...] = jnp.zeros(y_ref.shape, F32)


def moe_experts(tile_expert, tile_valid, xs, w1, b1, w2, b2):
    n_tiles = tile_expert.shape[0]
    tm = MOE_TILE
    d = w1.shape[1]
    ff2 = w1.shape[2]
    assert ff2 % MOE_FF_CHUNK == 0 and MOE_FF_CHUNK % (2 * LANES) == 0 and xs.shape[1:] == (d // LANES, LANES)
    wmap = lambda t, te, tv: (te[t], 0, 0)
    row_tiles = pl.BlockSpec((tm, d // LANES, LANES), lambda t, te, tv: (t, 0, 0))
    grid_spec = pltpu.PrefetchScalarGridSpec(
        num_scalar_prefetch=2,
        grid=(n_tiles,),
        in_specs=[row_tiles,
                  pl.BlockSpec((1, d, ff2), wmap), pl.BlockSpec((1, 1, ff2), wmap),
                  pl.BlockSpec((1, ff2 // 2, d), wmap), pl.BlockSpec((1, 1, d), wmap)],
        out_specs=row_tiles,
        scratch_shapes=[pltpu.VMEM((d, ff2), BF16), pltpu.VMEM((d // LANES, ff2 // 2, LANES), F32),
                        pltpu.VMEM((ff2 // 2, d), BF16), pltpu.VMEM((tm, ff2 // 2), BF16)],
    )
    return pl.pallas_call(
        _moe_kernel,
        out_shape=jax.ShapeDtypeStruct(xs.shape, F32),
        grid_spec=grid_spec,
        compiler_params=pltpu.CompilerParams(dimension_semantics=("arbitrary",), vmem_limit_bytes=MOE_VMEM_LIMIT),
        name="moe_experts",
    )(tile_expert, tile_valid, xs, w1, b1, w2, b2)


def _combine_kernel(dest_ref, destn_ref, y_hbm, h2_ref, gate_ref, gain_ref, o_ref, ybuf, sem, *, nt):
    t = pl.program_id(0)
    tm = ROW_TILE
    slot = t % 2

    def copies(s, dest_of):
        def make(n):
            return [pltpu.make_async_copy(y_hbm.at[dest_of(n, k)], ybuf.at[s, k * tm + n], sem.at[s])
                    for k in range(TOP_K)]
        return make

    @pl.when(t == 0)
    def _():
        _row_dma_loop(tm, copies(0, lambda n, k: dest_ref[0, 0, n * TOP_K + k]), wait=False)

    if nt > 1:
        @pl.when(t + 1 < nt)
        def _():
            _row_dma_loop(tm, copies(1 - slot, lambda n, k: destn_ref[0, 0, n * TOP_K + k]), wait=False)

    _row_dma_loop(tm, copies(slot, lambda n, k: 0), wait=True)
    gates = gate_ref[...]
    acc = h2_ref[...]
    for k in range(TOP_K):
        yk = ybuf[slot, k * tm:(k + 1) * tm].reshape(acc.shape)
        acc = acc + gates[:, k:k + 1] * yk
    o_ref[...] = _rms(acc, gain_ref[...])


def moe_combine(dest, y_sorted, h2, gates, norm_final):
    m, d = h2.shape
    tm = ROW_TILE
    assert m % tm == 0
    n_tiles = m // tm
    row = lambda t: (t, 0)
    dest3 = dest.reshape(n_tiles, 1, TOP_K * tm)
    return pl.pallas_call(
        functools.partial(_combine_kernel, nt=n_tiles),
        out_shape=jax.ShapeDtypeStruct((m, d), F32),
        grid=(n_tiles,),
        in_specs=[pl.BlockSpec((1, 1, TOP_K * tm), lambda t: (t, 0, 0), memory_space=pltpu.SMEM),
                  pl.BlockSpec((1, 1, TOP_K * tm), lambda t: (jnp.minimum(t + 1, n_tiles - 1), 0, 0),
                               memory_space=pltpu.SMEM),
                  pl.BlockSpec(memory_space=pl.ANY),
                  pl.BlockSpec((tm, d), row), pl.BlockSpec((tm, LANES), row),
                  pl.BlockSpec((1, d), lambda t: (0, 0))],
        out_specs=pl.BlockSpec((tm, d), row),
        scratch_shapes=[pltpu.VMEM((2, TOP_K * tm, d // LANES, LANES), F32), pltpu.SemaphoreType.DMA((2,))],
        compiler_params=_cparams("arbitrary"),
        name="moe_combine",
    )(dest3, dest3, y_sorted, h2, gates, norm_final.reshape(1, -1))


def _moe_plan(counts_row, n_assign):
    tm = MOE_TILE
    n_tiles = (n_assign + N_EXPERTS * (tm - 1) + tm - 1) // tm
    counts = counts_row[0, :N_EXPERTS].astype(I32)
    tiles_per = (counts + tm - 1) // tm
    tile_end = jnp.cumsum(tiles_per)
    pad_start = (tile_end - tiles_per) * tm
    tile_ids = jnp.arange(n_tiles, dtype=I32)
    used = tile_end[-1]
    tile_valid = (tile_ids < used).astype(I32)
    clamped = jnp.minimum(tile_ids, used - 1)
    tile_expert = jnp.sum((clamped[:, None] >= tile_end[None, :]).astype(I32), axis=1)
    return n_tiles, jnp.minimum(tile_expert, N_EXPERTS - 1), tile_valid, pad_start


def _dest_rows(ids, ranks, pad_start):
    sel = ids[:, :TOP_K, None] == jnp.arange(N_EXPERTS, dtype=I32)
    return ranks[:, :TOP_K] + jnp.sum(jnp.where(sel, pad_start, 0), axis=-1)


def _rope_tables(positions):
    half = MLA_ROPE // 2
    inv_freq = 1.0 / (ROPE_THETA ** (jnp.arange(half, dtype=F32) / half))
    ang = positions.astype(F32)[:, None] * inv_freq[None, :]
    cos, sin = jnp.cos(ang), jnp.sin(ang)
    zeros = jnp.zeros((positions.shape[0], LANES - MLA_ROPE), F32)
    return (jnp.concatenate([cos, cos, zeros], axis=1), jnp.concatenate([-sin, sin, zeros], axis=1))


def _in_proj_weight(w_in):
    o = np.cumsum([0, CONV_CH, GDN_HEADS * GDN_D, GDN_HEADS, GDN_HEADS, MLA_Q_RANK, MLA_KV_RANK, MLA_ROPE])
    conv, z, bl, al, cq, ckv, kpe = (w_in[:, o[i]:o[i + 1]] for i in range(7))
    pad = jnp.zeros((w_in.shape[0], LANES - MLA_ROPE - 2 * GDN_HEADS), w_in.dtype)
    return jnp.concatenate([conv, z, cq, ckv, kpe, bl, al, pad], axis=1).astype(BF16)


IN_SPLITS = (CONV_CH, GDN_HEADS * GDN_D, MLA_Q_RANK, MLA_KV_RANK, LANES)


def _lane_row(vals, lane0):
    return jnp.zeros((1, LANES), F32).at[0, lane0:lane0 + vals.shape[0]].set(vals.astype(F32))


def kernel(x_prompt, x_sample, state_gdn, state_conv, cache_latent, cache_krope, page_table, cache_mem_k, cache_mem_v, mem_prompt, norm_mix, w_in, conv_w, a_log, dt_bias, gdn_norm, q_norm, w_uq, kv_norm, w_uk, w_uv, w_out, norm_x, mem_norm, w_xq, w_xk, w_xv, w_xo, norm_ffn, w_router, b_router, w_e1, b_e1, w_e2, b_e2, norm_final):
    depth = w_in.shape[0]
    assert depth == 1
    L = 0
    bp, t, d = x_prompt.shape
    bs = x_sample.shape[0]
    assert x_sample.shape[1] == 1
    past_len = page_table.shape[1] * cache_latent.shape[2]
    mp = bp * t

    w_in_p = _in_proj_weight(w_in[L])
    alog_row = _lane_row(a_log[L], DECAY_LANE)
    dtb_row = _lane_row(dt_bias[L], DECAY_LANE)
    wuq = w_uq[L]
    wuq_p = jnp.concatenate(
        [wuq, jnp.zeros(wuq.shape[:2] + (MLA_QK_PAD - wuq.shape[2],), wuq.dtype)], axis=2
    ).reshape(wuq.shape[0], MLA_HEADS * MLA_QK_PAD).astype(BF16)
    wuk = w_uk[L].reshape(MLA_KV_RANK, MLA_HEADS * MLA_NOPE).astype(BF16)
    wuv = w_uv[L].reshape(MLA_KV_RANK, MLA_HEADS * MLA_V).astype(BF16)
    wuk_t = jnp.transpose(w_uk[L], (1, 2, 0)).astype(BF16)
    w_out_b = w_out[L].astype(BF16)
    n_gdn = GDN_HEADS * GDN_D
    w_xq_b, w_xo_b = w_xq[L].astype(BF16), w_xo[L].astype(BF16)
    w_xkv_b = jnp.concatenate([w_xk[L], w_xv[L]], axis=1).astype(BF16)
    wr = jnp.concatenate([w_router[L], jnp.zeros((d, LANES - N_EXPERTS), F32)], axis=1)
    wr_hi = wr.astype(BF16)
    wr_lo = (wr - wr_hi.astype(F32)).astype(BF16)
    br_row = _lane_row(b_router[L], 0)
    b1 = b_e1[L][:, None, :]
    b2 = b_e2[L][:, None, :]
    cos_p, sin_p = _rope_tables(jnp.arange(t, dtype=I32))
    cos_s, sin_s = _rope_tables(jnp.full((bs,), past_len, I32))

    xp = x_prompt.reshape(mp, d)
    xs = x_sample.reshape(bs, d)
    conv_p, z_p, cq_p, ckv_p, kba_p = fused_linear([xp], [w_in_p], gain=norm_mix[L], splits=IN_SPLITS, name="in_proj_prompt")
    conv_s, z_s, cq_s, ckv_s, kba_s = fused_linear([xs], [w_in_p], gain=norm_mix[L], splits=IN_SPLITS, name="in_proj_sample")

    y_gdn_p, gdn_state_p = gdn_prompt(conv_p.reshape(bp, t, CONV_CH), z_p.reshape(bp, t, n_gdn),
                                      kba_p.reshape(bp, t, LANES), conv_w[L], alog_row, dtb_row, gdn_norm[L])
    conv_state_p = conv_p.reshape(bp, t, CONV_CH)[:, t - (CONV_WIDTH - 1):, :]
    gdn_state_s, conv_state_s, y_gdn_s = gdn_sample(conv_s, state_conv[L], kba_s, z_s, state_gdn[L],
                                                    conv_w[L], alog_row, dtb_row, gdn_norm[L])

    q_p, k_p, v_p, lat_p, kpe_p = mla_prep(cq_p, ckv_p, kba_p, cos_p, sin_p, q_norm[L], kv_norm[L],
                                           wuq_p, wuk, wuv, seq=t, q_dtype=BF16)
    y_mla_p = mla_flash(q_p.reshape(bp, t, -1), k_p.reshape(bp, t, -1), v_p.reshape(bp, t, -1))
    q_s, _, _, lat_s, kpe_s = mla_prep(cq_s, ckv_s, kba_s, cos_s, sin_s, q_norm[L], kv_norm[L],
                                       wuq_p, wuk, wuv, seq=bs, q_dtype=F32)
    q_s4 = q_s.reshape(bs, MLA_HEADS, MLA_QK_PAD)
    qlat = jnp.concatenate(
        [fused_linear([q_s4[:, h, :MLA_NOPE]], [wuk_t[h]], name=f"absorb_q{h}")[0][:, None, :]
         for h in range(MLA_HEADS)], axis=1)
    head_pad = ((0, 0), (0, DEC_HEAD_PAD - MLA_HEADS), (0, 0))
    qlat8 = jnp.pad(qlat, head_pad)
    qpe8 = jnp.pad(q_s4[:, :, MLA_NOPE:MLA_NOPE + MLA_ROPE], head_pad)
    y_mla_s = mla_decode(page_table, qlat8, qpe8, lat_s, kpe_s, wuv,
                         cache_latent, jnp.swapaxes(cache_krope, 2, 3)).reshape(bs, -1)

    (h_p,) = fused_linear([y_gdn_p.reshape(mp, n_gdn), y_mla_p.reshape(mp, -1)], [w_out_b[:n_gdn], w_out_b[n_gdn:]],
                          residual=xp, name="out_proj_prompt")
    (h_s,) = fused_linear([y_gdn_s, y_mla_s], [w_out_b[:n_gdn], w_out_b[n_gdn:]], residual=xs, name="out_proj_sample")

    mem_tokens = mem_prompt.shape[1]
    mk_f, mv_f, mk_b, mv_b = fused_linear([mem_prompt.reshape(bp * mem_tokens, d)], [w_xkv_b], gain=mem_norm[L],
                                          splits=(d, d), bf16_copies=True, name="memory_kv")
    (qx_p,) = fused_linear([h_p], [w_xq_b], gain=norm_x[L], out_dtypes=(BF16,), name="xq_prompt")
    (qx_s,) = fused_linear([h_s], [w_xq_b], gain=norm_x[L], out_dtypes=(BF16,), name="xq_sample")
    ox_p = xattn_prompt(qx_p.reshape(bp, t, d), mk_b.reshape(bp, mem_tokens, d), mv_b.reshape(bp, mem_tokens, d))
    ox_s = xattn_sample(qx_s.reshape(bs, X_HEADS, d // X_HEADS), cache_mem_k[L], cache_mem_v[L])

    h2_p, hn_p, gate_p, ids_p, rank_p, cnt_p = post_xattn(ox_p.reshape(mp, d), w_xo_b, h_p, norm_ffn[L],
                                                          wr_hi, wr_lo, br_row, jnp.zeros((1, LANES), F32))
    h2_s, hn_s, gate_s, ids_s, rank_s, cnt_all = post_xattn(ox_s.reshape(bs, d), w_xo_b, h_s, norm_ffn[L],
                                                            wr_hi, wr_lo, br_row, cnt_p)

    n_tiles, tile_expert, tile_valid, pad_start = _moe_plan(cnt_all, TOP_K * (mp + bs))
    dest_p = _dest_rows(ids_p, rank_p, pad_start)
    dest_s = _dest_rows(ids_s, rank_s, pad_start)
    x_grouped = jnp.zeros((n_tiles * MOE_TILE, d // LANES, LANES), F32)
    x_grouped = moe_dispatch(dest_p, hn_p, x_grouped)
    x_grouped = moe_dispatch(dest_s, hn_s, x_grouped)
    y_sorted = moe_experts(tile_expert, tile_valid, x_grouped, w_e1[L], b1, w_e2[L], b2)
    y_prompt = moe_combine(dest_p, y_sorted, h2_p, gate_p, norm_final)
    y_sample = moe_combine(dest_s, y_sorted, h2_s, gate_s, norm_final)

    x_heads = X_HEADS
    return (y_prompt.reshape(bp, t, d), y_sample.reshape(bs, 1, d),
            gdn_state_p[None], conv_state_p[None],
            lat_p.reshape(bp, 1, t, MLA_KV_RANK), kpe_p.reshape(bp, 1, t, MLA_ROPE),
            mk_f.reshape(1, bp, mem_tokens, x_heads, d // x_heads), mv_f.reshape(1, bp, mem_tokens, x_heads, d // x_heads),
            gdn_state_s[None], conv_state_s[None],
            lat_s.reshape(bs, 1, 1, MLA_KV_RANK), kpe_s.reshape(bs, 1, 1, MLA_ROPE))
```

```python
import functools

import jax
import jax.numpy as jnp
import numpy as np
from jax import lax
from jax.experimental import pallas as pl
from jax.experimental.pallas import tpu as pltpu

F32 = jnp.float32
BF16 = jnp.bfloat16
I32 = jnp.int32

NORM_EPS = 1e-6
LANES = 128
SUBLANES = 8
VMEM_LIMIT = 48 * 1024 * 1024
MOE_VMEM_LIMIT = 56 * 1024 * 1024

GDN_HEADS = 4
GDN_D = 128
CONV_WIDTH = 4
CONV_CH = 3 * GDN_HEADS * GDN_D
GDN_BLOCK = 128
MLA_HEADS = 4
MLA_NOPE = 128
MLA_ROPE = 64
MLA_V = 128
MLA_Q_RANK = 384
MLA_KV_RANK = 256
MLA_QK_PAD = 256
ROPE_THETA = 10000.0
MLA_SCALE = (MLA_NOPE + MLA_ROPE) ** -0.5
X_HEADS = 4
N_EXPERTS = 32
TOP_K = 4
SWIGLU_LIMIT = 7.0
SWIGLU_ALPHA = 1.702
BETA_LANE = MLA_ROPE
DECAY_LANE = MLA_ROPE + GDN_HEADS
NEG_BIG = -1e30
LOG2_E = 1.4426950408889634


def _cparams(*sem):
    return pltpu.CompilerParams(dimension_semantics=sem, vmem_limit_bytes=VMEM_LIMIT)


def _rms(x, gain):
    return x * lax.rsqrt(jnp.mean(x * x, axis=-1, keepdims=True) + NORM_EPS) * gain


def _mm(a, b):
    return jnp.dot(a.astype(BF16), b.astype(BF16), preferred_element_type=F32)


def _mm_nt(a, b):
    return lax.dot_general(a.astype(BF16), b.astype(BF16), (((1,), (1,)), ((), ())),
                           preferred_element_type=F32)


def _mm3(a, b):
    a_hi = a.astype(BF16)
    b_hi = b.astype(BF16)
    a_lo = (a - a_hi.astype(F32)).astype(BF16)
    b_lo = (b - b_hi.astype(F32)).astype(BF16)
    return jnp.dot(jnp.concatenate([a_hi, a_lo, a_hi], axis=1), jnp.concatenate([b_hi, b_hi, b_lo], axis=0),
                   preferred_element_type=F32)


def _bmm(a, b):
    return lax.dot_general(a.astype(BF16), b.astype(BF16), (((2,), (1,)), ((0,), (0,))), preferred_element_type=F32)


def _bmm_nt(a, b):
    return lax.dot_general(a.astype(BF16), b.astype(BF16), (((2,), (2,)), ((0,), (0,))), preferred_element_type=F32)


def _bmm3(a, b):
    a_hi = a.astype(BF16)
    b_hi = b.astype(BF16)
    a_lo = (a - a_hi.astype(F32)).astype(BF16)
    b_lo = (b - b_hi.astype(F32)).astype(BF16)
    return lax.dot_general(jnp.concatenate([a_hi, a_lo, a_hi], axis=2), jnp.concatenate([b_hi, b_hi, b_lo], axis=1),
                           (((2,), (1,)), ((0,), (0,))), preferred_element_type=F32)


def _sigmoid(x):
    return 1.0 / (1.0 + jnp.exp(-x))


def _softplus(x):
    return jnp.maximum(x, 0.0) + jnp.log1p(jnp.exp(-jnp.abs(x)))


def _linear_kernel(*refs, n_in, has_gain, has_res, splits):
    a_refs = refs[:n_in]
    w_refs = refs[n_in:2 * n_in]
    pos = 2 * n_in
    g_ref = refs[pos] if has_gain else None
    pos += int(has_gain)
    r_ref = refs[pos] if has_res else None
    pos += int(has_res)
    out_refs = refs[pos:]
    a0 = a_refs[0][...]
    if has_gain:
        a0 = _rms(a0.astype(F32), g_ref[...])
    acts = [a0.astype(BF16)] + [a[...].astype(BF16) for a in a_refs[1:]]
    off = 0
    for i, width in enumerate(splits):
        acc = None
        for a, w in zip(acts, w_refs):
            d = jnp.dot(a, w[:, off:off + width], preferred_element_type=F32)
            acc = d if acc is None else acc + d
        if has_res:
            acc = acc + r_ref[:, off:off + width]
        for o_ref in out_refs[i::len(splits)]:
            o_ref[...] = acc.astype(o_ref.dtype)
        off += width


def fused_linear(acts, weights, *, gain=None, residual=None, splits=None, out_dtypes=None,
                 bf16_copies=False, tm=256, name="fused_linear"):
    m = acts[0].shape[0]
    n = weights[0].shape[1]
    tm = min(tm, m)
    assert m % tm == 0
    splits = tuple(splits) if splits is not None else (n,)
    assert sum(splits) == n and all(s % LANES == 0 for s in splits)
    out_dtypes = tuple(out_dtypes) if out_dtypes is not None else (F32,) * len(splits)
    out_widths = splits
    if bf16_copies:
        out_widths = splits + splits
        out_dtypes = out_dtypes + (BF16,) * len(splits)
    in_specs = [pl.BlockSpec((tm, a.shape[1]), lambda i: (i, 0)) for a in acts]
    in_specs += [pl.BlockSpec(w.shape, lambda i: (0, 0)) for w in weights]
    args = list(acts) + list(weights)
    if gain is not None:
        in_specs.append(pl.BlockSpec((1, gain.shape[-1]), lambda i: (0, 0)))
        args.append(gain.reshape(1, -1))
    if residual is not None:
        in_specs.append(pl.BlockSpec((tm, n), lambda i: (i, 0)))
        args.append(residual)
    outs = pl.pallas_call(
        functools.partial(_linear_kernel, n_in=len(acts), has_gain=gain is not None,
                          has_res=residual is not None, splits=splits),
        out_shape=[jax.ShapeDtypeStruct((m, s), dt) for s, dt in zip(out_widths, out_dtypes)],
        grid=(m // tm,),
        in_specs=in_specs,
        out_specs=[pl.BlockSpec((tm, s), lambda i: (i, 0)) for s in out_widths],
        compiler_params=_cparams("parallel"),
        name=name,
    )(*args)
    return outs


def _out_proj_xq_kernel(yg_ref, ym_ref, wg_ref, wm_ref, x_ref, gx_ref, wxq_ref, h_ref, qx_ref):
    h = (x_ref[...] + jnp.dot(yg_ref[...], wg_ref[...], preferred_element_type=F32)
         + jnp.dot(ym_ref[...], wm_ref[...], preferred_element_type=F32))
    h_ref[...] = h
    qx_ref[...] = jnp.dot(_rms(h, gx_ref[...]).astype(BF16), wxq_ref[...],
                          preferred_element_type=F32).astype(qx_ref.dtype)


def out_proj_xq(y_gdn, y_mla, w_gdn, w_mla, x, norm_x, w_xq, *, tm=256, name="out_proj_xq"):
    m, d = x.shape
    tm = min(tm, m)
    assert m % tm == 0
    row = lambda i: (i, 0)
    const = lambda i: (0, 0)
    return pl.pallas_call(
        _out_proj_xq_kernel,
        out_shape=[jax.ShapeDtypeStruct((m, d), F32), jax.ShapeDtypeStruct((m, w_xq.shape[1]), BF16)],
        grid=(m // tm,),
        in_specs=[pl.BlockSpec((tm, y_gdn.shape[1]), row), pl.BlockSpec((tm, y_mla.shape[1]), row),
                  pl.BlockSpec(w_gdn.shape, const), pl.BlockSpec(w_mla.shape, const),
                  pl.BlockSpec((tm, d), row), pl.BlockSpec((1, d), const), pl.BlockSpec(w_xq.shape, const)],
        out_specs=[pl.BlockSpec((tm, d), row), pl.BlockSpec((tm, w_xq.shape[1]), row)],
        compiler_params=_cparams("parallel"),
        name=name,
    )(y_gdn, y_mla, w_gdn, w_mla, x, norm_x.reshape(1, -1), w_xq)


def _gate_values(kba, alog_row, dtb_row):
    beta = _sigmoid(kba)
    g = -jnp.exp(alog_row) * _softplus(kba + dtb_row)
    return beta, g


def _l2norm(x):
    return x * lax.rsqrt(jnp.sum(x * x, axis=-1, keepdims=True) + NORM_EPS)


GDN_PROMPT_BB = 2


def _gdn_prompt_kernel(x_ref, z_ref, kba_ref, cw_ref, alog_ref, dtb_ref, gn_ref,
                       y_ref, s_out_ref, xbuf, state):
    t = pl.program_id(1)
    nt = pl.num_programs(1)

    @pl.when(t == 0)
    def _():
        xbuf[:, 0:SUBLANES, :] = jnp.zeros((GDN_PROMPT_BB, SUBLANES, CONV_CH), F32)
        state[...] = jnp.zeros(state.shape, F32)

    blk = GDN_BLOCK
    nh = GDN_HEADS * GDN_D
    row = lax.broadcasted_iota(I32, (blk, blk), 0)
    col = lax.broadcasted_iota(I32, (blk, blk), 1)
    qs, ks, vs, betas, gcols, grows = [], [], [], [], [], []
    for bi in range(GDN_PROMPT_BB):
        c, beta_all, gc = _gdn_conv_and_gates(x_ref.at[bi], kba_ref.at[bi], cw_ref, alog_ref, dtb_ref, xbuf.at[bi], row)
        gc_t = gc.T
        for h in range(GDN_HEADS):
            qs.append(c[:, h * GDN_D:(h + 1) * GDN_D])
            ks.append(c[:, nh + h * GDN_D:nh + (h + 1) * GDN_D])
            vs.append(c[:, 2 * nh + h * GDN_D:2 * nh + (h + 1) * GDN_D])
            betas.append(beta_all[:, BETA_LANE + h:BETA_LANE + h + 1])
            gcols.append(gc[:, DECAY_LANE + h:DECAY_LANE + h + 1])
            grows.append(gc_t[DECAY_LANE + h:DECAY_LANE + h + 1, :])

    q = _l2norm(jnp.stack(qs)) * (GDN_D ** -0.5)
    k = _l2norm(jnp.stack(ks))
    v = jnp.stack(vs)
    bcol = jnp.stack(betas)
    gcol = jnp.stack(gcols)
    grow = jnp.stack(grows)
    decay = jnp.exp(jnp.where((row >= col)[None], gcol - grow, NEG_BIG))
    kb = k * bcol
    vb = v * bcol
    a = jnp.where((row > col)[None], _bmm_nt(kb, k) * decay, 0.0)
    x = (row == col).astype(F32)[None] - a
    p = _bmm3(a, a)
    x = x + _bmm3(x, p)
    for _ in range(5):
        p = _bmm3(p, p)
        x = x + _bmm3(x, p)
    egc = jnp.exp(gcol)
    u = _bmm(x, vb)
    w = _bmm(x, kb * egc)
    intra = _bmm_nt(q, k) * decay
    s_all = state[...].reshape(GDN_PROMPT_BB * GDN_HEADS, GDN_D, GDN_D)
    v_new = u - _bmm(w, s_all)
    o = _bmm(q * egc, s_all) + _bmm(intra, v_new)
    g_last = gcol[:, blk - 1:blk, :]
    kd = k * jnp.exp(g_last - gcol)
    s_new = s_all * jnp.exp(g_last) + _bmm(jnp.swapaxes(kd, 1, 2), v_new)
    state[...] = s_new.reshape(state.shape)
    o = _rms(o, gn_ref[...])
    for bi in range(GDN_PROMPT_BB):
        z = z_ref[bi]
        for h in range(GDN_HEADS):
            zz = z[:, h * GDN_D:(h + 1) * GDN_D]
            y_ref[bi, :, h * GDN_D:(h + 1) * GDN_D] = (o[bi * GDN_HEADS + h] * (zz * _sigmoid(zz))).astype(y_ref.dtype)

    @pl.when(t == nt - 1)
    def _():
        s_out_ref[...] = state[...]


def _gdn_conv_and_gates(x_ref, kba_ref, cw_ref, alog_ref, dtb_ref, xbuf, row):
    blk = GDN_BLOCK
    hist = CONV_WIDTH - 1
    xbuf[SUBLANES:SUBLANES + blk, :] = x_ref[...]
    cw = cw_ref[...]
    conv = xbuf[SUBLANES - hist:SUBLANES - hist + blk, :] * cw[0:1, :]
    for j in range(1, CONV_WIDTH):
        conv = conv + xbuf[SUBLANES - hist + j:SUBLANES - hist + j + blk, :] * cw[j:j + 1, :]
    xbuf[SUBLANES - hist:SUBLANES, :] = xbuf[SUBLANES + blk - hist:SUBLANES + blk, :]
    c = conv * _sigmoid(conv)
    beta_all, g_all = _gate_values(kba_ref[...], alog_ref[...], dtb_ref[...])
    gc = g_all
    shift = 1
    while shift < blk:
        gc = gc + jnp.where(row >= shift, pltpu.roll(gc, shift, 0), 0.0)
        shift *= 2
    return c, beta_all, gc


def gdn_prompt(conv_in, z, kba, conv_w, alog_row, dtb_row, gdn_norm):
    b, t, _ = conv_in.shape
    bb = GDN_PROMPT_BB
    assert t % GDN_BLOCK == 0 and b % bb == 0
    nt = t // GDN_BLOCK
    y, s = pl.pallas_call(
        _gdn_prompt_kernel,
        out_shape=[jax.ShapeDtypeStruct((b, t, GDN_HEADS * GDN_D), BF16),
                   jax.ShapeDtypeStruct((b, GDN_HEADS, GDN_D, GDN_D), F32)],
        grid=(b // bb, nt),
        in_specs=[pl.BlockSpec((bb, GDN_BLOCK, CONV_CH), lambda i, j: (i, j, 0)),
                  pl.BlockSpec((bb, GDN_BLOCK, GDN_HEADS * GDN_D), lambda i, j: (i, j, 0)),
                  pl.BlockSpec((bb, GDN_BLOCK, LANES), lambda i, j: (i, j, 0)),
                  pl.BlockSpec((CONV_WIDTH, CONV_CH), lambda i, j: (0, 0)),
                  pl.BlockSpec((1, LANES), lambda i, j: (0, 0)),
                  pl.BlockSpec((1, LANES), lambda i, j: (0, 0)),
                  pl.BlockSpec((1, GDN_D), lambda i, j: (0, 0))],
        out_specs=[pl.BlockSpec((bb, GDN_BLOCK, GDN_HEADS * GDN_D), lambda i, j: (i, j, 0)),
                   pl.BlockSpec((bb, GDN_HEADS, GDN_D, GDN_D), lambda i, j: (i, 0, 0, 0))],
        scratch_shapes=[pltpu.VMEM((bb, SUBLANES + GDN_BLOCK, CONV_CH), F32),
                        pltpu.VMEM((bb, GDN_HEADS, GDN_D, GDN_D), F32)],
        compiler_params=_cparams("parallel", "arbitrary"),
        name="gdn_prompt",
    )(conv_in, z, kba, conv_w, alog_row, dtb_row, gdn_norm.reshape(1, -1))
    return y, s


GDN_SAMPLE_BB = 8


def _gdn_sample_kernel(x_ref, sc_ref, kba_ref, z_ref, s_ref, cw_ref, alog_ref, dtb_ref, gn_ref,
                       s_out_ref, sc_out_ref, y_ref, tbuf):
    bb = GDN_SAMPLE_BB
    x = x_ref[...]
    cw = cw_ref[...]
    conv = x * cw[CONV_WIDTH - 1:CONV_WIDTH, :]
    for j in range(CONV_WIDTH - 1):
        conv = conv + sc_ref[:, j, :] * cw[j:j + 1, :]
    for j in range(CONV_WIDTH - 2):
        sc_out_ref[:, j, :] = sc_ref[:, j + 1, :]
    sc_out_ref[:, CONV_WIDTH - 2, :] = x
    c = conv * _sigmoid(conv)
    beta_all, g_all = _gate_values(kba_ref[...], alog_ref[...], dtb_ref[...])
    eg_all = jnp.exp(g_all)
    z = z_ref[...]
    gn = gn_ref[...]
    nh = GDN_HEADS * GDN_D
    tbuf[...] = jnp.zeros(tbuf.shape, F32)
    for h in range(GDN_HEADS):
        q = _l2norm(c[:, h * GDN_D:(h + 1) * GDN_D]) * (GDN_D ** -0.5)
        k = _l2norm(c[:, nh + h * GDN_D:nh + (h + 1) * GDN_D])
        v = c[:, 2 * nh + h * GDN_D:2 * nh + (h + 1) * GDN_D]
        tbuf[0:bb, :] = q
        q_t = tbuf[...].T
        tbuf[0:bb, :] = k
        k_t = tbuf[...].T
        for b in range(bb):
            qcol = q_t[:, b:b + 1]
            kcol = k_t[:, b:b + 1]
            eg = eg_all[b:b + 1, DECAY_LANE + h:DECAY_LANE + h + 1]
            beta = beta_all[b:b + 1, BETA_LANE + h:BETA_LANE + h + 1]
            s1 = s_ref[b, h] * eg
            pred = jnp.sum(s1 * kcol, axis=0, keepdims=True)
            u = (v[b:b + 1, :] - pred) * beta
            s2 = s1 + kcol * u
            s_out_ref[b, h] = s2
            o = jnp.sum(s2 * qcol, axis=0, keepdims=True)
            zz = z[b:b + 1, h * GDN_D:(h + 1) * GDN_D]
            y_ref[b:b + 1, h * GDN_D:(h + 1) * GDN_D] = (_rms(o, gn) * (zz * _sigmoid(zz))).astype(y_ref.dtype)


def gdn_sample(conv_in, state_conv, kba, z, state_gdn, conv_w, alog_row, dtb_row, gdn_norm):
    b = conv_in.shape[0]
    bb = GDN_SAMPLE_BB
    assert b % bb == 0
    hist = CONV_WIDTH - 1
    return pl.pallas_call(
        _gdn_sample_kernel,
        out_shape=[jax.ShapeDtypeStruct(state_gdn.shape, F32),
                   jax.ShapeDtypeStruct(state_conv.shape, F32),
                   jax.ShapeDtypeStruct((b, GDN_HEADS * GDN_D), BF16)],
        grid=(b // bb,),
        in_specs=[pl.BlockSpec((bb, CONV_CH), lambda i: (i, 0)),
                  pl.BlockSpec((bb, hist, CONV_CH), lambda i: (i, 0, 0)),
                  pl.BlockSpec((bb, LANES), lambda i: (i, 0)),
                  pl.BlockSpec((bb, GDN_HEADS * GDN_D), lambda i: (i, 0)),
                  pl.BlockSpec((bb, GDN_HEADS, GDN_D, GDN_D), lambda i: (i, 0, 0, 0)),
                  pl.BlockSpec((CONV_WIDTH, CONV_CH), lambda i: (0, 0)),
                  pl.BlockSpec((1, LANES), lambda i: (0, 0)),
                  pl.BlockSpec((1, LANES), lambda i: (0, 0)),
                  pl.BlockSpec((1, GDN_D), lambda i: (0, 0))],
        out_specs=[pl.BlockSpec((bb, GDN_HEADS, GDN_D, GDN_D), lambda i: (i, 0, 0, 0)),
                   pl.BlockSpec((bb, hist, CONV_CH), lambda i: (i, 0, 0)),
                   pl.BlockSpec((bb, GDN_HEADS * GDN_D), lambda i: (i, 0))],
        scratch_shapes=[pltpu.VMEM((GDN_D, GDN_D), F32)],
        compiler_params=_cparams("parallel"),
        name="gdn_sample",
    )(conv_in, state_conv, kba, z, state_gdn, conv_w, alog_row, dtb_row, gdn_norm.reshape(1, -1))


def _rope128(x, cos, sin):
    half = MLA_ROPE // 2
    lane = lax.broadcasted_iota(I32, x.shape, 1)
    swapped = jnp.where(lane < half, pltpu.roll(x, LANES - half, 1), pltpu.roll(x, half, 1))
    return x * cos + swapped * sin


def _mla_prep_kernel(cq_ref, ckv_ref, kba_ref, cos_ref, sin_ref, qn_ref, kvn_ref, wuq_ref, wuk_ref, wuv_ref,
                     q_ref, k_ref, v_ref, lat_ref, kpe_ref):
    cos = cos_ref[...]
    sin = sin_ref[...]
    qn = _rms(cq_ref[...], qn_ref[...]).astype(BF16)
    lat = _rms(ckv_ref[...], kvn_ref[...])
    lat_ref[...] = lat
    lat_b = lat.astype(BF16)
    kpe = _rope128(kba_ref[...], cos, sin)
    kpe_ref[...] = kpe[:, :MLA_ROPE]
    for h in range(MLA_HEADS):
        lo = h * MLA_QK_PAD
        q_ref[:, lo:lo + MLA_NOPE] = jnp.dot(
            qn, wuq_ref[:, lo:lo + MLA_NOPE], preferred_element_type=F32).astype(q_ref.dtype)
        q_pe = jnp.dot(qn, wuq_ref[:, lo + MLA_NOPE:lo + MLA_QK_PAD], preferred_element_type=F32)
        q_ref[:, lo + MLA_NOPE:lo + MLA_QK_PAD] = _rope128(q_pe, cos, sin).astype(q_ref.dtype)
        k_ref[:, lo:lo + MLA_NOPE] = jnp.dot(
            lat_b, wuk_ref[:, h * MLA_NOPE:(h + 1) * MLA_NOPE], preferred_element_type=F32).astype(k_ref.dtype)
        k_ref[:, lo + MLA_NOPE:lo + MLA_QK_PAD] = kpe.astype(k_ref.dtype)
    v_ref[...] = jnp.dot(lat_b, wuv_ref[...], preferred_element_type=F32).astype(v_ref.dtype)


def mla_prep(c_q, c_kv, kba, cos_tab, sin_tab, q_norm, kv_norm, wuq_p, wuk, wuv, *, seq, q_dtype, tm=256):
    m = c_q.shape[0]
    tm = min(tm, m, seq)
    assert m % tm == 0 and seq % tm == 0
    nseq = seq // tm
    hq = MLA_HEADS * MLA_QK_PAD
    row = lambda i: (i, 0)
    const = lambda i: (0, 0)
    return pl.pallas_call(
        _mla_prep_kernel,
        out_shape=[jax.ShapeDtypeStruct((m, hq), q_dtype),
                   jax.ShapeDtypeStruct((m, hq), BF16),
                   jax.ShapeDtypeStruct((m, MLA_HEADS * MLA_V), BF16),
                   jax.ShapeDtypeStruct((m, MLA_KV_RANK), F32),
                   jax.ShapeDtypeStruct((m, MLA_ROPE), F32)],
        grid=(m // tm,),
        in_specs=[pl.BlockSpec((tm, MLA_Q_RANK), row),
                  pl.BlockSpec((tm, MLA_KV_RANK), row),
                  pl.BlockSpec((tm, LANES), row),
                  pl.BlockSpec((tm, LANES), lambda i: (i % nseq, 0)),
                  pl.BlockSpec((tm, LANES), lambda i: (i % nseq, 0)),
                  pl.BlockSpec((1, MLA_Q_RANK), const),
                  pl.BlockSpec((1, MLA_KV_RANK), const),
                  pl.BlockSpec(wuq_p.shape, const),
                  pl.BlockSpec(wuk.shape, const),
                  pl.BlockSpec(wuv.shape, const)],
        out_specs=[pl.BlockSpec((tm, hq), row),
                   pl.BlockSpec((tm, hq), row),
                   pl.BlockSpec((tm, MLA_HEADS * MLA_V), row),
                   pl.BlockSpec((tm, MLA_KV_RANK), row),
                   pl.BlockSpec((tm, MLA_ROPE), row)],
        compiler_params=_cparams("parallel"),
        name="mla_prep",
    )(c_q, c_kv, kba, cos_tab, sin_tab, q_norm.reshape(1, -1), kv_norm.reshape(1, -1), wuq_p, wuk, wuv)


def _flash_kernel(q_ref, k_ref, v_ref, o_ref, m_s, l_s, acc_s, *, tq):
    qi = pl.program_id(2)
    m_s[...] = jnp.full(m_s.shape, NEG_BIG, F32)
    l_s[...] = jnp.zeros(l_s.shape, F32)
    acc_s[...] = jnp.zeros(acc_s.shape, F32)

    def block(j, masked):
        k0 = pl.multiple_of(j * tq, tq)
        for h in range(FLASH_HEADS):
            q = q_ref[0, :, h * MLA_QK_PAD:(h + 1) * MLA_QK_PAD]
            s = lax.dot_general(q, k_ref[0, pl.ds(k0, tq), h * MLA_QK_PAD:(h + 1) * MLA_QK_PAD],
                                (((1,), (1,)), ((), ())), preferred_element_type=F32) * (MLA_SCALE * LOG2_E)
            if masked:
                s = jnp.where(lax.broadcasted_iota(I32, (tq, tq), 1) <= lax.broadcasted_iota(I32, (tq, tq), 0),
                              s, NEG_BIG)
            m_prev = m_s[h]
            m_new = jnp.maximum(m_prev, jnp.max(s, axis=-1, keepdims=True))
            corr = jnp.exp2(m_prev - m_new)
            p = jnp.exp2(s - m_new)
            l_s[h] = l_s[h] * corr + jnp.sum(p, axis=-1, keepdims=True)
            acc_s[h] = acc_s[h] * corr + jnp.dot(p.astype(BF16), v_ref[0, pl.ds(k0, tq), h * MLA_V:(h + 1) * MLA_V],
                                                 preferred_element_type=F32)
            m_s[h] = m_new

    def below_diagonal(j, _):
        block(j, masked=False)
        return 0

    lax.fori_loop(0, qi, below_diagonal, 0)
    block(qi, masked=True)
    for h in range(FLASH_HEADS):
        o_ref[0, :, h * MLA_V:(h + 1) * MLA_V] = (acc_s[h] / l_s[h]).astype(o_ref.dtype)


FLASH_HEADS = 2


def mla_flash(q, k, v, *, tq=512):
    b, t, _ = q.shape
    tq = min(tq, t)
    hs = FLASH_HEADS
    assert t % tq == 0 and MLA_HEADS % hs == 0
    seq_map = lambda bi, h, qi: (bi, 0, h)
    return pl.pallas_call(
        functools.partial(_flash_kernel, tq=tq),
        out_shape=jax.ShapeDtypeStruct((b, t, MLA_HEADS * MLA_V), BF16),
        grid=(b, MLA_HEADS // hs, t // tq),
        in_specs=[pl.BlockSpec((1, tq, hs * MLA_QK_PAD), lambda bi, h, qi: (bi, qi, h)),
                  pl.BlockSpec((1, t, hs * MLA_QK_PAD), seq_map),
                  pl.BlockSpec((1, t, hs * MLA_V), seq_map)],
        out_specs=pl.BlockSpec((1, tq, hs * MLA_V), lambda bi, h, qi: (bi, qi, h)),
        scratch_shapes=[pltpu.VMEM((hs, tq, 1), F32), pltpu.VMEM((hs, tq, 1), F32), pltpu.VMEM((hs, tq, MLA_V), F32)],
        compiler_params=_cparams("parallel", "parallel", "arbitrary"),
        name="mla_flash",
    )(q, k, v)


DEC_HEAD_PAD = 8
DEC_GROUP = 16
DEC_SLOTS = 4


def _decode_kernel(pt_ref, ptn_ref, qlat_ref, qpe_ref, latn_ref, kpen_ref, wuv_ref, lat_hbm, kpe_hbm,
                   o_ref, latbuf, kpebuf, sem, *, nb, n_pages, page):
    b = pl.program_id(0)
    grp = DEC_GROUP
    ns = DEC_SLOTS
    n_groups = n_pages // grp
    n_outer = n_groups // ns

    def page_copies(pg, slot, j):
        return (pltpu.make_async_copy(lat_hbm.at[pg, 0], latbuf.at[slot, j], sem.at[0, slot]),
                pltpu.make_async_copy(kpe_hbm.at[pg, 0], kpebuf.at[slot, j], sem.at[1, slot]))

    def start_group(tbl_ref, g, slot):
        for j in range(grp):
            for cp in page_copies(tbl_ref[0, 0, g * grp + j], slot, j):
                cp.start()

    def wait_group(slot):
        for j in range(grp):
            for cp in page_copies(0, slot, j):
                cp.wait()

    @pl.when(b == 0)
    def _():
        for s in range(ns - 1):
            start_group(pt_ref, s, s)

    qlat = qlat_ref[0]
    qpe = qpe_ref[0]
    latn = latn_ref[0]
    kpen = kpen_ref[0]
    qlat_b = qlat.astype(BF16)
    qpe_b = qpe.astype(BF16)
    s_new = (jnp.sum(qlat * latn, axis=-1, keepdims=True)
             + jnp.sum(qpe * kpen, axis=-1, keepdims=True)) * MLA_SCALE
    m0 = s_new
    l0 = jnp.ones_like(s_new)
    acc0 = jnp.broadcast_to(latn, qlat.shape)

    def consume(slot, carry):
        m, l, acc = carry
        lat = latbuf[slot].reshape(grp * page, MLA_KV_RANK).astype(BF16)
        kpe_t = jnp.concatenate([kpebuf[slot, j] for j in range(grp)], axis=1).astype(BF16)
        s = (lax.dot_general(qlat_b, lat, (((1,), (1,)), ((), ())), preferred_element_type=F32)
             + jnp.dot(qpe_b, kpe_t, preferred_element_type=F32)) * MLA_SCALE
        m_new = jnp.maximum(m, jnp.max(s, axis=-1, keepdims=True))
        corr = jnp.exp(m - m_new)
        p = jnp.exp(s - m_new)
        l = l * corr + jnp.sum(p, axis=-1, keepdims=True)
        acc = acc * corr + jnp.dot(p.astype(BF16), lat, preferred_element_type=F32)
        return m_new, l, acc

    def ring_body(i, carry):
        for j in range(ns):
            ahead = (j + ns - 1) % ns
            if j == 0:
                start_group(pt_ref, i * ns + ns - 1, ahead)
            else:
                @pl.when(i + 1 < n_outer)
                def _():
                    start_group(pt_ref, (i + 1) * ns + ahead, ahead)

                @pl.when(jnp.logical_and(i + 1 >= n_outer, b + 1 < nb))
                def _():
                    start_group(ptn_ref, ahead, ahead)
            wait_group(j)
            carry = consume(j, carry)
        return carry

    m, l, acc = lax.fori_loop(0, n_outer, ring_body, (m0, l0, acc0))
    o_lat = (acc / l).astype(BF16)
    res = jnp.dot(o_lat, wuv_ref[...], preferred_element_type=F32)
    o_ref[0] = jnp.concatenate(
        [res[h:h + 1, h * MLA_V:(h + 1) * MLA_V] for h in range(MLA_HEADS)], axis=1).astype(o_ref.dtype)


def mla_decode(page_table, qlat8, qpe8, lat_new, kpe_new, wuv, cache_latent, cache_krope_t):
    b, n_pages = page_table.shape
    page = cache_latent.shape[2]
    assert n_pages % (DEC_SLOTS * DEC_GROUP) == 0
    pt3 = page_table.reshape(b, 1, n_pages)
    smem_row = lambda f: pl.BlockSpec((1, 1, n_pages), f, memory_space=pltpu.SMEM)
    return pl.pallas_call(
        functools.partial(_decode_kernel, nb=b, n_pages=n_pages, page=page),
        out_shape=jax.ShapeDtypeStruct((b, 1, MLA_HEADS * MLA_V), BF16),
        grid=(b,),
        in_specs=[smem_row(lambda i: (i, 0, 0)),
                  smem_row(lambda i: (jnp.minimum(i + 1, b - 1), 0, 0)),
                  pl.BlockSpec((1, DEC_HEAD_PAD, MLA_KV_RANK), lambda i: (i, 0, 0)),
                  pl.BlockSpec((1, DEC_HEAD_PAD, MLA_ROPE), lambda i: (i, 0, 0)),
                  pl.BlockSpec((1, 1, MLA_KV_RANK), lambda i: (i, 0, 0)),
                  pl.BlockSpec((1, 1, MLA_ROPE), lambda i: (i, 0, 0)),
                  pl.BlockSpec(wuv.shape, lambda i: (0, 0)),
                  pl.BlockSpec(memory_space=pl.ANY),
                  pl.BlockSpec(memory_space=pl.ANY)],
        out_specs=pl.BlockSpec((1, 1, MLA_HEADS * MLA_V), lambda i: (i, 0, 0)),
        scratch_shapes=[pltpu.VMEM((DEC_SLOTS, DEC_GROUP, page, MLA_KV_RANK), F32),
                        pltpu.VMEM((DEC_SLOTS, DEC_GROUP, MLA_ROPE, page), F32),
                        pltpu.SemaphoreType.DMA((2, DEC_SLOTS))],
        compiler_params=_cparams("arbitrary"),
        name="mla_decode",
    )(pt3, pt3, qlat8, qpe8, lat_new.reshape(b, 1, -1), kpe_new.reshape(b, 1, -1), wuv,
      cache_latent, cache_krope_t)


def _softmax_rows(s):
    m = jnp.max(s, axis=-1, keepdims=True)
    p = jnp.exp(s - m)
    return p / jnp.sum(p, axis=-1, keepdims=True)


def _xattn_prompt_kernel(q_ref, k_ref, v_ref, o_ref, *, dh):
    scale = dh ** -0.5
    for h in range(X_HEADS):
        q = q_ref[0, :, h * dh:(h + 1) * dh]
        s = lax.dot_general(q, k_ref[0, :, h * dh:(h + 1) * dh], (((1,), (1,)), ((), ())),
                            preferred_element_type=F32) * scale
        p = _softmax_rows(s).astype(BF16)
        o_ref[0, :, h * dh:(h + 1) * dh] = jnp.dot(
            p, v_ref[0, :, h * dh:(h + 1) * dh], preferred_element_type=F32).astype(o_ref.dtype)


def xattn_prompt(q, k, v, *, tq=512):
    b, t, d = q.shape
    mem = k.shape[1]
    tq = min(tq, t)
    return pl.pallas_call(
        functools.partial(_xattn_prompt_kernel, dh=d // X_HEADS),
        out_shape=jax.ShapeDtypeStruct((b, t, d), BF16),
        grid=(b, t // tq),
        in_specs=[pl.BlockSpec((1, tq, d), lambda i, j: (i, j, 0)),
                  pl.BlockSpec((1, mem, d), lambda i, j: (i, 0, 0)),
                  pl.BlockSpec((1, mem, d), lambda i, j: (i, 0, 0))],
        out_specs=pl.BlockSpec((1, tq, d), lambda i, j: (i, j, 0)),
        compiler_params=_cparams("parallel", "parallel"),
        name="xattn_prompt",
    )(q, k, v)


XATTN_SAMPLE_BB = 2


def _xattn_sample_kernel(q_ref, k_ref, v_ref, o_ref, *, dh):
    scale = dh ** -0.5
    for b in range(XATTN_SAMPLE_BB):
        q = q_ref[b].astype(F32)
        s = jnp.sum(k_ref[b] * q[None], axis=-1, keepdims=True) * scale
        m = jnp.max(s, axis=0, keepdims=True)
        p = jnp.exp(s - m)
        denom = jnp.sum(p, axis=0, keepdims=True)
        o = jnp.sum(p * v_ref[b], axis=0) / denom[0]
        o_ref[b] = o.astype(o_ref.dtype)


def xattn_sample(q, k, v):
    b, heads, dh = q.shape
    mem = k.shape[1]
    bb = XATTN_SAMPLE_BB
    assert b % bb == 0
    kv_spec = pl.BlockSpec((bb, mem, heads, dh), lambda i: (i, 0, 0, 0))
    return pl.pallas_call(
        functools.partial(_xattn_sample_kernel, dh=dh),
        out_shape=jax.ShapeDtypeStruct((b, heads, dh), BF16),
        grid=(b // bb,),
        in_specs=[pl.BlockSpec((bb, heads, dh), lambda i: (i, 0, 0)), kv_spec, kv_spec],
        out_specs=pl.BlockSpec((bb, heads, dh), lambda i: (i, 0, 0)),
        compiler_params=_cparams("parallel"),
        name="xattn_sample",
    )(q, k, v)


def _post_xattn_kernel(ox_ref, wxo_ref, h_ref, g_ref, wr_hi_ref, wr_lo_ref, br_ref, cnt_in_ref, before_ref,
                       h2_ref, hn_ref, gate_ref, idx_ref, rank_ref, cnt_ref, cnt_s):
    @pl.when(pl.program_id(0) == 0)
    def _():
        cnt_s[...] = cnt_in_ref[...]

    h2 = h_ref[...] + jnp.dot(ox_ref[...], wxo_ref[...], preferred_element_type=F32)
    h2_ref[...] = h2
    hn = _rms(h2, g_ref[...])
    hn_ref[...] = hn.reshape(hn_ref.shape)
    hn_hi = hn.astype(BF16)
    hn_lo = (hn - hn_hi.astype(F32)).astype(BF16)
    logits = (jnp.dot(hn_hi, wr_hi_ref[...], preferred_element_type=F32)
              + jnp.dot(hn_hi, wr_lo_ref[...], preferred_element_type=F32)
              + jnp.dot(hn_lo, wr_hi_ref[...], preferred_element_type=F32)) + br_ref[...]
    lane = lax.broadcasted_iota(I32, logits.shape, 1)
    logits = jnp.where(lane < N_EXPERTS, logits, NEG_BIG)
    vals, picks = [], []
    gates = jnp.zeros(logits.shape, F32)
    ids = jnp.zeros(logits.shape, I32)
    member = jnp.zeros(logits.shape, F32)
    for k in range(TOP_K):
        m = jnp.max(logits, axis=-1, keepdims=True)
        idx = jnp.min(jnp.where(logits == m, lane, LANES), axis=-1, keepdims=True)
        vals.append(m)
        picks.append(idx)
        ids = jnp.where(lane == k, idx, ids)
        member = jnp.where(lane == idx, 1.0, member)
        logits = jnp.where(lane == idx, NEG_BIG, logits)
    exps = [jnp.exp(v - vals[0]) for v in vals]
    denom = exps[0]
    for e in exps[1:]:
        denom = denom + e
    for k in range(TOP_K):
        gates = jnp.where(lane == k, exps[k] / denom, gates)
    gate_ref[...] = gates
    idx_ref[...] = ids
    prior = jnp.dot(before_ref[...], member.astype(BF16), preferred_element_type=F32) + cnt_s[...]
    ranks = jnp.zeros(logits.shape, I32)
    for k in range(TOP_K):
        rk = jnp.sum(jnp.where(lane == picks[k], prior, 0.0), axis=-1, keepdims=True)
        ranks = jnp.where(lane == k, rk.astype(I32), ranks)
    rank_ref[...] = ranks
    cnt = cnt_s[...] + jnp.sum(member, axis=0, keepdims=True)
    cnt_s[...] = cnt
    cnt_ref[...] = cnt


def post_xattn(ox, wxo, h, norm_ffn, wr_hi, wr_lo, br_row, cnt_in, *, tm=256):
    m, d = h.shape
    tm = min(tm, m)
    row = lambda i: (i, 0)
    const = lambda i: (0, 0)
    before = jnp.tril(jnp.ones((tm, tm), F32), -1).astype(BF16)
    return pl.pallas_call(
        _post_xattn_kernel,
        out_shape=[jax.ShapeDtypeStruct((m, d), F32), jax.ShapeDtypeStruct((m, d // LANES, LANES), F32),
                   jax.ShapeDtypeStruct((m, LANES), F32), jax.ShapeDtypeStruct((m, LANES), I32),
                   jax.ShapeDtypeStruct((m, LANES), I32), jax.ShapeDtypeStruct((1, LANES), F32)],
        grid=(m // tm,),
        in_specs=[pl.BlockSpec((tm, d), row), pl.BlockSpec(wxo.shape, const), pl.BlockSpec((tm, d), row),
                  pl.BlockSpec((1, d), const), pl.BlockSpec(wr_hi.shape, const), pl.BlockSpec(wr_lo.shape, const),
                  pl.BlockSpec((1, LANES), const), pl.BlockSpec((1, LANES), const), pl.BlockSpec((tm, tm), const)],
        out_specs=[pl.BlockSpec((tm, d), row), pl.BlockSpec((tm, d // LANES, LANES), lambda i: (i, 0, 0)),
                   pl.BlockSpec((tm, LANES), row), pl.BlockSpec((tm, LANES), row),
                   pl.BlockSpec((tm, LANES), row), pl.BlockSpec((1, LANES), const)],
        scratch_shapes=[pltpu.VMEM((1, LANES), F32)],
        compiler_params=_cparams("arbitrary"),
        name="post_xattn_router",
    )(ox, wxo, h, norm_ffn.reshape(1, -1), wr_hi, wr_lo, br_row, cnt_in, before)


MOE_TILE = 256


ROW_TILE = 128
MOE_FF_CHUNK = 512


def _row_dma_loop(n_tokens, make_copies, wait):
    def body(n, _):
        for k, cp in enumerate(make_copies(n)):
            cp.wait() if wait else cp.start(priority=k % 2)
        return 0
    lax.fori_loop(0, n_tokens, body, 0, unroll=4)


def _dispatch_kernel(dest_ref, x_ref, xs_in_ref, xs_ref, buf, sem, *, nt):
    del xs_in_ref
    t = pl.program_id(0)
    tm = ROW_TILE
    slot = t % 2

    def copies(s, dest_of):
        def make(n):
            return [pltpu.make_async_copy(buf.at[s, n], xs_ref.at[dest_of(n, k)], sem.at[s]) for k in range(TOP_K)]
        return make

    def wait_slot(s):
        _row_dma_loop(tm, copies(s, lambda n, k: 0), wait=True)

    if nt > 2:
        @pl.when(t >= 2)
        def _():
            wait_slot(slot)

    buf[slot] = x_ref[...]
    _row_dma_loop(tm, copies(slot, lambda n, k: dest_ref[0, 0, n * TOP_K + k]), wait=False)

    @pl.when(t == nt - 1)
    def _():
        wait_slot(slot)
        if nt > 1:
            wait_slot(1 - slot)


def moe_dispatch(dest, x3, xs):
    m = x3.shape[0]
    tm = ROW_TILE
    assert m % tm == 0
    n_tiles = m // tm
    return pl.pallas_call(
        functools.partial(_dispatch_kernel, nt=n_tiles),
        out_shape=jax.ShapeDtypeStruct(xs.shape, xs.dtype),
        grid=(n_tiles,),
        in_specs=[pl.BlockSpec((1, 1, TOP_K * tm), lambda t: (t, 0, 0), memory_space=pltpu.SMEM),
                  pl.BlockSpec((tm,) + x3.shape[1:], lambda t: (t, 0, 0)),
                  pl.BlockSpec(memory_space=pl.ANY)],
        out_specs=pl.BlockSpec(memory_space=pl.ANY),
        scratch_shapes=[pltpu.VMEM((2, tm) + x3.shape[1:], x3.dtype), pltpu.SemaphoreType.DMA((2,))],
        input_output_aliases={2: 0},
        compiler_params=_cparams("arbitrary"),
        name="moe_dispatch",
    )(dest.reshape(n_tiles, 1, TOP_K * tm), x3, xs)


def _moe_kernel(te_ref, tv_ref, x_ref, w1_ref, b1_ref, w2_ref, b2_ref, y_ref, w1b, w2s, w2p_ref, act_s):
    t = pl.program_id(0)

    @pl.when(jnp.logical_or(t == 0, te_ref[t] != te_ref[jnp.maximum(t - 1, 0)]))
    def _():
        w1b[...] = w1_ref[0].astype(BF16)
        half = LANES // 2
        n_chunks = w2s.shape[0]
        for c in range(n_chunks):
            for blk in range(w2s.shape[1] // LANES):
                r0 = blk * LANES
                w2s[c, pl.ds(r0, half, stride=2), :] = w2_ref[0, r0:r0 + half, c * LANES:(c + 1) * LANES]
                w2s[c, pl.ds(r0 + 1, half, stride=2), :] = w2_ref[0, r0 + half:r0 + LANES, c * LANES:(c + 1) * LANES]
        w2p_ref[...] = jnp.concatenate([w2s[c] for c in range(n_chunks)], axis=1).astype(BF16)

    @pl.when(tv_ref[t] > 0)
    def _():
        tm = x_ref.shape[0]
        x = x_ref[...].reshape(tm, w1b.shape[0]).astype(BF16)
        ff2 = w1b.shape[1]
        even = lax.broadcasted_iota(I32, (x.shape[0], LANES), 1) % 2 == 0
        for c in range(ff2 // MOE_FF_CHUNK):
            lo = c * MOE_FF_CHUNK
            hh = jnp.dot(x, w1b[:, lo:lo + MOE_FF_CHUNK], preferred_element_type=F32) + b1_ref[0, :, lo:lo + MOE_FF_CHUNK]
            parts = []
            for j in range(MOE_FF_CHUNK // (2 * LANES)):
                a = hh[:, (2 * j) * LANES:(2 * j + 1) * LANES]
                b = hh[:, (2 * j + 1) * LANES:(2 * j + 2) * LANES]
                glu = jnp.where(even, a, pltpu.roll(b, 1, 1))
                lin = jnp.where(even, pltpu.roll(a, LANES - 1, 1), b)
                glu = jnp.minimum(glu, SWIGLU_LIMIT)
                lin = jnp.clip(lin, -SWIGLU_LIMIT, SWIGLU_LIMIT)
                parts.append((lin + 1.0) * (glu * _sigmoid(SWIGLU_ALPHA * glu)))
            act_s[:, lo // 2:(lo + MOE_FF_CHUNK) // 2] = jnp.concatenate(parts, axis=1).astype(BF16)
        y = jnp.dot(act_s[...], w2p_ref[...], preferred_element_type=F32) + b2_ref[0]
        y_ref[...] = y.reshape(y_ref.shape)

    @pl.when(tv_ref[t] == 0)
    def _():
        y_ref[...] = jnp.zeros(y_ref.shape, F32)


def moe_experts(tile_expert, tile_valid, xs, w1, b1, w2, b2):
    n_tiles = tile_expert.shape[0]
    tm = MOE_TILE
    d = w1.shape[1]
    ff2 = w1.shape[2]
    assert ff2 % MOE_FF_CHUNK == 0 and MOE_FF_CHUNK % (2 * LANES) == 0 and xs.shape[1:] == (d // LANES, LANES)
    wmap = lambda t, te, tv: (te[t], 0, 0)
    row_tiles = pl.BlockSpec((tm, d // LANES, LANES), lambda t, te, tv: (t, 0, 0))
    grid_spec = pltpu.PrefetchScalarGridSpec(
        num_scalar_prefetch=2,
        grid=(n_tiles,),
        in_specs=[row_tiles,
                  pl.BlockSpec((1, d, ff2), wmap), pl.BlockSpec((1, 1, ff2), wmap),
                  pl.BlockSpec((1, ff2 // 2, d), wmap), pl.BlockSpec((1, 1, d), wmap)],
        out_specs=row_tiles,
        scratch_shapes=[pltpu.VMEM((d, ff2), BF16), pltpu.VMEM((d // LANES, ff2 // 2, LANES), F32),
                        pltpu.VMEM((ff2 // 2, d), BF16), pltpu.VMEM((tm, ff2 // 2), BF16)],
    )
    return pl.pallas_call(
        _moe_kernel,
        out_shape=jax.ShapeDtypeStruct(xs.shape, F32),
        grid_spec=grid_spec,
        compiler_params=pltpu.CompilerParams(dimension_semantics=("arbitrary",), vmem_limit_bytes=MOE_VMEM_LIMIT),
        name="moe_experts",
    )(tile_expert, tile_valid, xs, w1, b1, w2, b2)


def _combine_kernel(dest_ref, destn_ref, y_hbm, h2_ref, gate_ref, gain_ref, o_ref, ybuf, sem, *, nt):
    t = pl.program_id(0)
    tm = ROW_TILE
    slot = t % 2

    def copies(s, dest_of):
        def make(n):
            return [pltpu.make_async_copy(y_hbm.at[dest_of(n, k)], ybuf.at[s, k * tm + n], sem.at[s])
                    for k in range(TOP_K)]
        return make

    @pl.when(t == 0)
    def _():
        _row_dma_loop(tm, copies(0, lambda n, k: dest_ref[0, 0, n * TOP_K + k]), wait=False)

    if nt > 1:
        @pl.when(t + 1 < nt)
        def _():
            _row_dma_loop(tm, copies(1 - slot, lambda n, k: destn_ref[0, 0, n * TOP_K + k]), wait=False)

    _row_dma_loop(tm, copies(slot, lambda n, k: 0), wait=True)
    gates = gate_ref[...]
    acc = h2_ref[...]
    for k in range(TOP_K):
        yk = ybuf[slot, k * tm:(k + 1) * tm].reshape(acc.shape)
        acc = acc + gates[:, k:k + 1] * yk
    o_ref[...] = _rms(acc, gain_ref[...])


def moe_combine(dest, y_sorted, h2, gates, norm_final):
    m, d = h2.shape
    tm = ROW_TILE
    assert m % tm == 0
    n_tiles = m // tm
    row = lambda t: (t, 0)
    dest3 = dest.reshape(n_tiles, 1, TOP_K * tm)
    return pl.pallas_call(
        functools.partial(_combine_kernel, nt=n_tiles),
        out_shape=jax.ShapeDtypeStruct((m, d), F32),
        grid=(n_tiles,),
        in_specs=[pl.BlockSpec((1, 1, TOP_K * tm), lambda t: (t, 0, 0), memory_space=pltpu.SMEM),
                  pl.BlockSpec((1, 1, TOP_K * tm), lambda t: (jnp.minimum(t + 1, n_tiles - 1), 0, 0),
                               memory_space=pltpu.SMEM),
                  pl.BlockSpec(memory_space=pl.ANY),
                  pl.BlockSpec((tm, d), row), pl.BlockSpec((tm, LANES), row),
                  pl.BlockSpec((1, d), lambda t: (0, 0))],
        out_specs=pl.BlockSpec((tm, d), row),
        scratch_shapes=[pltpu.VMEM((2, TOP_K * tm, d // LANES, LANES), F32), pltpu.SemaphoreType.DMA((2,))],
        compiler_params=_cparams("arbitrary"),
        name="moe_combine",
    )(dest3, dest3, y_sorted, h2, gates, norm_final.reshape(1, -1))


def _moe_plan(counts_row, n_assign):
    tm = MOE_TILE
    n_tiles = (n_assign + N_EXPERTS * (tm - 1) + tm - 1) // tm
    counts = counts_row[0, :N_EXPERTS].astype(I32)
    tiles_per = (counts + tm - 1) // tm
    tile_end = jnp.cumsum(tiles_per)
    pad_start = (tile_end - tiles_per) * tm
    tile_ids = jnp.arange(n_tiles, dtype=I32)
    used = tile_end[-1]
    tile_valid = (tile_ids < used).astype(I32)
    clamped = jnp.minimum(tile_ids, used - 1)
    tile_expert = jnp.sum((clamped[:, None] >= tile_end[None, :]).astype(I32), axis=1)
    return n_tiles, jnp.minimum(tile_expert, N_EXPERTS - 1), tile_valid, pad_start


def _dest_rows(ids, ranks, pad_start):
    sel = ids[:, :TOP_K, None] == jnp.arange(N_EXPERTS, dtype=I32)
    return ranks[:, :TOP_K] + jnp.sum(jnp.where(sel, pad_start, 0), axis=-1)


def _rope_tables(positions):
    half = MLA_ROPE // 2
    inv_freq = 1.0 / (ROPE_THETA ** (jnp.arange(half, dtype=F32) / half))
    ang = positions.astype(F32)[:, None] * inv_freq[None, :]
    cos, sin = jnp.cos(ang), jnp.sin(ang)
    zeros = jnp.zeros((positions.shape[0], LANES - MLA_ROPE), F32)
    return (jnp.concatenate([cos, cos, zeros], axis=1), jnp.concatenate([-sin, sin, zeros], axis=1))


def _in_proj_weight(w_in):
    o = np.cumsum([0, CONV_CH, GDN_HEADS * GDN_D, GDN_HEADS, GDN_HEADS, MLA_Q_RANK, MLA_KV_RANK, MLA_ROPE])
    conv, z, bl, al, cq, ckv, kpe = (w_in[:, o[i]:o[i + 1]] for i in range(7))
    pad = jnp.zeros((w_in.shape[0], LANES - MLA_ROPE - 2 * GDN_HEADS), w_in.dtype)
    return jnp.concatenate([conv, z, cq, ckv, kpe, bl, al, pad], axis=1).astype(BF16)


IN_SPLITS = (CONV_CH, GDN_HEADS * GDN_D, MLA_Q_RANK, MLA_KV_RANK, LANES)


def _lane_row(vals, lane0):
    return jnp.zeros((1, LANES), F32).at[0, lane0:lane0 + vals.shape[0]].set(vals.astype(F32))


def kernel(x_prompt, x_sample, state_gdn, state_conv, cache_latent, cache_krope, page_table, cache_mem_k, cache_mem_v, mem_prompt, norm_mix, w_in, conv_w, a_log, dt_bias, gdn_norm, q_norm, w_uq, kv_norm, w_uk, w_uv, w_out, norm_x, mem_norm, w_xq, w_xk, w_xv, w_xo, norm_ffn, w_router, b_router, w_e1, b_e1, w_e2, b_e2, norm_final):
    depth = w_in.shape[0]
    assert depth == 1
    L = 0
    bp, t, d = x_prompt.shape
    bs = x_sample.shape[0]
    assert x_sample.shape[1] == 1
    past_len = page_table.shape[1] * cache_latent.shape[2]
    mp = bp * t

    w_in_p = _in_proj_weight(w_in[L])
    alog_row = _lane_row(a_log[L], DECAY_LANE)
    dtb_row = _lane_row(dt_bias[L], DECAY_LANE)
    wuq = w_uq[L]
    wuq_p = jnp.concatenate(
        [wuq, jnp.zeros(wuq.shape[:2] + (MLA_QK_PAD - wuq.shape[2],), wuq.dtype)], axis=2
    ).reshape(wuq.shape[0], MLA_HEADS * MLA_QK_PAD).astype(BF16)
    wuk = w_uk[L].reshape(MLA_KV_RANK, MLA_HEADS * MLA_NOPE).astype(BF16)
    wuv = w_uv[L].reshape(MLA_KV_RANK, MLA_HEADS * MLA_V).astype(BF16)
    wuk_t = jnp.transpose(w_uk[L], (1, 2, 0)).astype(BF16)
    w_out_b = w_out[L].astype(BF16)
    n_gdn = GDN_HEADS * GDN_D
    w_xq_b, w_xo_b = w_xq[L].astype(BF16), w_xo[L].astype(BF16)
    w_xkv_b = jnp.concatenate([w_xk[L], w_xv[L]], axis=1).astype(BF16)
    wr = jnp.concatenate([w_router[L], jnp.zeros((d, LANES - N_EXPERTS), F32)], axis=1)
    wr_hi = wr.astype(BF16)
    wr_lo = (wr - wr_hi.astype(F32)).astype(BF16)
    br_row = _lane_row(b_router[L], 0)
    b1 = b_e1[L][:, None, :]
    b2 = b_e2[L][:, None, :]
    cos_p, sin_p = _rope_tables(jnp.arange(t, dtype=I32))
    cos_s, sin_s = _rope_tables(jnp.full((bs,), past_len, I32))

    xp = x_prompt.reshape(mp, d)
    xs = x_sample.reshape(bs, d)
    conv_p, z_p, cq_p, ckv_p, kba_p = fused_linear([xp], [w_in_p], gain=norm_mix[L], splits=IN_SPLITS, name="in_proj_prompt")
    conv_s, z_s, cq_s, ckv_s, kba_s = fused_linear([xs], [w_in_p], gain=norm_mix[L], splits=IN_SPLITS, name="in_proj_sample")

    y_gdn_p, gdn_state_p = gdn_prompt(conv_p.reshape(bp, t, CONV_CH), z_p.reshape(bp, t, n_gdn),
                                      kba_p.reshape(bp, t, LANES), conv_w[L], alog_row, dtb_row, gdn_norm[L])
    conv_state_p = conv_p.reshape(bp, t, CONV_CH)[:, t - (CONV_WIDTH - 1):, :]
    gdn_state_s, conv_state_s, y_gdn_s = gdn_sample(conv_s, state_conv[L], kba_s, z_s, state_gdn[L],
                                                    conv_w[L], alog_row, dtb_row, gdn_norm[L])

    q_p, k_p, v_p, lat_p, kpe_p = mla_prep(cq_p, ckv_p, kba_p, cos_p, sin_p, q_norm[L], kv_norm[L],
                                           wuq_p, wuk, wuv, seq=t, q_dtype=BF16)
    y_mla_p = mla_flash(q_p.reshape(bp, t, -1), k_p.reshape(bp, t, -1), v_p.reshape(bp, t, -1))
    q_s, _, _, lat_s, kpe_s = mla_prep(cq_s, ckv_s, kba_s, cos_s, sin_s, q_norm[L], kv_norm[L],
                                       wuq_p, wuk, wuv, seq=bs, q_dtype=F32)
    q_s4 = q_s.reshape(bs, MLA_HEADS, MLA_QK_PAD)
    qlat = jnp.concatenate(
        [fused_linear([q_s4[:, h, :MLA_NOPE]], [wuk_t[h]], name=f"absorb_q{h}")[0][:, None, :]
         for h in range(MLA_HEADS)], axis=1)
    head_pad = ((0, 0), (0, DEC_HEAD_PAD - MLA_HEADS), (0, 0))
    qlat8 = jnp.pad(qlat, head_pad)
    qpe8 = jnp.pad(q_s4[:, :, MLA_NOPE:MLA_NOPE + MLA_ROPE], head_pad)
    y_mla_s = mla_decode(page_table, qlat8, qpe8, lat_s, kpe_s, wuv,
                         cache_latent, jnp.swapaxes(cache_krope, 2, 3)).reshape(bs, -1)

    h_p, qx_p = out_proj_xq(y_gdn_p.reshape(mp, n_gdn), y_mla_p.reshape(mp, -1), w_out_b[:n_gdn], w_out_b[n_gdn:],
                            xp, norm_x[L], w_xq_b, name="out_proj_xq_prompt")
    h_s, qx_s = out_proj_xq(y_gdn_s, y_mla_s, w_out_b[:n_gdn], w_out_b[n_gdn:], xs, norm_x[L], w_xq_b,
                            name="out_proj_xq_sample")

    mem_tokens = mem_prompt.shape[1]
    mk_f, mv_f, mk_b, mv_b = fused_linear([mem_prompt.reshape(bp * mem_tokens, d)], [w_xkv_b], gain=mem_norm[L],
                                          splits=(d, d), bf16_copies=True, name="memory_kv")
    ox_p = xattn_prompt(qx_p.reshape(bp, t, d), mk_b.reshape(bp, mem_tokens, d), mv_b.reshape(bp, mem_tokens, d))
    ox_s = xattn_sample(qx_s.reshape(bs, X_HEADS, d // X_HEADS), cache_mem_k[L], cache_mem_v[L])

    h2_p, hn_p, gate_p, ids_p, rank_p, cnt_p = post_xattn(ox_p.reshape(mp, d), w_xo_b, h_p, norm_ffn[L],
                                                          wr_hi, wr_lo, br_row, jnp.zeros((1, LANES), F32))
    h2_s, hn_s, gate_s, ids_s, rank_s, cnt_all = post_xattn(ox_s.reshape(bs, d), w_xo_b, h_s, norm_ffn[L],
                                                            wr_hi, wr_lo, br_row, cnt_p)

    n_tiles, tile_expert, tile_valid, pad_start = _moe_plan(cnt_all, TOP_K * (mp + bs))
    dest_p = _dest_rows(ids_p, rank_p, pad_start)
    dest_s = _dest_rows(ids_s, rank_s, pad_start)
    x_grouped = jnp.zeros((n_tiles * MOE_TILE, d // LANES, LANES), F32)
    x_grouped = moe_dispatch(dest_p, hn_p, x_grouped)
    x_grouped = moe_dispatch(dest_s, hn_s, x_grouped)
    y_sorted = moe_experts(tile_expert, tile_valid, x_grouped, w_e1[L], b1, w_e2[L], b2)
    y_prompt = moe_combine(dest_p, y_sorted, h2_p, gate_p, norm_final)
    y_sample = moe_combine(dest_s, y_sorted, h2_s, gate_s, norm_final)

    x_heads = X_HEADS
    return (y_prompt.reshape(bp, t, d), y_sample.reshape(bs, 1, d),
            gdn_state_p[None], conv_state_p[None],
            lat_p.reshape(bp, 1, t, MLA_KV_RANK), kpe_p.reshape(bp, 1, t, MLA_ROPE),
            mk_f.reshape(1, bp, mem_tokens, x_heads, d // x_heads), mv_f.reshape(1, bp, mem_tokens, x_heads, d // x_heads),
            gdn_state_s[None], conv_state_s[None],
            lat_s.reshape(bs, 1, 1, MLA_KV_RANK), kpe_s.reshape(bs, 1, 1, MLA_ROPE))
```

```python
import functools

import jax
import jax.numpy as jnp
import numpy as np
from jax import lax
from jax.experimental import pallas as pl
from jax.experimental.pallas import tpu as pltpu

F32 = jnp.float32
BF16 = jnp.bfloat16
I32 = jnp.int32

NORM_EPS = 1e-6
LANES = 128
SUBLANES = 8
VMEM_LIMIT = 48 * 1024 * 1024
MOE_VMEM_LIMIT = 56 * 1024 * 1024

GDN_HEADS = 4
GDN_D = 128
CONV_WIDTH = 4
CONV_CH = 3 * GDN_HEADS * GDN_D
GDN_BLOCK = 128
MLA_HEADS = 4
MLA_NOPE = 128
MLA_ROPE = 64
MLA_V = 128
MLA_Q_RANK = 384
MLA_KV_RANK = 256
MLA_QK_PAD = 256
ROPE_THETA = 10000.0
MLA_SCALE = (MLA_NOPE + MLA_ROPE) ** -0.5
X_HEADS = 4
N_EXPERTS = 32
TOP_K = 4
SWIGLU_LIMIT = 7.0
SWIGLU_ALPHA = 1.702
BETA_LANE = MLA_ROPE
DECAY_LANE = MLA_ROPE + GDN_HEADS
NEG_BIG = -1e30
LOG2_E = 1.4426950408889634


def _cparams(*sem):
    return pltpu.CompilerParams(dimension_semantics=sem, vmem_limit_bytes=VMEM_LIMIT)


def _rms(x, gain):
    return x * lax.rsqrt(jnp.mean(x * x, axis=-1, keepdims=True) + NORM_EPS) * gain


def _mm(a, b):
    return jnp.dot(a.astype(BF16), b.astype(BF16), preferred_element_type=F32)


def _mm_nt(a, b):
    return lax.dot_general(a.astype(BF16), b.astype(BF16), (((1,), (1,)), ((), ())),
                           preferred_element_type=F32)


def _mm3(a, b):
    a_hi = a.astype(BF16)
    b_hi = b.astype(BF16)
    a_lo = (a - a_hi.astype(F32)).astype(BF16)
    b_lo = (b - b_hi.astype(F32)).astype(BF16)
    return jnp.dot(jnp.concatenate([a_hi, a_lo, a_hi], axis=1), jnp.concatenate([b_hi, b_hi, b_lo], axis=0),
                   preferred_element_type=F32)


def _bmm(a, b):
    return lax.dot_general(a.astype(BF16), b.astype(BF16), (((2,), (1,)), ((0,), (0,))), preferred_element_type=F32)


def _bmm_nt(a, b):
    return lax.dot_general(a.astype(BF16), b.astype(BF16), (((2,), (2,)), ((0,), (0,))), preferred_element_type=F32)


def _bmm3(a, b):
    a_hi = a.astype(BF16)
    b_hi = b.astype(BF16)
    a_lo = (a - a_hi.astype(F32)).astype(BF16)
    b_lo = (b - b_hi.astype(F32)).astype(BF16)
    return lax.dot_general(jnp.concatenate([a_hi, a_lo, a_hi], axis=2), jnp.concatenate([b_hi, b_hi, b_lo], axis=1),
                           (((2,), (1,)), ((0,), (0,))), preferred_element_type=F32)


def _sigmoid(x):
    return 1.0 / (1.0 + jnp.exp(-x))


def _softplus(x):
    return jnp.maximum(x, 0.0) + jnp.log1p(jnp.exp(-jnp.abs(x)))


def _linear_kernel(*refs, n_in, has_gain, has_res, splits):
    a_refs = refs[:n_in]
    w_refs = refs[n_in:2 * n_in]
    pos = 2 * n_in
    g_ref = refs[pos] if has_gain else None
    pos += int(has_gain)
    r_ref = refs[pos] if has_res else None
    pos += int(has_res)
    out_refs = refs[pos:]
    a0 = a_refs[0][...]
    if has_gain:
        a0 = _rms(a0.astype(F32), g_ref[...])
    acts = [a0.astype(BF16)] + [a[...].astype(BF16) for a in a_refs[1:]]
    off = 0
    for i, width in enumerate(splits):
        acc = None
        for a, w in zip(acts, w_refs):
            d = jnp.dot(a, w[:, off:off + width], preferred_element_type=F32)
            acc = d if acc is None else acc + d
        if has_res:
            acc = acc + r_ref[:, off:off + width]
        for o_ref in out_refs[i::len(splits)]:
            o_ref[...] = acc.astype(o_ref.dtype)
        off += width


def fused_linear(acts, weights, *, gain=None, residual=None, splits=None, out_dtypes=None,
                 bf16_copies=False, tm=256, name="fused_linear"):
    m = acts[0].shape[0]
    n = weights[0].shape[1]
    tm = min(tm, m)
    assert m % tm == 0
    splits = tuple(splits) if splits is not None else (n,)
    assert sum(splits) == n and all(s % LANES == 0 for s in splits)
    out_dtypes = tuple(out_dtypes) if out_dtypes is not None else (F32,) * len(splits)
    out_widths = splits
    if bf16_copies:
        out_widths = splits + splits
        out_dtypes = out_dtypes + (BF16,) * len(splits)
    in_specs = [pl.BlockSpec((tm, a.shape[1]), lambda i: (i, 0)) for a in acts]
    in_specs += [pl.BlockSpec(w.shape, lambda i: (0, 0)) for w in weights]
    args = list(acts) + list(weights)
    if gain is not None:
        in_specs.append(pl.BlockSpec((1, gain.shape[-1]), lambda i: (0, 0)))
        args.append(gain.reshape(1, -1))
    if residual is not None:
        in_specs.append(pl.BlockSpec((tm, n), lambda i: (i, 0)))
        args.append(residual)
    outs = pl.pallas_call(
        functools.partial(_linear_kernel, n_in=len(acts), has_gain=gain is not None,
                          has_res=residual is not None, splits=splits),
        out_shape=[jax.ShapeDtypeStruct((m, s), dt) for s, dt in zip(out_widths, out_dtypes)],
        grid=(m // tm,),
        in_specs=in_specs,
        out_specs=[pl.BlockSpec((tm, s), lambda i: (i, 0)) for s in out_widths],
        compiler_params=_cparams("parallel"),
        name=name,
    )(*args)
    return outs


def _out_proj_xq_kernel(yg_ref, ym_ref, wg_ref, wm_ref, x_ref, gx_ref, wxq_ref, h_ref, qx_ref):
    h = (x_ref[...] + jnp.dot(yg_ref[...], wg_ref[...], preferred_element_type=F32)
         + jnp.dot(ym_ref[...], wm_ref[...], preferred_element_type=F32))
    h_ref[...] = h
    qx_ref[...] = jnp.dot(_rms(h, gx_ref[...]).astype(BF16), wxq_ref[...],
                          preferred_element_type=F32).astype(qx_ref.dtype)


def out_proj_xq(y_gdn, y_mla, w_gdn, w_mla, x, norm_x, w_xq, *, tm=256, name="out_proj_xq"):
    m, d = x.shape
    tm = min(tm, m)
    assert m % tm == 0
    row = lambda i: (i, 0)
    const = lambda i: (0, 0)
    return pl.pallas_call(
        _out_proj_xq_kernel,
        out_shape=[jax.ShapeDtypeStruct((m, d), F32), jax.ShapeDtypeStruct((m, w_xq.shape[1]), BF16)],
        grid=(m // tm,),
        in_specs=[pl.BlockSpec((tm, y_gdn.shape[1]), row), pl.BlockSpec((tm, y_mla.shape[1]), row),
                  pl.BlockSpec(w_gdn.shape, const), pl.BlockSpec(w_mla.shape, const),
                  pl.BlockSpec((tm, d), row), pl.BlockSpec((1, d), const), pl.BlockSpec(w_xq.shape, const)],
        out_specs=[pl.BlockSpec((tm, d), row), pl.BlockSpec((tm, w_xq.shape[1]), row)],
        compiler_params=_cparams("parallel"),
        name=name,
    )(y_gdn, y_mla, w_gdn, w_mla, x, norm_x.reshape(1, -1), w_xq)


def _gate_values(kba, alog_row, dtb_row):
    beta = _sigmoid(kba)
    g = -jnp.exp(alog_row) * _softplus(kba + dtb_row)
    return beta, g


def _l2norm(x):
    return x * lax.rsqrt(jnp.sum(x * x, axis=-1, keepdims=True) + NORM_EPS)


GDN_PROMPT_BB = 2


def _gdn_prompt_kernel(x_ref, z_ref, kba_ref, cw_ref, alog_ref, dtb_ref, gn_ref,
                       y_ref, s_out_ref, xbuf, state):
    t = pl.program_id(1)
    nt = pl.num_programs(1)

    @pl.when(t == 0)
    def _():
        xbuf[:, 0:SUBLANES, :] = jnp.zeros((GDN_PROMPT_BB, SUBLANES, CONV_CH), F32)
        state[...] = jnp.zeros(state.shape, F32)

    blk = GDN_BLOCK
    nh = GDN_HEADS * GDN_D
    row = lax.broadcasted_iota(I32, (blk, blk), 0)
    col = lax.broadcasted_iota(I32, (blk, blk), 1)
    qs, ks, vs, betas, gcols, grows = [], [], [], [], [], []
    for bi in range(GDN_PROMPT_BB):
        c, beta_all, gc = _gdn_conv_and_gates(x_ref.at[bi], kba_ref.at[bi], cw_ref, alog_ref, dtb_ref, xbuf.at[bi], row)
        gc_t = gc.T
        for h in range(GDN_HEADS):
            qs.append(c[:, h * GDN_D:(h + 1) * GDN_D])
            ks.append(c[:, nh + h * GDN_D:nh + (h + 1) * GDN_D])
            vs.append(c[:, 2 * nh + h * GDN_D:2 * nh + (h + 1) * GDN_D])
            betas.append(beta_all[:, BETA_LANE + h:BETA_LANE + h + 1])
            gcols.append(gc[:, DECAY_LANE + h:DECAY_LANE + h + 1])
            grows.append(gc_t[DECAY_LANE + h:DECAY_LANE + h + 1, :])

    q = _l2norm(jnp.stack(qs)) * (GDN_D ** -0.5)
    k = _l2norm(jnp.stack(ks))
    v = jnp.stack(vs)
    bcol = jnp.stack(betas)
    gcol = jnp.stack(gcols)
    grow = jnp.stack(grows)
    decay = jnp.exp(jnp.where((row >= col)[None], gcol - grow, NEG_BIG))
    kb = k * bcol
    vb = v * bcol
    a = jnp.where((row > col)[None], _bmm_nt(kb, k) * decay, 0.0)
    x = (row == col).astype(F32)[None] - a
    p = _bmm3(a, a)
    x = x + _bmm3(x, p)
    for _ in range(5):
        p = _bmm3(p, p)
        x = x + _bmm3(x, p)
    egc = jnp.exp(gcol)
    u = _bmm(x, vb)
    w = _bmm(x, kb * egc)
    intra = _bmm_nt(q, k) * decay
    s_all = state[...].reshape(GDN_PROMPT_BB * GDN_HEADS, GDN_D, GDN_D)
    v_new = u - _bmm(w, s_all)
    o = _bmm(q * egc, s_all) + _bmm(intra, v_new)
    g_last = gcol[:, blk - 1:blk, :]
    kd = k * jnp.exp(g_last - gcol)
    s_new = s_all * jnp.exp(g_last) + _bmm(jnp.swapaxes(kd, 1, 2), v_new)
    state[...] = s_new.reshape(state.shape)
    o = _rms(o, gn_ref[...])
    for bi in range(GDN_PROMPT_BB):
        z = z_ref[bi]
        for h in range(GDN_HEADS):
            zz = z[:, h * GDN_D:(h + 1) * GDN_D]
            y_ref[bi, :, h * GDN_D:(h + 1) * GDN_D] = (o[bi * GDN_HEADS + h] * (zz * _sigmoid(zz))).astype(y_ref.dtype)

    @pl.when(t == nt - 1)
    def _():
        s_out_ref[...] = state[...]


def _gdn_conv_and_gates(x_ref, kba_ref, cw_ref, alog_ref, dtb_ref, xbuf, row):
    blk = GDN_BLOCK
    hist = CONV_WIDTH - 1
    xbuf[SUBLANES:SUBLANES + blk, :] = x_ref[...]
    cw = cw_ref[...]
    conv = xbuf[SUBLANES - hist:SUBLANES - hist + blk, :] * cw[0:1, :]
    for j in range(1, CONV_WIDTH):
        conv = conv + xbuf[SUBLANES - hist + j:SUBLANES - hist + j + blk, :] * cw[j:j + 1, :]
    xbuf[SUBLANES - hist:SUBLANES, :] = xbuf[SUBLANES + blk - hist:SUBLANES + blk, :]
    c = conv * _sigmoid(conv)
    beta_all, g_all = _gate_values(kba_ref[...], alog_ref[...], dtb_ref[...])
    gc = g_all
    shift = 1
    while shift < blk:
        gc = gc + jnp.where(row >= shift, pltpu.roll(gc, shift, 0), 0.0)
        shift *= 2
    return c, beta_all, gc


def gdn_prompt(conv_in, z, kba, conv_w, alog_row, dtb_row, gdn_norm):
    b, t, _ = conv_in.shape
    bb = GDN_PROMPT_BB
    assert t % GDN_BLOCK == 0 and b % bb == 0
    nt = t // GDN_BLOCK
    y, s = pl.pallas_call(
        _gdn_prompt_kernel,
        out_shape=[jax.ShapeDtypeStruct((b, t, GDN_HEADS * GDN_D), BF16),
                   jax.ShapeDtypeStruct((b, GDN_HEADS, GDN_D, GDN_D), F32)],
        grid=(b // bb, nt),
        in_specs=[pl.BlockSpec((bb, GDN_BLOCK, CONV_CH), lambda i, j: (i, j, 0)),
                  pl.BlockSpec((bb, GDN_BLOCK, GDN_HEADS * GDN_D), lambda i, j: (i, j, 0)),
                  pl.BlockSpec((bb, GDN_BLOCK, LANES), lambda i, j: (i, j, 0)),
                  pl.BlockSpec((CONV_WIDTH, CONV_CH), lambda i, j: (0, 0)),
                  pl.BlockSpec((1, LANES), lambda i, j: (0, 0)),
                  pl.BlockSpec((1, LANES), lambda i, j: (0, 0)),
                  pl.BlockSpec((1, GDN_D), lambda i, j: (0, 0))],
        out_specs=[pl.BlockSpec((bb, GDN_BLOCK, GDN_HEADS * GDN_D), lambda i, j: (i, j, 0)),
                   pl.BlockSpec((bb, GDN_HEADS, GDN_D, GDN_D), lambda i, j: (i, 0, 0, 0))],
        scratch_shapes=[pltpu.VMEM((bb, SUBLANES + GDN_BLOCK, CONV_CH), F32),
                        pltpu.VMEM((bb, GDN_HEADS, GDN_D, GDN_D), F32)],
        compiler_params=_cparams("parallel", "arbitrary"),
        name="gdn_prompt",
    )(conv_in, z, kba, conv_w, alog_row, dtb_row, gdn_norm.reshape(1, -1))
    return y, s


GDN_SAMPLE_BB = 8


def _gdn_sample_kernel(x_ref, sc_ref, kba_ref, z_ref, s_ref, cw_ref, alog_ref, dtb_ref, gn_ref,
                       s_out_ref, sc_out_ref, y_ref, tbuf):
    bb = GDN_SAMPLE_BB
    x = x_ref[...]
    cw = cw_ref[...]
    conv = x * cw[CONV_WIDTH - 1:CONV_WIDTH, :]
    for j in range(CONV_WIDTH - 1):
        conv = conv + sc_ref[:, j, :] * cw[j:j + 1, :]
    for j in range(CONV_WIDTH - 2):
        sc_out_ref[:, j, :] = sc_ref[:, j + 1, :]
    sc_out_ref[:, CONV_WIDTH - 2, :] = x
    c = conv * _sigmoid(conv)
    beta_all, g_all = _gate_values(kba_ref[...], alog_ref[...], dtb_ref[...])
    eg_all = jnp.exp(g_all)
    z = z_ref[...]
    gn = gn_ref[...]
    nh = GDN_HEADS * GDN_D
    tbuf[...] = jnp.zeros(tbuf.shape, F32)
    for h in range(GDN_HEADS):
        q = _l2norm(c[:, h * GDN_D:(h + 1) * GDN_D]) * (GDN_D ** -0.5)
        k = _l2norm(c[:, nh + h * GDN_D:nh + (h + 1) * GDN_D])
        v = c[:, 2 * nh + h * GDN_D:2 * nh + (h + 1) * GDN_D]
        tbuf[0:bb, :] = q
        q_t = tbuf[...].T
        tbuf[0:bb, :] = k
        k_t = tbuf[...].T
        for b in range(bb):
            qcol = q_t[:, b:b + 1]
            kcol = k_t[:, b:b + 1]
            eg = eg_all[b:b + 1, DECAY_LANE + h:DECAY_LANE + h + 1]
            beta = beta_all[b:b + 1, BETA_LANE + h:BETA_LANE + h + 1]
            s1 = s_ref[b, h] * eg
            pred = jnp.sum(s1 * kcol, axis=0, keepdims=True)
            u = (v[b:b + 1, :] - pred) * beta
            s2 = s1 + kcol * u
            s_out_ref[b, h] = s2
            o = jnp.sum(s2 * qcol, axis=0, keepdims=True)
            zz = z[b:b + 1, h * GDN_D:(h + 1) * GDN_D]
            y_ref[b:b + 1, h * GDN_D:(h + 1) * GDN_D] = (_rms(o, gn) * (zz * _sigmoid(zz))).astype(y_ref.dtype)


def gdn_sample(conv_in, state_conv, kba, z, state_gdn, conv_w, alog_row, dtb_row, gdn_norm):
    b = conv_in.shape[0]
    bb = GDN_SAMPLE_BB
    assert b % bb == 0
    hist = CONV_WIDTH - 1
    return pl.pallas_call(
        _gdn_sample_kernel,
        out_shape=[jax.ShapeDtypeStruct(state_gdn.shape, F32),
                   jax.ShapeDtypeStruct(state_conv.shape, F32),
                   jax.ShapeDtypeStruct((b, GDN_HEADS * GDN_D), BF16)],
        grid=(b // bb,),
        in_specs=[pl.BlockSpec((bb, CONV_CH), lambda i: (i, 0)),
                  pl.BlockSpec((bb, hist, CONV_CH), lambda i: (i, 0, 0)),
                  pl.BlockSpec((bb, LANES), lambda i: (i, 0)),
                  pl.BlockSpec((bb, GDN_HEADS * GDN_D), lambda i: (i, 0)),
                  pl.BlockSpec((bb, GDN_HEADS, GDN_D, GDN_D), lambda i: (i, 0, 0, 0)),
                  pl.BlockSpec((CONV_WIDTH, CONV_CH), lambda i: (0, 0)),
                  pl.BlockSpec((1, LANES), lambda i: (0, 0)),
                  pl.BlockSpec((1, LANES), lambda i: (0, 0)),
                  pl.BlockSpec((1, GDN_D), lambda i: (0, 0))],
        out_specs=[pl.BlockSpec((bb, GDN_HEADS, GDN_D, GDN_D), lambda i: (i, 0, 0, 0)),
                   pl.BlockSpec((bb, hist, CONV_CH), lambda i: (i, 0, 0)),
                   pl.BlockSpec((bb, GDN_HEADS * GDN_D), lambda i: (i, 0))],
        scratch_shapes=[pltpu.VMEM((GDN_D, GDN_D), F32)],
        compiler_params=_cparams("parallel"),
        name="gdn_sample",
    )(conv_in, state_conv, kba, z, state_gdn, conv_w, alog_row, dtb_row, gdn_norm.reshape(1, -1))


def _rope128(x, cos, sin):
    half = MLA_ROPE // 2
    lane = lax.broadcasted_iota(I32, x.shape, 1)
    swapped = jnp.where(lane < half, pltpu.roll(x, LANES - half, 1), pltpu.roll(x, half, 1))
    return x * cos + swapped * sin


def _mla_prep_kernel(cq_ref, ckv_ref, kba_ref, cos_ref, sin_ref, qn_ref, kvn_ref, wuq_ref, wuk_ref, wuv_ref,
                     q_ref, k_ref, v_ref, lat_ref, kpe_ref):
    cos = cos_ref[...]
    sin = sin_ref[...]
    qn = _rms(cq_ref[...], qn_ref[...]).astype(BF16)
    lat = _rms(ckv_ref[...], kvn_ref[...])
    lat_ref[...] = lat
    lat_b = lat.astype(BF16)
    kpe = _rope128(kba_ref[...], cos, sin)
    kpe_ref[...] = kpe[:, :MLA_ROPE]
    for h in range(MLA_HEADS):
        lo = h * MLA_QK_PAD
        q_ref[:, lo:lo + MLA_NOPE] = jnp.dot(
            qn, wuq_ref[:, lo:lo + MLA_NOPE], preferred_element_type=F32).astype(q_ref.dtype)
        q_pe = jnp.dot(qn, wuq_ref[:, lo + MLA_NOPE:lo + MLA_QK_PAD], preferred_element_type=F32)
        q_ref[:, lo + MLA_NOPE:lo + MLA_QK_PAD] = _rope128(q_pe, cos, sin).astype(q_ref.dtype)
        k_ref[:, lo:lo + MLA_NOPE] = jnp.dot(
            lat_b, wuk_ref[:, h * MLA_NOPE:(h + 1) * MLA_NOPE], preferred_element_type=F32).astype(k_ref.dtype)
        k_ref[:, lo + MLA_NOPE:lo + MLA_QK_PAD] = kpe.astype(k_ref.dtype)
    v_ref[...] = jnp.dot(lat_b, wuv_ref[...], preferred_element_type=F32).astype(v_ref.dtype)


def mla_prep(c_q, c_kv, kba, cos_tab, sin_tab, q_norm, kv_norm, wuq_p, wuk, wuv, *, seq, q_dtype, tm=256):
    m = c_q.shape[0]
    tm = min(tm, m, seq)
    assert m % tm == 0 and seq % tm == 0
    nseq = seq // tm
    hq = MLA_HEADS * MLA_QK_PAD
    row = lambda i: (i, 0)
    const = lambda i: (0, 0)
    return pl.pallas_call(
        _mla_prep_kernel,
        out_shape=[jax.ShapeDtypeStruct((m, hq), q_dtype),
                   jax.ShapeDtypeStruct((m, hq), BF16),
                   jax.ShapeDtypeStruct((m, MLA_HEADS * MLA_V), BF16),
                   jax.ShapeDtypeStruct((m, MLA_KV_RANK), F32),
                   jax.ShapeDtypeStruct((m, MLA_ROPE), F32)],
        grid=(m // tm,),
        in_specs=[pl.BlockSpec((tm, MLA_Q_RANK), row),
                  pl.BlockSpec((tm, MLA_KV_RANK), row),
                  pl.BlockSpec((tm, LANES), row),
                  pl.BlockSpec((tm, LANES), lambda i: (i % nseq, 0)),
                  pl.BlockSpec((tm, LANES), lambda i: (i % nseq, 0)),
                  pl.BlockSpec((1, MLA_Q_RANK), const),
                  pl.BlockSpec((1, MLA_KV_RANK), const),
                  pl.BlockSpec(wuq_p.shape, const),
                  pl.BlockSpec(wuk.shape, const),
                  pl.BlockSpec(wuv.shape, const)],
        out_specs=[pl.BlockSpec((tm, hq), row),
                   pl.BlockSpec((tm, hq), row),
                   pl.BlockSpec((tm, MLA_HEADS * MLA_V), row),
                   pl.BlockSpec((tm, MLA_KV_RANK), row),
                   pl.BlockSpec((tm, MLA_ROPE), row)],
        compiler_params=_cparams("parallel"),
        name="mla_prep",
    )(c_q, c_kv, kba, cos_tab, sin_tab, q_norm.reshape(1, -1), kv_norm.reshape(1, -1), wuq_p, wuk, wuv)


def _flash_kernel(q_ref, k_ref, v_ref, o_ref, m_s, l_s, acc_s, *, tq):
    qi = pl.program_id(2)
    m_s[...] = jnp.full(m_s.shape, NEG_BIG, F32)
    l_s[...] = jnp.zeros(l_s.shape, F32)
    acc_s[...] = jnp.zeros(acc_s.shape, F32)

    def block(j, masked):
        k0 = pl.multiple_of(j * tq, tq)
        for h in range(FLASH_HEADS):
            q = q_ref[0, :, h * MLA_QK_PAD:(h + 1) * MLA_QK_PAD]
            s = lax.dot_general(q, k_ref[0, pl.ds(k0, tq), h * MLA_QK_PAD:(h + 1) * MLA_QK_PAD],
                                (((1,), (1,)), ((), ())), preferred_element_type=F32) * (MLA_SCALE * LOG2_E)
            if masked:
                s = jnp.where(lax.broadcasted_iota(I32, (tq, tq), 1) <= lax.broadcasted_iota(I32, (tq, tq), 0),
                              s, NEG_BIG)
            m_prev = m_s[h]
            m_new = jnp.maximum(m_prev, jnp.max(s, axis=-1, keepdims=True))
            corr = jnp.exp2(m_prev - m_new)
            p = jnp.exp2(s - m_new)
            l_s[h] = l_s[h] * corr + jnp.sum(p, axis=-1, keepdims=True)
            acc_s[h] = acc_s[h] * corr + jnp.dot(p.astype(BF16), v_ref[0, pl.ds(k0, tq), h * MLA_V:(h + 1) * MLA_V],
                                                 preferred_element_type=F32)
            m_s[h] = m_new

    def below_diagonal(j, _):
        block(j, masked=False)
        return 0

    lax.fori_loop(0, qi, below_diagonal, 0)
    block(qi, masked=True)
    for h in range(FLASH_HEADS):
        o_ref[0, :, h * MLA_V:(h + 1) * MLA_V] = (acc_s[h] / l_s[h]).astype(o_ref.dtype)


FLASH_HEADS = 2


def mla_flash(q, k, v, *, tq=512):
    b, t, _ = q.shape
    tq = min(tq, t)
    hs = FLASH_HEADS
    assert t % tq == 0 and MLA_HEADS % hs == 0
    seq_map = lambda bi, h, qi: (bi, 0, h)
    return pl.pallas_call(
        functools.partial(_flash_kernel, tq=tq),
        out_shape=jax.ShapeDtypeStruct((b, t, MLA_HEADS * MLA_V), BF16),
        grid=(b, MLA_HEADS // hs, t // tq),
        in_specs=[pl.BlockSpec((1, tq, hs * MLA_QK_PAD), lambda bi, h, qi: (bi, qi, h)),
                  pl.BlockSpec((1, t, hs * MLA_QK_PAD), seq_map),
                  pl.BlockSpec((1, t, hs * MLA_V), seq_map)],
        out_specs=pl.BlockSpec((1, tq, hs * MLA_V), lambda bi, h, qi: (bi, qi, h)),
        scratch_shapes=[pltpu.VMEM((hs, tq, 1), F32), pltpu.VMEM((hs, tq, 1), F32), pltpu.VMEM((hs, tq, MLA_V), F32)],
        compiler_params=_cparams("parallel", "parallel", "arbitrary"),
        name="mla_flash",
    )(q, k, v)


DEC_HEAD_PAD = 8
DEC_GROUP = 16
DEC_SLOTS = 8


def _decode_kernel(pt_ref, ptn_ref, qlat_ref, qpe_ref, latn_ref, kpen_ref, wuv_ref, lat_hbm, kpe_hbm,
                   o_ref, latbuf, kpebuf, sem, *, nb, n_pages, page):
    b = pl.program_id(0)
    grp = DEC_GROUP
    ns = DEC_SLOTS
    n_groups = n_pages // grp
    n_outer = n_groups // ns

    def page_copies(pg, slot, j):
        return (pltpu.make_async_copy(lat_hbm.at[pg, 0], latbuf.at[slot, j], sem.at[0, slot]),
                pltpu.make_async_copy(kpe_hbm.at[pg, 0], kpebuf.at[slot, j], sem.at[1, slot]))

    def start_group(tbl_ref, g, slot):
        for j in range(grp):
            for cp in page_copies(tbl_ref[0, 0, g * grp + j], slot, j):
                cp.start()

    def wait_group(slot):
        for j in range(grp):
            for cp in page_copies(0, slot, j):
                cp.wait()

    @pl.when(b == 0)
    def _():
        for s in range(ns - 1):
            start_group(pt_ref, s, s)

    qlat = qlat_ref[0]
    qpe = qpe_ref[0]
    latn = latn_ref[0]
    kpen = kpen_ref[0]
    qlat_b = qlat.astype(BF16)
    qpe_b = qpe.astype(BF16)
    s_new = (jnp.sum(qlat * latn, axis=-1, keepdims=True)
             + jnp.sum(qpe * kpen, axis=-1, keepdims=True)) * MLA_SCALE
    m0 = s_new
    l0 = jnp.ones_like(s_new)
    acc0 = jnp.broadcast_to(latn, qlat.shape)

    def consume(slot, carry):
        m, l, acc = carry
        lat = latbuf[slot].reshape(grp * page, MLA_KV_RANK).astype(BF16)
        kpe_t = jnp.concatenate([kpebuf[slot, j] for j in range(grp)], axis=1).astype(BF16)
        s = (lax.dot_general(qlat_b, lat, (((1,), (1,)), ((), ())), preferred_element_type=F32)
             + jnp.dot(qpe_b, kpe_t, preferred_element_type=F32)) * MLA_SCALE
        m_new = jnp.maximum(m, jnp.max(s, axis=-1, keepdims=True))
        corr = jnp.exp(m - m_new)
        p = jnp.exp(s - m_new)
        l = l * corr + jnp.sum(p, axis=-1, keepdims=True)
        acc = acc * corr + jnp.dot(p.astype(BF16), lat, preferred_element_type=F32)
        return m_new, l, acc

    def ring_body(i, carry):
        for j in range(ns):
            ahead = (j + ns - 1) % ns
            if j == 0:
                start_group(pt_ref, i * ns + ns - 1, ahead)
            else:
                @pl.when(i + 1 < n_outer)
                def _():
                    start_group(pt_ref, (i + 1) * ns + ahead, ahead)

                @pl.when(jnp.logical_and(i + 1 >= n_outer, b + 1 < nb))
                def _():
                    start_group(ptn_ref, ahead, ahead)
            wait_group(j)
            carry = consume(j, carry)
        return carry

    m, l, acc = lax.fori_loop(0, n_outer, ring_body, (m0, l0, acc0))
    o_lat = (acc / l).astype(BF16)
    res = jnp.dot(o_lat, wuv_ref[...], preferred_element_type=F32)
    o_ref[0] = jnp.concatenate(
        [res[h:h + 1, h * MLA_V:(h + 1) * MLA_V] for h in range(MLA_HEADS)], axis=1).astype(o_ref.dtype)


def mla_decode(page_table, qlat8, qpe8, lat_new, kpe_new, wuv, cache_latent, cache_krope_t):
    b, n_pages = page_table.shape
    page = cache_latent.shape[2]
    assert n_pages % (DEC_SLOTS * DEC_GROUP) == 0
    pt3 = page_table.reshape(b, 1, n_pages)
    smem_row = lambda f: pl.BlockSpec((1, 1, n_pages), f, memory_space=pltpu.SMEM)
    return pl.pallas_call(
        functools.partial(_decode_kernel, nb=b, n_pages=n_pages, page=page),
        out_shape=jax.ShapeDtypeStruct((b, 1, MLA_HEADS * MLA_V), BF16),
        grid=(b,),
        in_specs=[smem_row(lambda i: (i, 0, 0)),
                  smem_row(lambda i: (jnp.minimum(i + 1, b - 1), 0, 0)),
                  pl.BlockSpec((1, DEC_HEAD_PAD, MLA_KV_RANK), lambda i: (i, 0, 0)),
                  pl.BlockSpec((1, DEC_HEAD_PAD, MLA_ROPE), lambda i: (i, 0, 0)),
                  pl.BlockSpec((1, 1, MLA_KV_RANK), lambda i: (i, 0, 0)),
                  pl.BlockSpec((1, 1, MLA_ROPE), lambda i: (i, 0, 0)),
                  pl.BlockSpec(wuv.shape, lambda i: (0, 0)),
                  pl.BlockSpec(memory_space=pl.ANY),
                  pl.BlockSpec(memory_space=pl.ANY)],
        out_specs=pl.BlockSpec((1, 1, MLA_HEADS * MLA_V), lambda i: (i, 0, 0)),
        scratch_shapes=[pltpu.VMEM((DEC_SLOTS, DEC_GROUP, page, MLA_KV_RANK), F32),
                        pltpu.VMEM((DEC_SLOTS, DEC_GROUP, MLA_ROPE, page), F32),
                        pltpu.SemaphoreType.DMA((2, DEC_SLOTS))],
        compiler_params=_cparams("arbitrary"),
        name="mla_decode",
    )(pt3, pt3, qlat8, qpe8, lat_new.reshape(b, 1, -1), kpe_new.reshape(b, 1, -1), wuv,
      cache_latent, cache_krope_t)


def _softmax_rows(s):
    m = jnp.max(s, axis=-1, keepdims=True)
    p = jnp.exp(s - m)
    return p / jnp.sum(p, axis=-1, keepdims=True)


def _xattn_prompt_kernel(q_ref, k_ref, v_ref, o_ref, *, dh):
    scale = dh ** -0.5
    for h in range(X_HEADS):
        q = q_ref[0, :, h * dh:(h + 1) * dh]
        s = lax.dot_general(q, k_ref[0, :, h * dh:(h + 1) * dh], (((1,), (1,)), ((), ())),
                            preferred_element_type=F32) * scale
        p = _softmax_rows(s).astype(BF16)
        o_ref[0, :, h * dh:(h + 1) * dh] = jnp.dot(
            p, v_ref[0, :, h * dh:(h + 1) * dh], preferred_element_type=F32).astype(o_ref.dtype)


def xattn_prompt(q, k, v, *, tq=512):
    b, t, d = q.shape
    mem = k.shape[1]
    tq = min(tq, t)
    return pl.pallas_call(
        functools.partial(_xattn_prompt_kernel, dh=d // X_HEADS),
        out_shape=jax.ShapeDtypeStruct((b, t, d), BF16),
        grid=(b, t // tq),
        in_specs=[pl.BlockSpec((1, tq, d), lambda i, j: (i, j, 0)),
                  pl.BlockSpec((1, mem, d), lambda i, j: (i, 0, 0)),
                  pl.BlockSpec((1, mem, d), lambda i, j: (i, 0, 0))],
        out_specs=pl.BlockSpec((1, tq, d), lambda i, j: (i, j, 0)),
        compiler_params=_cparams("parallel", "parallel"),
        name="xattn_prompt",
    )(q, k, v)


XATTN_SAMPLE_BB = 2


def _xattn_sample_kernel(q_ref, k_ref, v_ref, o_ref, *, dh):
    scale = dh ** -0.5
    for b in range(XATTN_SAMPLE_BB):
        q = q_ref[b].astype(F32)
        s = jnp.sum(k_ref[b] * q[None], axis=-1, keepdims=True) * scale
        m = jnp.max(s, axis=0, keepdims=True)
        p = jnp.exp(s - m)
        denom = jnp.sum(p, axis=0, keepdims=True)
        o = jnp.sum(p * v_ref[b], axis=0) / denom[0]
        o_ref[b] = o.astype(o_ref.dtype)


def xattn_sample(q, k, v):
    b, heads, dh = q.shape
    mem = k.shape[1]
    bb = XATTN_SAMPLE_BB
    assert b % bb == 0
    kv_spec = pl.BlockSpec((bb, mem, heads, dh), lambda i: (i, 0, 0, 0))
    return pl.pallas_call(
        functools.partial(_xattn_sample_kernel, dh=dh),
        out_shape=jax.ShapeDtypeStruct((b, heads, dh), BF16),
        grid=(b // bb,),
        in_specs=[pl.BlockSpec((bb, heads, dh), lambda i: (i, 0, 0)), kv_spec, kv_spec],
        out_specs=pl.BlockSpec((bb, heads, dh), lambda i: (i, 0, 0)),
        compiler_params=_cparams("parallel"),
        name="xattn_sample",
    )(q, k, v)


def _post_xattn_kernel(ox_ref, wxo_ref, h_ref, g_ref, wr_hi_ref, wr_lo_ref, br_ref, cnt_in_ref, before_ref,
                       h2_ref, hn_ref, gate_ref, idx_ref, rank_ref, cnt_ref, cnt_s):
    @pl.when(pl.program_id(0) == 0)
    def _():
        cnt_s[...] = cnt_in_ref[...]

    h2 = h_ref[...] + jnp.dot(ox_ref[...], wxo_ref[...], preferred_element_type=F32)
    h2_ref[...] = h2
    hn = _rms(h2, g_ref[...])
    hn_ref[...] = hn.reshape(hn_ref.shape)
    hn_hi = hn.astype(BF16)
    hn_lo = (hn - hn_hi.astype(F32)).astype(BF16)
    logits = (jnp.dot(hn_hi, wr_hi_ref[...], preferred_element_type=F32)
              + jnp.dot(hn_hi, wr_lo_ref[...], preferred_element_type=F32)
              + jnp.dot(hn_lo, wr_hi_ref[...], preferred_element_type=F32)) + br_ref[...]
    lane = lax.broadcasted_iota(I32, logits.shape, 1)
    logits = jnp.where(lane < N_EXPERTS, logits, NEG_BIG)
    vals, picks = [], []
    gates = jnp.zeros(logits.shape, F32)
    ids = jnp.zeros(logits.shape, I32)
    member = jnp.zeros(logits.shape, F32)
    for k in range(TOP_K):
        m = jnp.max(logits, axis=-1, keepdims=True)
        idx = jnp.min(jnp.where(logits == m, lane, LANES), axis=-1, keepdims=True)
        vals.append(m)
        picks.append(idx)
        ids = jnp.where(lane == k, idx, ids)
        member = jnp.where(lane == idx, 1.0, member)
        logits = jnp.where(lane == idx, NEG_BIG, logits)
    exps = [jnp.exp(v - vals[0]) for v in vals]
    denom = exps[0]
    for e in exps[1:]:
        denom = denom + e
    for k in range(TOP_K):
        gates = jnp.where(lane == k, exps[k] / denom, gates)
    gate_ref[...] = gates
    idx_ref[...] = ids
    prior = jnp.dot(before_ref[...], member.astype(BF16), preferred_element_type=F32) + cnt_s[...]
    ranks = jnp.zeros(logits.shape, I32)
    for k in range(TOP_K):
        rk = jnp.sum(jnp.where(lane == picks[k], prior, 0.0), axis=-1, keepdims=True)
        ranks = jnp.where(lane == k, rk.astype(I32), ranks)
    rank_ref[...] = ranks
    cnt = cnt_s[...] + jnp.sum(member, axis=0, keepdims=True)
    cnt_s[...] = cnt
    cnt_ref[...] = cnt


def post_xattn(ox, wxo, h, norm_ffn, wr_hi, wr_lo, br_row, cnt_in, *, tm=256):
    m, d = h.shape
    tm = min(tm, m)
    row = lambda i: (i, 0)
    const = lambda i: (0, 0)
    before = jnp.tril(jnp.ones((tm, tm), F32), -1).astype(BF16)
    return pl.pallas_call(
        _post_xattn_kernel,
        out_shape=[jax.ShapeDtypeStruct((m, d), F32), jax.ShapeDtypeStruct((m, d // LANES, LANES), F32),
                   jax.ShapeDtypeStruct((m, LANES), F32), jax.ShapeDtypeStruct((m, LANES), I32),
                   jax.ShapeDtypeStruct((m, LANES), I32), jax.ShapeDtypeStruct((1, LANES), F32)],
        grid=(m // tm,),
        in_specs=[pl.BlockSpec((tm, d), row), pl.BlockSpec(wxo.shape, const), pl.BlockSpec((tm, d), row),
                  pl.BlockSpec((1, d), const), pl.BlockSpec(wr_hi.shape, const), pl.BlockSpec(wr_lo.shape, const),
                  pl.BlockSpec((1, LANES), const), pl.BlockSpec((1, LANES), const), pl.BlockSpec((tm, tm), const)],
        out_specs=[pl.BlockSpec((tm, d), row), pl.BlockSpec((tm, d // LANES, LANES), lambda i: (i, 0, 0)),
                   pl.BlockSpec((tm, LANES), row), pl.BlockSpec((tm, LANES), row),
                   pl.BlockSpec((tm, LANES), row), pl.BlockSpec((1, LANES), const)],
        scratch_shapes=[pltpu.VMEM((1, LANES), F32)],
        compiler_params=_cparams("arbitrary"),
        name="post_xattn_router",
    )(ox, wxo, h, norm_ffn.reshape(1, -1), wr_hi, wr_lo, br_row, cnt_in, before)


MOE_TILE = 256


ROW_TILE = 128
MOE_FF_CHUNK = 512


def _row_dma_loop(n_tokens, make_copies, wait):
    def body(n, _):
        for k, cp in enumerate(make_copies(n)):
            cp.wait() if wait else cp.start(priority=k % 2)
        return 0
    lax.fori_loop(0, n_tokens, body, 0, unroll=4)


def _dispatch_kernel(dest_ref, x_ref, xs_in_ref, xs_ref, buf, sem, *, nt):
    del xs_in_ref
    t = pl.program_id(0)
    tm = ROW_TILE
    slot = t % 2

    def copies(s, dest_of):
        def make(n):
            return [pltpu.make_async_copy(buf.at[s, n], xs_ref.at[dest_of(n, k)], sem.at[s]) for k in range(TOP_K)]
        return make

    def wait_slot(s):
        _row_dma_loop(tm, copies(s, lambda n, k: 0), wait=True)

    if nt > 2:
        @pl.when(t >= 2)
        def _():
            wait_slot(slot)

    buf[slot] = x_ref[...]
    _row_dma_loop(tm, copies(slot, lambda n, k: dest_ref[0, 0, n * TOP_K + k]), wait=False)

    @pl.when(t == nt - 1)
    def _():
        wait_slot(slot)
        if nt > 1:
            wait_slot(1 - slot)


def moe_dispatch(dest, x3, xs):
    m = x3.shape[0]
    tm = ROW_TILE
    assert m % tm == 0
    n_tiles = m // tm
    return pl.pallas_call(
        functools.partial(_dispatch_kernel, nt=n_tiles),
        out_shape=jax.ShapeDtypeStruct(xs.shape, xs.dtype),
        grid=(n_tiles,),
        in_specs=[pl.BlockSpec((1, 1, TOP_K * tm), lambda t: (t, 0, 0), memory_space=pltpu.SMEM),
                  pl.BlockSpec((tm,) + x3.shape[1:], lambda t: (t, 0, 0)),
                  pl.BlockSpec(memory_space=pl.ANY)],
        out_specs=pl.BlockSpec(memory_space=pl.ANY),
        scratch_shapes=[pltpu.VMEM((2, tm) + x3.shape[1:], x3.dtype), pltpu.SemaphoreType.DMA((2,))],
        input_output_aliases={2: 0},
        compiler_params=_cparams("arbitrary"),
        name="moe_dispatch",
    )(dest.reshape(n_tiles, 1, TOP_K * tm), x3, xs)


def _moe_kernel(te_ref, tv_ref, x_ref, w1_ref, b1_ref, w2_ref, b2_ref, y_ref, w1b, w2s, w2p_ref, act_s):
    t = pl.program_id(0)

    @pl.when(jnp.logical_or(t == 0, te_ref[t] != te_ref[jnp.maximum(t - 1, 0)]))
    def _():
        w1b[...] = w1_ref[0].astype(BF16)
        half = LANES // 2
        n_chunks = w2s.shape[0]
        for c in range(n_chunks):
            for blk in range(w2s.shape[1] // LANES):
                r0 = blk * LANES
                w2s[c, pl.ds(r0, half, stride=2), :] = w2_ref[0, r0:r0 + half, c * LANES:(c + 1) * LANES]
                w2s[c, pl.ds(r0 + 1, half, stride=2), :] = w2_ref[0, r0 + half:r0 + LANES, c * LANES:(c + 1) * LANES]
        w2p_ref[...] = jnp.concatenate([w2s[c] for c in range(n_chunks)], axis=1).astype(BF16)

    @pl.when(tv_ref[t] > 0)
    def _():
        tm = x_ref.shape[0]
        x = x_ref[...].reshape(tm, w1b.shape[0]).astype(BF16)
        ff2 = w1b.shape[1]
        even = lax.broadcasted_iota(I32, (x.shape[0], LANES), 1) % 2 == 0
        for c in range(ff2 // MOE_FF_CHUNK):
            lo = c * MOE_FF_CHUNK
            hh = jnp.dot(x, w1b[:, lo:lo + MOE_FF_CHUNK], preferred_element_type=F32) + b1_ref[0, :, lo:lo + MOE_FF_CHUNK]
            parts = []
            for j in range(MOE_FF_CHUNK // (2 * LANES)):
                a = hh[:, (2 * j) * LANES:(2 * j + 1) * LANES]
                b = hh[:, (2 * j + 1) * LANES:(2 * j + 2) * LANES]
                glu = jnp.where(even, a, pltpu.roll(b, 1, 1))
                lin = jnp.where(even, pltpu.roll(a, LANES - 1, 1), b)
                glu = jnp.minimum(glu, SWIGLU_LIMIT)
                lin = jnp.clip(lin, -SWIGLU_LIMIT, SWIGLU_LIMIT)
                parts.append((lin + 1.0) * (glu * _sigmoid(SWIGLU_ALPHA * glu)))
            act_s[:, lo // 2:(lo + MOE_FF_CHUNK) // 2] = jnp.concatenate(parts, axis=1).astype(BF16)
        y = jnp.dot(act_s[...], w2p_ref[...], preferred_element_type=F32) + b2_ref[0]
        y_ref[...] = y.reshape(y_ref.shape)

    @pl.when(tv_ref[t] == 0)
    def _():
        y_ref[...] = jnp.zeros(y_ref.shape, F32)


def moe_experts(tile_expert, tile_valid, xs, w1, b1, w2, b2):
    n_tiles = tile_expert.shape[0]
    tm = MOE_TILE
    d = w1.shape[1]
    ff2 = w1.shape[2]
    assert ff2 % MOE_FF_CHUNK == 0 and MOE_FF_CHUNK % (2 * LANES) == 0 and xs.shape[1:] == (d // LANES, LANES)
    wmap = lambda t, te, tv: (te[t], 0, 0)
    row_tiles = pl.BlockSpec((tm, d // LANES, LANES), lambda t, te, tv: (t, 0, 0))
    grid_spec = pltpu.PrefetchScalarGridSpec(
        num_scalar_prefetch=2,
        grid=(n_tiles,),
        in_specs=[row_tiles,
                  pl.BlockSpec((1, d, ff2), wmap), pl.BlockSpec((1, 1, ff2), wmap),
                  pl.BlockSpec((1, ff2 // 2, d), wmap), pl.BlockSpec((1, 1, d), wmap)],
        out_specs=row_tiles,
        scratch_shapes=[pltpu.VMEM((d, ff2), BF16), pltpu.VMEM((d // LANES, ff2 // 2, LANES), F32),
                        pltpu.VMEM((ff2 // 2, d), BF16), pltpu.VMEM((tm, ff2 // 2), BF16)],
    )
    return pl.pallas_call(
        _moe_kernel,
        out_shape=jax.ShapeDtypeStruct(xs.shape, F32),
        grid_spec=grid_spec,
        compiler_params=pltpu.CompilerParams(dimension_semantics=("arbitrary",), vmem_limit_bytes=MOE_VMEM_LIMIT),
        name="moe_experts",
    )(tile_expert, tile_valid, xs, w1, b1, w2, b2)


def _combine_kernel(dest_ref, destn_ref, y_hbm, h2_ref, gate_ref, gain_ref, o_ref, ybuf, sem, *, nt):
    t = pl.program_id(0)
    tm = ROW_TILE
    slot = t % 2

    def copies(s, dest_of):
        def make(n):
            return [pltpu.make_async_copy(y_hbm.at[dest_of(n, k)], ybuf.at[s, k * tm + n], sem.at[s])
                    for k in range(TOP_K)]
        return make

    @pl.when(t == 0)
    def _():
        _row_dma_loop(tm, copies(0, lambda n, k: dest_ref[0, 0, n * TOP_K + k]), wait=False)

    if nt > 1:
        @pl.when(t + 1 < nt)
        def _():
            _row_dma_loop(tm, copies(1 - slot, lambda n, k: destn_ref[0, 0, n * TOP_K + k]), wait=False)

    _row_dma_loop(tm, copies(slot, lambda n, k: 0), wait=True)
    gates = gate_ref[...]
    acc = h2_ref[...]
    for k in range(TOP_K):
        yk = ybuf[slot, k * tm:(k + 1) * tm].reshape(acc.shape)
        acc = acc + gates[:, k:k + 1] * yk
    o_ref[...] = _rms(acc, gain_ref[...])


def moe_combine(dest, y_sorted, h2, gates, norm_final):
    m, d = h2.shape
    tm = ROW_TILE
    assert m % tm == 0
    n_tiles = m // tm
    row = lambda t: (t, 0)
    dest3 = dest.reshape(n_tiles, 1, TOP_K * tm)
    return pl.pallas_call(
        functools.partial(_combine_kernel, nt=n_tiles),
        out_shape=jax.ShapeDtypeStruct((m, d), F32),
        grid=(n_tiles,),
        in_specs=[pl.BlockSpec((1, 1, TOP_K * tm), lambda t: (t, 0, 0), memory_space=pltpu.SMEM),
                  pl.BlockSpec((1, 1, TOP_K * tm), lambda t: (jnp.minimum(t + 1, n_tiles - 1), 0, 0),
                               memory_space=pltpu.SMEM),
                  pl.BlockSpec(memory_space=pl.ANY),
                  pl.BlockSpec((tm, d), row), pl.BlockSpec((tm, LANES), row),
                  pl.BlockSpec((1, d), lambda t: (0, 0))],
        out_specs=pl.BlockSpec((tm, d), row),
        scratch_shapes=[pltpu.VMEM((2, TOP_K * tm, d // LANES, LANES), F32), pltpu.SemaphoreType.DMA((2,))],
        compiler_params=_cparams("arbitrary"),
        name="moe_combine",
    )(dest3, dest3, y_sorted, h2, gates, norm_final.reshape(1, -1))


def _moe_plan(counts_row, n_assign):
    tm = MOE_TILE
    n_tiles = (n_assign + N_EXPERTS * (tm - 1) + tm - 1) // tm
    counts = counts_row[0, :N_EXPERTS].astype(I32)
    tiles_per = (counts + tm - 1) // tm
    tile_end = jnp.cumsum(tiles_per)
    pad_start = (tile_end - tiles_per) * tm
    tile_ids = jnp.arange(n_tiles, dtype=I32)
    used = tile_end[-1]
    tile_valid = (tile_ids < used).astype(I32)
    clamped = jnp.minimum(tile_ids, used - 1)
    tile_expert = jnp.sum((clamped[:, None] >= tile_end[None, :]).astype(I32), axis=1)
    return n_tiles, jnp.minimum(tile_expert, N_EXPERTS - 1), tile_valid, pad_start


def _dest_rows(ids, ranks, pad_start):
    sel = ids[:, :TOP_K, None] == jnp.arange(N_EXPERTS, dtype=I32)
    return ranks[:, :TOP_K] + jnp.sum(jnp.where(sel, pad_start, 0), axis=-1)


def _rope_tables(positions):
    half = MLA_ROPE // 2
    inv_freq = 1.0 / (ROPE_THETA ** (jnp.arange(half, dtype=F32) / half))
    ang = positions.astype(F32)[:, None] * inv_freq[None, :]
    cos, sin = jnp.cos(ang), jnp.sin(ang)
    zeros = jnp.zeros((positions.shape[0], LANES - MLA_ROPE), F32)
    return (jnp.concatenate([cos, cos, zeros], axis=1), jnp.concatenate([-sin, sin, zeros], axis=1))


def _in_proj_weight(w_in):
    o = np.cumsum([0, CONV_CH, GDN_HEADS * GDN_D, GDN_HEADS, GDN_HEADS, MLA_Q_RANK, MLA_KV_RANK, MLA_ROPE])
    conv, z, bl, al, cq, ckv, kpe = (w_in[:, o[i]:o[i + 1]] for i in range(7))
    pad = jnp.zeros((w_in.shape[0], LANES - MLA_ROPE - 2 * GDN_HEADS), w_in.dtype)
    return jnp.concatenate([conv, z, cq, ckv, kpe, bl, al, pad], axis=1).astype(BF16)


IN_SPLITS = (CONV_CH, GDN_HEADS * GDN_D, MLA_Q_RANK, MLA_KV_RANK, LANES)


def _lane_row(vals, lane0):
    return jnp.zeros((1, LANES), F32).at[0, lane0:lane0 + vals.shape[0]].set(vals.astype(F32))


def kernel(x_prompt, x_sample, state_gdn, state_conv, cache_latent, cache_krope, page_table, cache_mem_k, cache_mem_v, mem_prompt, norm_mix, w_in, conv_w, a_log, dt_bias, gdn_norm, q_norm, w_uq, kv_norm, w_uk, w_uv, w_out, norm_x, mem_norm, w_xq, w_xk, w_xv, w_xo, norm_ffn, w_router, b_router, w_e1, b_e1, w_e2, b_e2, norm_final):
    depth = w_in.shape[0]
    assert depth == 1
    L = 0
    bp, t, d = x_prompt.shape
    bs = x_sample.shape[0]
    assert x_sample.shape[1] == 1
    past_len = page_table.shape[1] * cache_latent.shape[2]
    mp = bp * t

    w_in_p = _in_proj_weight(w_in[L])
    alog_row = _lane_row(a_log[L], DECAY_LANE)
    dtb_row = _lane_row(dt_bias[L], DECAY_LANE)
    wuq = w_uq[L]
    wuq_p = jnp.concatenate(
        [wuq, jnp.zeros(wuq.shape[:2] + (MLA_QK_PAD - wuq.shape[2],), wuq.dtype)], axis=2
    ).reshape(wuq.shape[0], MLA_HEADS * MLA_QK_PAD).astype(BF16)
    wuk = w_uk[L].reshape(MLA_KV_RANK, MLA_HEADS * MLA_NOPE).astype(BF16)
    wuv = w_uv[L].reshape(MLA_KV_RANK, MLA_HEADS * MLA_V).astype(BF16)
    wuk_t = jnp.transpose(w_uk[L], (1, 2, 0)).astype(BF16)
    w_out_b = w_out[L].astype(BF16)
    n_gdn = GDN_HEADS * GDN_D
    w_xq_b, w_xo_b = w_xq[L].astype(BF16), w_xo[L].astype(BF16)
    w_xkv_b = jnp.concatenate([w_xk[L], w_xv[L]], axis=1).astype(BF16)
    wr = jnp.concatenate([w_router[L], jnp.zeros((d, LANES - N_EXPERTS), F32)], axis=1)
    wr_hi = wr.astype(BF16)
    wr_lo = (wr - wr_hi.astype(F32)).astype(BF16)
    br_row = _lane_row(b_router[L], 0)
    b1 = b_e1[L][:, None, :]
    b2 = b_e2[L][:, None, :]
    cos_p, sin_p = _rope_tables(jnp.arange(t, dtype=I32))
    cos_s, sin_s = _rope_tables(jnp.full((bs,), past_len, I32))

    xp = x_prompt.reshape(mp, d)
    xs = x_sample.reshape(bs, d)
    conv_p, z_p, cq_p, ckv_p, kba_p = fused_linear([xp], [w_in_p], gain=norm_mix[L], splits=IN_SPLITS, name="in_proj_prompt")
    conv_s, z_s, cq_s, ckv_s, kba_s = fused_linear([xs], [w_in_p], gain=norm_mix[L], splits=IN_SPLITS, name="in_proj_sample")

    y_gdn_p, gdn_state_p = gdn_prompt(conv_p.reshape(bp, t, CONV_CH), z_p.reshape(bp, t, n_gdn),
                                      kba_p.reshape(bp, t, LANES), conv_w[L], alog_row, dtb_row, gdn_norm[L])
    conv_state_p = conv_p.reshape(bp, t, CONV_CH)[:, t - (CONV_WIDTH - 1):, :]
    gdn_state_s, conv_state_s, y_gdn_s = gdn_sample(conv_s, state_conv[L], kba_s, z_s, state_gdn[L],
                                                    conv_w[L], alog_row, dtb_row, gdn_norm[L])

    q_p, k_p, v_p, lat_p, kpe_p = mla_prep(cq_p, ckv_p, kba_p, cos_p, sin_p, q_norm[L], kv_norm[L],
                                           wuq_p, wuk, wuv, seq=t, q_dtype=BF16)
    y_mla_p = mla_flash(q_p.reshape(bp, t, -1), k_p.reshape(bp, t, -1), v_p.reshape(bp, t, -1))
    q_s, _, _, lat_s, kpe_s = mla_prep(cq_s, ckv_s, kba_s, cos_s, sin_s, q_norm[L], kv_norm[L],
                                       wuq_p, wuk, wuv, seq=bs, q_dtype=F32)
    q_s4 = q_s.reshape(bs, MLA_HEADS, MLA_QK_PAD)
    qlat = jnp.concatenate(
        [fused_linear([q_s4[:, h, :MLA_NOPE]], [wuk_t[h]], name=f"absorb_q{h}")[0][:, None, :]
         for h in range(MLA_HEADS)], axis=1)
    head_pad = ((0, 0), (0, DEC_HEAD_PAD - MLA_HEADS), (0, 0))
    qlat8 = jnp.pad(qlat, head_pad)
    qpe8 = jnp.pad(q_s4[:, :, MLA_NOPE:MLA_NOPE + MLA_ROPE], head_pad)
    y_mla_s = mla_decode(page_table, qlat8, qpe8, lat_s, kpe_s, wuv,
                         cache_latent, jnp.swapaxes(cache_krope, 2, 3)).reshape(bs, -1)

    h_p, qx_p = out_proj_xq(y_gdn_p.reshape(mp, n_gdn), y_mla_p.reshape(mp, -1), w_out_b[:n_gdn], w_out_b[n_gdn:],
                            xp, norm_x[L], w_xq_b, name="out_proj_xq_prompt")
    h_s, qx_s = out_proj_xq(y_gdn_s, y_mla_s, w_out_b[:n_gdn], w_out_b[n_gdn:], xs, norm_x[L], w_xq_b,
                            name="out_proj_xq_sample")

    mem_tokens = mem_prompt.shape[1]
    mk_f, mv_f, mk_b, mv_b = fused_linear([mem_prompt.reshape(bp * mem_tokens, d)], [w_xkv_b], gain=mem_norm[L],
                                          splits=(d, d), bf16_copies=True, name="memory_kv")
    ox_p = xattn_prompt(qx_p.reshape(bp, t, d), mk_b.reshape(bp, mem_tokens, d), mv_b.reshape(bp, mem_tokens, d))
    ox_s = xattn_sample(qx_s.reshape(bs, X_HEADS, d // X_HEADS), cache_mem_k[L], cache_mem_v[L])

    h2_p, hn_p, gate_p, ids_p, rank_p, cnt_p = post_xattn(ox_p.reshape(mp, d), w_xo_b, h_p, norm_ffn[L],
                                                          wr_hi, wr_lo, br_row, jnp.zeros((1, LANES), F32))
    h2_s, hn_s, gate_s, ids_s, rank_s, cnt_all = post_xattn(ox_s.reshape(bs, d), w_xo_b, h_s, norm_ffn[L],
                                                            wr_hi, wr_lo, br_row, cnt_p)

    n_tiles, tile_expert, tile_valid, pad_start = _moe_plan(cnt_all, TOP_K * (mp + bs))
    dest_p = _dest_rows(ids_p, rank_p, pad_start)
    dest_s = _dest_rows(ids_s, rank_s, pad_start)
    x_grouped = jnp.zeros((n_tiles * MOE_TILE, d // LANES, LANES), F32)
    x_grouped = moe_dispatch(dest_p, hn_p, x_grouped)
    x_grouped = moe_dispatch(dest_s, hn_s, x_grouped)
    y_sorted = moe_experts(tile_expert, tile_valid, x_grouped, w_e1[L], b1, w_e2[L], b2)
    y_prompt = moe_combine(dest_p, y_sorted, h2_p, gate_p, norm_final)
    y_sample = moe_combine(dest_s, y_sorted, h2_s, gate_s, norm_final)

    x_heads = X_HEADS
    return (y_prompt.reshape(bp, t, d), y_sample.reshape(bs, 1, d),
            gdn_state_p[None], conv_state_p[None],
            lat_p.reshape(bp, 1, t, MLA_KV_RANK), kpe_p.reshape(bp, 1, t, MLA_ROPE),
            mk_f.reshape(1, bp, mem_tokens, x_heads, d // x_heads), mv_f.reshape(1, bp, mem_tokens, x_heads, d // x_heads),
            gdn_state_s[None], conv_state_s[None],
            lat_s.reshape(bs, 1, 1, MLA_KV_RANK), kpe_s.reshape(bs, 1, 1, MLA_ROPE))
```
